```python
import math
import jax, jax.numpy as jnp
from jax import lax
import numpy as np

D_MODEL = 1024
BATCH = 4
SEQ = 4096
DEPTH = 1
DEC_BATCH = 32
DEC_SEQ = 4
PAST_LEN = 16384
PAGE_SIZE = 128

DN_HEADS = 8
DN_HEAD_DIM = 128
DN_WIDTH = DN_HEADS * DN_HEAD_DIM
CONV_W = 4
DN_CONV_CH = 3 * DN_WIDTH
DN_CHUNK = 64
NSA_HEADS = 16
NSA_KV_HEADS = 2
NSA_GROUP = NSA_HEADS // NSA_KV_HEADS
NSA_HEAD_DIM = 64
NSA_WIDTH = NSA_HEADS * NSA_HEAD_DIM
NSA_KV_WIDTH = NSA_KV_HEADS * NSA_HEAD_DIM
CMP_BLOCK = 32
CMP_STRIDE = 16
CMP_RATIO = CMP_BLOCK // CMP_STRIDE
SLC_BLOCK = 64
N_SELECT = 16
WINDOW = 512
Q_BLOCK = 64
N_EXPERTS = 64
TOP_K = 6
N_GROUPS = 8
TOPK_GROUPS = 4
D_EXPERT = 256
D_SHARED = 256
ROUTED_SCALE = 2.5
MOE_BLOCK = 128
D_IN_PROJ = 4 * DN_WIDTH + 2 * DN_HEADS + NSA_WIDTH + 6 * NSA_KV_WIDTH + 3 * NSA_HEADS + 2 * D_MODEL
DEEPNORM_ALPHA = (2.0 * DEPTH) ** 0.25
DEEPNORM_BETA = (8.0 * DEPTH) ** -0.25
LN_EPS = 1e-5
RMS_EPS = 1e-6
NEG = -1e30
FORCE_BONUS = 1e6
F32 = jnp.float32

kernel_name = 'hybrid_gdn_nsa_moe_deepnorm_step'


def _layer_norm(x, g, b):
    xf = x.astype(F32)
    xc = xf - jnp.mean(xf, -1, keepdims=True)
    var = jnp.mean(xc * xc, -1, keepdims=True)
    return (xc * lax.rsqrt(var + LN_EPS) * g.astype(F32) + b.astype(F32)).astype(x.dtype)


def _l2norm(x):
    return x * lax.rsqrt(jnp.sum(x * x, -1, keepdims=True) + 1e-6)


def _masked_softmax(s, valid, axis):
    s = jnp.where(valid, s, NEG)
    m = jnp.max(s, axis=axis, keepdims=True)
    p = jnp.where(valid, jnp.exp(s - m), 0.0)
    return p / jnp.maximum(jnp.sum(p, axis=axis, keepdims=True), 1e-30)


def _alibi_slopes():
    return 2.0 ** (-8.0 * jnp.arange(1, NSA_HEADS + 1, dtype=F32) / NSA_HEADS)


def _split_proj(p):
    sizes = (3 * DN_WIDTH, DN_WIDTH, DN_HEADS, DN_HEADS, NSA_WIDTH,
             2 * NSA_KV_WIDTH, 2 * NSA_KV_WIDTH, 2 * NSA_KV_WIDTH, 3 * NSA_HEADS, 2 * D_MODEL)
    return jnp.split(p, np.cumsum(sizes)[:-1].tolist(), axis=-1)


def _to_kv(t):
    return t.reshape(t.shape[:2] + (2, NSA_KV_HEADS, NSA_HEAD_DIM))


def _causal_conv(x, buf, w):
    T = x.shape[1]
    xp = jnp.concatenate([buf.astype(x.dtype), x], axis=1)
    y = xp[:, 0:T] * w[0]
    for j in range(1, CONV_W):
        y = y + xp[:, j:j + T] * w[j]
    return jax.nn.silu(y), xp[:, -(CONV_W - 1):]


def _gated_delta_chunked(q, k, v, g, beta, S0):
    Bsz, T, H, _ = q.shape
    dv = v.shape[-1]
    C = DN_CHUNK
    n = -(-T // C)
    pad = n * C - T

    def prep(x):
        x = jnp.pad(x, [(0, 0), (0, pad)] + [(0, 0)] * (x.ndim - 2))
        x = x.reshape((Bsz, n, C) + x.shape[2:])
        return jnp.moveaxis(jnp.moveaxis(x, 1, 0), 2, 3)

    qc, kc, vc, gc, bc = prep(q), prep(k), prep(v), prep(g), prep(beta)
    gcum = jnp.cumsum(gc, axis=-1)
    idx = jnp.arange(C)
    incl = idx[:, None] >= idx[None, :]
    strict = idx[:, None] > idx[None, :]
    diff = gcum[..., :, None] - gcum[..., None, :]
    decay = jnp.where(incl, jnp.exp(jnp.where(incl, diff, 0.0)), 0.0)
    kb = kc * bc[..., None]
    vb = vc * bc[..., None]
    Lm = jnp.where(strict, jnp.einsum('nbhid,nbhjd->nbhij', kb, kc) * decay, 0.0)
    eye = jnp.broadcast_to(jnp.eye(C, dtype=F32), Lm.shape)
    Tm = lax.linalg.triangular_solve(Lm, eye, left_side=True, lower=True, unit_diagonal=True)
    u = jnp.einsum('nbhij,nbhjd->nbhid', Tm, vb)
    w = jnp.einsum('nbhij,nbhjd->nbhid', Tm, kb * jnp.exp(gcum)[..., None])
    A = jnp.where(incl, jnp.einsum('nbhid,nbhjd->nbhij', qc, kc) * decay, 0.0)

    def step(S, xs):
        q_i, k_i, u_i, w_i, A_i, g_i = xs
        v_new = u_i - jnp.einsum('bhcd,bhde->bhce', w_i, S)
        o_i = (jnp.einsum('bhcd,bhde->bhce', q_i * jnp.exp(g_i)[..., None], S)
               + jnp.einsum('bhij,bhje->bhie', A_i, v_new))
        g_last = g_i[..., -1]
        S = (S * jnp.exp(g_last)[..., None, None]
             + jnp.einsum('bhcd,bhce->bhde', k_i * jnp.exp(g_last[..., None] - g_i)[..., None], v_new))
        return S, o_i

    S_final, o = lax.scan(step, S0, (qc, kc, u, w, A, gcum))
    o = jnp.swapaxes(jnp.moveaxis(o, 0, 1), 2, 3).reshape(Bsz, n * C, H, dv)[:, :T]
    return o, S_final


def _deltanet_mixer(qkv_raw, z, a, b, conv_buf, S0, conv_w, A_log, dt_bias, norm_w):
    Bsz, T, _ = qkv_raw.shape
    qkv, conv_new = _causal_conv(qkv_raw, conv_buf, conv_w)
    qkv = qkv.astype(F32).reshape(Bsz, T, 3, DN_HEADS, DN_HEAD_DIM)
    q = _l2norm(qkv[:, :, 0]) * DN_HEAD_DIM ** -0.5
    k = _l2norm(qkv[:, :, 1])
    v = qkv[:, :, 2]
    beta = jax.nn.sigmoid(b.astype(F32))
    g = -jnp.exp(A_log.astype(F32)) * jax.nn.softplus(a.astype(F32) + dt_bias.astype(F32))
    o, S_new = _gated_delta_chunked(q, k, v, g, beta, S0.astype(F32))
    o = o * lax.rsqrt(jnp.mean(o * o, -1, keepdims=True) + RMS_EPS) * norm_w.astype(F32)
    o = o * jax.nn.silu(z.astype(F32).reshape(Bsz, T, DN_HEADS, DN_HEAD_DIM))
    return o.reshape(Bsz, T, DN_WIDTH), S_new.astype(S0.dtype), conv_new


def _nsa_compress(kv, w1, pos, w2):
    Bsz, L = kv.shape[:2]
    n_sub = L // CMP_STRIDE
    nc = n_sub - CMP_RATIO + 1
    sub = kv[:, :n_sub * CMP_STRIDE].reshape(Bsz, n_sub, CMP_STRIDE, 2, NSA_KV_HEADS, NSA_HEAD_DIM)
    w1r = w1.reshape(2, CMP_RATIO, CMP_STRIDE, NSA_HEAD_DIM, NSA_HEAD_DIM)
    proj = jnp.einsum('bnpshd,srpde->bnshre', sub.astype(F32), w1r.astype(F32))
    bias = jnp.einsum('sf,sfe->se', pos.reshape(2, -1).astype(F32), w1.astype(F32))
    h = bias[None, None, :, None, :]
    for r in range(CMP_RATIO):
        h = h + proj[:, r:r + nc, :, :, r]
    out = jnp.einsum('bnshe,sef->bnshf', jax.nn.gelu(h), w2.astype(F32))
    return out[:, :, 0], out[:, :, 1]


def _nsa_core(q, q_pos, k_c, v_c, gather_slc, n_slc, k_w, v_w, w_pos, gates):
    Bsz, Tq = q.shape[:2]
    qf = q.astype(F32).reshape(Bsz, Tq, NSA_KV_HEADS, NSA_GROUP, NSA_HEAD_DIM) * NSA_HEAD_DIM ** -0.5
    slopes = _alibi_slopes().reshape(NSA_KV_HEADS, NSA_GROUP)[None, :, :, None, None]
    tq = q_pos.astype(F32)
    n_cmp = k_c.shape[1]
    c_start = jnp.arange(n_cmp) * CMP_STRIDE
    c_end = c_start + CMP_BLOCK - 1
    s_c = jnp.einsum('bqhgd,bchd->bhgqc', qf, k_c) - slopes * (tq[:, None] - c_end.astype(F32)[None, :])
    p_c = _masked_softmax(s_c, c_end[None, :] <= q_pos[:, None], -1)
    o_c = jnp.einsum('bhgqc,bchd->bqhgd', p_c, v_c)
    s_start = jnp.arange(n_slc) * SLC_BLOCK
    cover = ((c_start[:, None] < s_start[None, :] + SLC_BLOCK) & (c_end[:, None] >= s_start[None, :])).astype(F32)
    imp = jnp.einsum('bhgqc,cs->bhqs', p_c, cover)
    cur = q_pos // SLC_BLOCK
    blk = jnp.arange(n_slc)
    forced = (blk[None, :] == 0) | (blk[None, :] == cur[:, None]) | (blk[None, :] == cur[:, None] - 1)
    score = jnp.where(s_start[None, :] <= q_pos[:, None], imp + jnp.where(forced, FORCE_BONUS, 0.0), NEG)
    _, sel = lax.top_k(score, min(N_SELECT, n_slc))
    kv_s = gather_slc(sel).astype(F32)
    pos_s = sel[..., None] * SLC_BLOCK + jnp.arange(SLC_BLOCK)
    dist_s = (tq[None, None, :, None, None] - pos_s.astype(F32))[:, :, None]
    s_s = jnp.einsum('bqhgd,bhqnkd->bhgqnk', qf, kv_s[..., 0, :]) - slopes[..., None] * dist_s
    p_s = _masked_softmax(s_s, dist_s >= 0, (-2, -1))
    o_s = jnp.einsum('bhgqnk,bhqnkd->bqhgd', p_s, kv_s[..., 1, :])
    dist_w = q_pos[:, None] - w_pos[None, :]
    valid_w = (dist_w >= 0) & (dist_w < WINDOW) & (w_pos[None, :] >= 0)
    s_w = jnp.einsum('bqhgd,bwhd->bhgqw', qf, k_w.astype(F32)) - slopes * dist_w.astype(F32)
    p_w = _masked_softmax(s_w, valid_w, -1)
    o_w = jnp.einsum('bhgqw,bwhd->bqhgd', p_w, v_w.astype(F32))
    gt = jax.nn.sigmoid(gates.astype(F32)).reshape(Bsz, Tq, NSA_KV_HEADS, NSA_GROUP, 3)
    o = gt[..., 0:1] * o_c + gt[..., 1:2] * o_s + gt[..., 2:3] * o_w
    return o.reshape(Bsz, Tq, NSA_WIDTH)


def _merge_branches(o_dn, o_nsa, mg):
    g_dn, g_nsa = jnp.split(jax.nn.sigmoid(mg.astype(F32)), 2, axis=-1)
    return (g_dn * o_dn.astype(F32) + g_nsa * o_nsa.astype(F32)).astype(mg.dtype)


def _mixer_prompt(x, w_in, conv_w, A_log, dt_bias, norm_w, cmp_w1, cmp_pos, cmp_w2):
    Bsz, T, _ = x.shape
    qkv, z, a, b, nq, kvc, kvs, kvw, ng, mg = _split_proj(x @ w_in)
    conv0 = jnp.zeros((Bsz, CONV_W - 1, DN_CONV_CH), x.dtype)
    S0 = jnp.zeros((Bsz, DN_HEADS, DN_HEAD_DIM, DN_HEAD_DIM), F32)
    o_dn, S_new, conv_new = _deltanet_mixer(qkv, z, a, b, conv0, S0, conv_w, A_log, dt_bias, norm_w)
    kvc, kvs, kvw = _to_kv(kvc), _to_kv(kvs), _to_kv(kvw)
    k_c, v_c = _nsa_compress(kvc, cmp_w1, cmp_pos, cmp_w2)
    n_slc = -(-T // SLC_BLOCK)
    blocks = jnp.pad(kvs, ((0, 0), (0, n_slc * SLC_BLOCK - T), (0, 0), (0, 0), (0, 0)))
    blocks = blocks.reshape(Bsz, n_slc, SLC_BLOCK, 2, NSA_KV_HEADS, NSA_HEAD_DIM)
    bi = jnp.arange(Bsz)[:, None, None, None]
    hi = jnp.arange(NSA_KV_HEADS)[None, :, None, None]

    def gather_slc(sel):
        return blocks[bi, sel, :, :, hi]

    kvw_pad = jnp.pad(kvw, ((0, 0), (WINDOW, 0), (0, 0), (0, 0), (0, 0)))
    q = nq.reshape(Bsz, T, NSA_HEADS, NSA_HEAD_DIM)
    gates = ng.reshape(Bsz, T, NSA_HEADS, 3)

    def one_block(j):
        q0 = j * Q_BLOCK
        qb = lax.dynamic_slice_in_dim(q, q0, Q_BLOCK, axis=1)
        gb = lax.dynamic_slice_in_dim(gates, q0, Q_BLOCK, axis=1)
        wb = lax.dynamic_slice_in_dim(kvw_pad, q0, WINDOW + Q_BLOCK, axis=1)
        q_pos = q0 + jnp.arange(Q_BLOCK)
        w_pos = q0 - WINDOW + jnp.arange(WINDOW + Q_BLOCK)
        return _nsa_core(qb, q_pos, k_c, v_c, gather_slc, n_slc, wb[:, :, 0], wb[:, :, 1], w_pos, gb)

    o_nsa = lax.map(one_block, jnp.arange(T // Q_BLOCK))
    o_nsa = jnp.moveaxis(o_nsa, 0, 1).reshape(Bsz, T, NSA_WIDTH)
    h = _merge_branches(o_dn, o_nsa, mg)
    return h, (kvc, kvs, kvw[:, -min(WINDOW, T):], S_new, conv_new)


def _paged_block_gather(cache_slc_kv, l, page_table, kv_new):
    Bsz, T = kv_new.shape[:2]
    bpp = PAGE_SIZE // SLC_BLOCK
    n_past_blocks = page_table.shape[1] * bpp
    pool = cache_slc_kv.reshape(cache_slc_kv.shape[:2] + (bpp, SLC_BLOCK) + cache_slc_kv.shape[3:])
    n_tail = -(-T // SLC_BLOCK)
    tail = jnp.pad(kv_new, ((0, 0), (0, n_tail * SLC_BLOCK - T), (0, 0), (0, 0), (0, 0)))
    tail = tail.reshape(Bsz, n_tail, SLC_BLOCK, 2, NSA_KV_HEADS, NSA_HEAD_DIM)
    bi = jnp.arange(Bsz)[:, None, None, None]
    hi = jnp.arange(NSA_KV_HEADS)[None, :, None, None]

    def gather(sel):
        jp = jnp.minimum(sel, n_past_blocks - 1)
        phys = page_table[bi, jp // bpp]
        from_pool = pool[l, phys, jp % bpp, :, :, hi]
        jt = jnp.clip(sel - n_past_blocks, 0, n_tail - 1)
        from_tail = tail[bi, jt, :, :, hi]
        return jnp.where((sel < n_past_blocks)[..., None, None, None], from_pool, from_tail)

    return gather, n_past_blocks + n_tail


def _mixer_sample(x, l, cache_cmp_kv, cache_slc_kv, win_buf, S0, conv_buf, page_table,
                  w_in, conv_w, A_log, dt_bias, norm_w, cmp_w1, cmp_pos, cmp_w2):
    Bsz, T, _ = x.shape
    past = page_table.shape[1] * PAGE_SIZE
    qkv, z, a, b, nq, kvc, kvs, kvw, ng, mg = _split_proj(x @ w_in)
    o_dn, S_new, conv_new = _deltanet_mixer(qkv, z, a, b, conv_buf, S0, conv_w, A_log, dt_bias, norm_w)
    kvc, kvs, kvw = _to_kv(kvc), _to_kv(kvs), _to_kv(kvw)
    cmp_past = cache_cmp_kv[l, page_table].reshape((Bsz, past) + kvc.shape[2:])
    k_c, v_c = _nsa_compress(jnp.concatenate([cmp_past.astype(kvc.dtype), kvc], axis=1), cmp_w1, cmp_pos, cmp_w2)
    gather_slc, n_slc = _paged_block_gather(cache_slc_kv, l, page_table, kvs)
    kvw_all = jnp.concatenate([win_buf.astype(kvw.dtype), kvw], axis=1)
    w_pos = past - win_buf.shape[1] + jnp.arange(kvw_all.shape[1])
    q_pos = past + jnp.arange(T)
    o_nsa = _nsa_core(nq.reshape(Bsz, T, NSA_HEADS, NSA_HEAD_DIM), q_pos, k_c, v_c, gather_slc, n_slc,
                      kvw_all[:, :, 0], kvw_all[:, :, 1], w_pos, ng.reshape(Bsz, T, NSA_HEADS, 3))
    h = _merge_branches(o_dn, o_nsa, mg)
    return h, (kvc, kvs, kvw_all[:, -min(WINDOW, past + T):], S_new, conv_new)


def _moe_ffn(x, w_router, b_router, w_gate, w_up, w_down, ws_gate, ws_up, ws_down):
    shp = x.shape
    xt = x.reshape(-1, D_MODEL)
    T = xt.shape[0]
    scores = jax.nn.sigmoid(jnp.einsum('td,de->te', xt.astype(F32), w_router.astype(F32)))
    sel = scores + b_router.astype(F32)
    grp_score = jnp.sum(lax.top_k(sel.reshape(T, N_GROUPS, N_EXPERTS // N_GROUPS), 2)[0], -1)
    _, top_g = lax.top_k(grp_score, TOPK_GROUPS)
    gmask = jnp.any(top_g[..., None] == jnp.arange(N_GROUPS), axis=1)
    sel = jnp.where(jnp.repeat(gmask, N_EXPERTS // N_GROUPS, axis=1), sel, NEG)
    _, idx = lax.top_k(sel, TOP_K)
    wts = jnp.take_along_axis(scores, idx, axis=1)
    wts = wts / jnp.sum(wts, -1, keepdims=True) * ROUTED_SCALE
    flat_e = idx.reshape(-1)
    order = jnp.argsort(flat_e)
    e_sorted = flat_e[order]
    counts = jnp.bincount(flat_e, length=N_EXPERTS)
    padded = (counts + MOE_BLOCK - 1) // MOE_BLOCK * MOE_BLOCK
    pad_end = jnp.cumsum(padded)
    pad_start = pad_end - padded
    start = jnp.cumsum(counts) - counts
    slot = pad_start[e_sorted] + jnp.arange(T * TOP_K) - start[e_sorted]
    n_blocks = -(-(T * TOP_K) // MOE_BLOCK) + N_EXPERTS
    slot_tok = jnp.full((n_blocks * MOE_BLOCK,), T, jnp.int32).at[slot].set((order // TOP_K).astype(jnp.int32))
    slot_w = jnp.zeros((n_blocks * MOE_BLOCK,), F32).at[slot].set(wts.reshape(-1)[order])
    blk_exp = jnp.minimum(jnp.sum(pad_end[None, :] <= (jnp.arange(n_blocks) * MOE_BLOCK)[:, None], axis=1),
                          N_EXPERTS - 1)
    x_pad = jnp.concatenate([xt, jnp.zeros((1, D_MODEL), xt.dtype)], axis=0)

    def expert_block(args):
        tok, e = args
        xb = x_pad[tok]
        hb = jax.nn.silu(xb @ w_gate[e]) * (xb @ w_up[e])
        return hb @ w_down[e]

    y_slots = lax.map(expert_block, (slot_tok.reshape(n_blocks, MOE_BLOCK), blk_exp)).reshape(-1, D_MODEL)
    routed = jax.ops.segment_sum(y_slots.astype(F32) * slot_w[:, None], slot_tok, num_segments=T + 1)[:T]
    shared = ((jax.nn.silu(xt @ ws_gate) * (xt @ ws_up)) @ ws_down).astype(F32)
    return (routed + shared).astype(x.dtype).reshape(shp)


def setup_inputs(seed: int = 0) -> dict:
    key = jax.random.key(seed)
    ks = jax.random.split(key, 32)
    n_pages = PAST_LEN // PAGE_SIZE
    n_pool = (5 * DEC_BATCH * n_pages + 3) // 4
    win_buf = min(WINDOW, PAST_LEN)
    kvs = (2, NSA_KV_HEADS, NSA_HEAD_DIM)

    def nrm(k, shape, scale):
        return scale * jax.random.normal(k, shape, F32)

    page_table = jax.random.permutation(ks[7], n_pool)[:DEC_BATCH * n_pages]
    page_table = page_table.reshape(DEC_BATCH, n_pages).astype(jnp.int32)
    dt = jnp.exp(jax.random.uniform(ks[10], (DEPTH, DN_HEADS), F32, math.log(1e-3), math.log(1e-1)))
    return {
        'x_prompt': nrm(ks[0], (BATCH, SEQ, D_MODEL), 1.0),
        'x_sample': nrm(ks[1], (DEC_BATCH, DEC_SEQ, D_MODEL), 1.0),
        'cache_cmp_kv': nrm(ks[2], (DEPTH, n_pool, PAGE_SIZE) + kvs, 1.0),
        'cache_slc_kv': nrm(ks[3], (DEPTH, n_pool, PAGE_SIZE) + kvs, 1.0),
        'cache_win_kv': nrm(ks[4], (DEPTH, DEC_BATCH, win_buf) + kvs, 1.0),
        'state_delta_S': nrm(ks[5], (DEPTH, DEC_BATCH, DN_HEADS, DN_HEAD_DIM, DN_HEAD_DIM), DN_HEAD_DIM ** -0.5),
        'state_delta_conv': nrm(ks[6], (DEPTH, DEC_BATCH, CONV_W - 1, DN_CONV_CH), 1.0),
        'page_table': page_table,
        'w_in': nrm(ks[8], (DEPTH, D_MODEL, D_IN_PROJ), D_MODEL ** -0.5),
        'dn_conv_w': nrm(ks[9], (DEPTH, CONV_W, DN_CONV_CH), CONV_W ** -0.5),
        'dn_A_log': jnp.log(jax.random.uniform(ks[11], (DEPTH, DN_HEADS), F32, 1.0, 16.0)),
        'dn_dt_bias': dt + jnp.log(-jnp.expm1(-dt)),
        'dn_norm_w': 1.0 + nrm(ks[12], (DEPTH, DN_HEAD_DIM), 0.1),
        'nsa_cmp_w1': nrm(ks[13], (DEPTH, 2, CMP_BLOCK * NSA_HEAD_DIM, NSA_HEAD_DIM), (CMP_BLOCK * NSA_HEAD_DIM) ** -0.5),
        'nsa_cmp_pos': nrm(ks[14], (DEPTH, 2, CMP_BLOCK, NSA_HEAD_DIM), 0.1),
        'nsa_cmp_w2': nrm(ks[15], (DEPTH, 2, NSA_HEAD_DIM, NSA_HEAD_DIM), NSA_HEAD_DIM ** -0.5),
        'w_out': nrm(ks[16], (DEPTH, D_MODEL, D_MODEL), D_MODEL ** -0.5 * DEEPNORM_BETA),
        'ln1_g': 1.0 + nrm(ks[17], (DEPTH, D_MODEL), 0.1),
        'ln1_b': nrm(ks[18], (DEPTH, D_MODEL), 0.02),
        'w_router': nrm(ks[19], (DEPTH, D_MODEL, N_EXPERTS), D_MODEL ** -0.5),
        'b_router': nrm(ks[20], (DEPTH, N_EXPERTS), 0.01),
        'w_exp_gate': nrm(ks[21], (DEPTH, N_EXPERTS, D_MODEL, D_EXPERT), D_MODEL ** -0.5),
        'w_exp_up': nrm(ks[22], (DEPTH, N_EXPERTS, D_MODEL, D_EXPERT), D_MODEL ** -0.5),
        'w_exp_down': nrm(ks[23], (DEPTH, N_EXPERTS, D_EXPERT, D_MODEL), D_EXPERT ** -0.5 * DEEPNORM_BETA),
        'w_sh_gate': nrm(ks[24], (DEPTH, D_MODEL, D_SHARED), D_MODEL ** -0.5),
        'w_sh_up': nrm(ks[25], (DEPTH, D_MODEL, D_SHARED), D_MODEL ** -0.5),
        'w_sh_down': nrm(ks[26], (DEPTH, D_SHARED, D_MODEL), D_SHARED ** -0.5 * DEEPNORM_BETA),
        'ln2_g': 1.0 + nrm(ks[27], (DEPTH, D_MODEL), 0.1),
        'ln2_b': nrm(ks[28], (DEPTH, D_MODEL), 0.02),
    }


def reference(x_prompt, x_sample, cache_cmp_kv, cache_slc_kv, cache_win_kv, state_delta_S, state_delta_conv,
              page_table, w_in, dn_conv_w, dn_A_log, dn_dt_bias, dn_norm_w, nsa_cmp_w1, nsa_cmp_pos, nsa_cmp_w2,
              w_out, ln1_g, ln1_b, w_router, b_router, w_exp_gate, w_exp_up, w_exp_down,
              w_sh_gate, w_sh_up, w_sh_down, ln2_g, ln2_b):
    xp, xs = x_prompt, x_sample
    st_p = ([], [], [], [], [])
    st_s = ([], [], [], [], [])
    for l in range(DEPTH):
        mix_w = (w_in[l], dn_conv_w[l], dn_A_log[l], dn_dt_bias[l], dn_norm_w[l],
                 nsa_cmp_w1[l], nsa_cmp_pos[l], nsa_cmp_w2[l])
        moe_w = (w_router[l], b_router[l], w_exp_gate[l], w_exp_up[l], w_exp_down[l],
                 w_sh_gate[l], w_sh_up[l], w_sh_down[l])
        hp, new_p = _mixer_prompt(xp, *mix_w)
        hs, new_s = _mixer_sample(xs, l, cache_cmp_kv, cache_slc_kv, cache_win_kv[l], state_delta_S[l],
                                  state_delta_conv[l], page_table, *mix_w)
        xp = _layer_norm(DEEPNORM_ALPHA * xp + hp @ w_out[l], ln1_g[l], ln1_b[l])
        xs = _layer_norm(DEEPNORM_ALPHA * xs + hs @ w_out[l], ln1_g[l], ln1_b[l])
        xp = _layer_norm(DEEPNORM_ALPHA * xp + _moe_ffn(xp, *moe_w), ln2_g[l], ln2_b[l])
        xs = _layer_norm(DEEPNORM_ALPHA * xs + _moe_ffn(xs, *moe_w), ln2_g[l], ln2_b[l])
        for acc, t in zip(st_p, new_p):
            acc.append(t)
        for acc, t in zip(st_s, new_s):
            acc.append(t)
    cmp_kv_p, slc_kv_p, win_kv_p, delta_S_p, delta_conv_p = [jnp.stack(a) for a in st_p]
    cmp_kv_s, slc_kv_s, win_kv_s, delta_S_s, delta_conv_s = [jnp.stack(a) for a in st_s]
    return (xp, xs, cmp_kv_p, slc_kv_p, win_kv_p, delta_S_p, delta_conv_p,
            cmp_kv_s, slc_kv_s, win_kv_s, delta_S_s, delta_conv_s)
```

```python
import functools
import math

import jax
import jax.numpy as jnp
import numpy as np
from jax import lax
from jax.experimental import pallas as pl
from jax.experimental.pallas import tpu as pltpu

F32 = jnp.float32
BF16 = jnp.bfloat16
I32 = jnp.int32
HIGHEST = lax.Precision.HIGHEST

D_MODEL = 1024
PAGE_SIZE = 128
DN_HEADS = 8
DN_HEAD_DIM = 128
DN_WIDTH = DN_HEADS * DN_HEAD_DIM
CONV_W = 4
DN_CHUNK = 64
NSA_HEADS = 16
NSA_KV_HEADS = 2
NSA_GROUP = NSA_HEADS // NSA_KV_HEADS
NSA_HEAD_DIM = 64
NSA_WIDTH = NSA_HEADS * NSA_HEAD_DIM
NSA_KV_WIDTH = NSA_KV_HEADS * NSA_HEAD_DIM
CMP_BLOCK = 32
CMP_STRIDE = 16
SLC_BLOCK = 64
N_SELECT = 16
WINDOW = 512
N_EXPERTS = 64
TOP_K = 6
N_GROUPS = 8
TOPK_GROUPS = 4
D_EXPERT = 256
ROUTED_SCALE = 2.5
LN_EPS = 1e-5
RMS_EPS = 1e-6
NEG = -1e30
FORCE_BONUS = 1e6

C_QKV = 0
C_Z = 3072
C_NQ = 4096
C_MG = 5120
C_KVC = 7168
C_KVS = 7424
C_KVW = 7680
C_SMALL = 7936
P_COLS = 8064
SM_A = 0
SM_B = DN_HEADS
SM_NG = 2 * DN_HEADS

LANES = 128
VMEM_LIMIT = 48 * 1024 * 1024


def _cparams(sem):
    return pltpu.CompilerParams(dimension_semantics=sem, vmem_limit_bytes=VMEM_LIMIT)


def _bdot(a, b):
    return jnp.dot(a.astype(BF16), b.astype(BF16), preferred_element_type=F32)


def _bdot_nt(a, b):
    return lax.dot_general(a.astype(BF16), b.astype(BF16), (((1,), (1,)), ((), ())),
                           preferred_element_type=F32)


def _hdot(a, b):
    return jnp.dot(a, b, precision=HIGHEST, preferred_element_type=F32)


def _hdot_nt(a, b):
    return lax.dot_general(a, b, (((1,), (1,)), ((), ())), precision=HIGHEST,
                           preferred_element_type=F32)


def _hdot_tn(a, b):
    return lax.dot_general(a, b, (((0,), (0,)), ((), ())), precision=HIGHEST,
                           preferred_element_type=F32)


def _sigmoid(x):
    return 1.0 / (1.0 + jnp.exp(-x))


def _silu(x):
    return x * _sigmoid(x)


def _softplus(x):
    return jnp.maximum(x, 0.0) + jnp.log(1.0 + jnp.exp(-jnp.abs(x)))


def _iota(shape, dim):
    return lax.broadcasted_iota(I32, shape, dim)


def _mm_kernel(x_ref, w_ref, o_ref):
    o_ref[...] = jnp.dot(x_ref[...].astype(BF16), w_ref[...], preferred_element_type=F32)


def _matmul(x, w_bf16, tm, tn):
    m, k = x.shape
    n = w_bf16.shape[1]
    assert m % tm == 0 and n % tn == 0
    return pl.pallas_call(
        _mm_kernel,
        grid=(n // tn, m // tm),
        in_specs=[pl.BlockSpec((tm, k), lambda j, i: (i, 0)),
                  pl.BlockSpec((k, tn), lambda j, i: (0, j))],
        out_specs=pl.BlockSpec((tm, tn), lambda j, i: (i, j)),
        out_shape=jax.ShapeDtypeStruct((m, n), F32),
        compiler_params=_cparams(("parallel", "parallel")),
        name="dense_matmul",
    )(x, w_bf16)


def _reorder_w_in(w_in):
    o = 0
    seg = {}
    for name, size in (("qkv", 3 * DN_WIDTH), ("z", DN_WIDTH), ("a", DN_HEADS), ("b", DN_HEADS),
                       ("nq", NSA_WIDTH), ("kvc", 2 * NSA_KV_WIDTH), ("kvs", 2 * NSA_KV_WIDTH),
                       ("kvw", 2 * NSA_KV_WIDTH), ("ng", 3 * NSA_HEADS), ("mg", 2 * D_MODEL)):
        seg[name] = w_in[:, o:o + size]
        o += size
    assert o == w_in.shape[1]
    pad = jnp.zeros((w_in.shape[0], P_COLS - C_SMALL - SM_NG - 3 * NSA_HEADS), w_in.dtype)
    w = jnp.concatenate([seg["qkv"], seg["z"], seg["nq"], seg["mg"], seg["kvc"], seg["kvs"], seg["kvw"],
                         seg["a"], seg["b"], seg["ng"], pad], axis=1)
    assert w.shape[1] == P_COLS
    return w.astype(BF16)


def _tri_inverse(lmat, c):
    r = _iota((c, c), 0)
    q = _iota((c, c), 1)
    eye = (r == q).astype(F32)
    blk = min(16, c)
    shift = int(math.log2(blk))
    same = (r >> shift) == (q >> shift)
    dmat = jnp.where(same, lmat, 0.0)
    prod = eye - dmat
    dpow = dmat
    k = 2
    while k < blk:
        dpow = _hdot(dpow, dpow)
        prod = prod + _hdot(prod, dpow)
        k *= 2
    if c == blk:
        return prod
    nmat = lmat - dmat
    mmat = _hdot(prod, nmat)
    outer = eye - mmat
    mpow = mmat
    k = 2
    while k < c // blk:
        mpow = _hdot(mpow, mpow)
        outer = outer + _hdot(outer, mpow)
        k *= 2
    return _hdot(outer, prod)


def _dn_kernel(q_ref, k_ref, v_ref, z_ref, sm_ref, hq_ref, hk_ref, hv_ref, cwq_ref, cwk_ref, cwv_ref,
               hp_ref, nw_ref, s0_ref, o_ref, sout_ref, s_scr, xp_scr, qn_scr, kn_scr, vn_scr, gb_scr,
               *, tb, c, t_valid):
    h = pl.program_id(1)
    t = pl.program_id(2)
    nt = pl.num_programs(2)

    @pl.when(t == 0)
    def _():
        s_scr[...] = s0_ref[0, 0]
        xp_scr[0, 0:8, :] = hq_ref[0]
        xp_scr[1, 0:8, :] = hk_ref[0]
        xp_scr[2, 0:8, :] = hv_ref[0]

    rows = t * tb + _iota((tb, 1), 0)
    valid = rows < t_valid

    def conv(i, raw_ref, cw_ref):
        xp_scr[i, 8:8 + tb, :] = raw_ref[0]
        acc = xp_scr[i, 8:8 + tb, :] * cw_ref[CONV_W - 1:CONV_W, :]
        for j in range(CONV_W - 1):
            acc = acc + xp_scr[i, 8 - (CONV_W - 1) + j:8 - (CONV_W - 1) + j + tb, :] * cw_ref[j:j + 1, :]
        tail = xp_scr[i, tb:tb + 8, :]
        xp_scr[i, 0:8, :] = tail
        return _silu(acc)

    qc = conv(0, q_ref, cwq_ref)
    kc = conv(1, k_ref, cwk_ref)
    vc = conv(2, v_ref, cwv_ref)
    qn = qc * lax.rsqrt(jnp.sum(qc * qc, -1, keepdims=True) + 1e-6) * (DN_HEAD_DIM ** -0.5)
    kn = kc * lax.rsqrt(jnp.sum(kc * kc, -1, keepdims=True) + 1e-6)
    lane = _iota((1, LANES), 1)
    sm = sm_ref[0]
    a_h = jnp.sum(jnp.where(lane == SM_A + h, sm, 0.0), -1, keepdims=True)
    b_h = jnp.sum(jnp.where(lane == SM_B + h, sm, 0.0), -1, keepdims=True)
    neg_a = -jnp.exp(jnp.sum(jnp.where(lane == h, hp_ref[0:1, :], 0.0), -1, keepdims=True))
    dtb = jnp.sum(jnp.where(lane == h, hp_ref[1:2, :], 0.0), -1, keepdims=True)
    g = neg_a * _softplus(a_h + dtb)
    beta = _sigmoid(b_h)
    qn_scr[...] = jnp.where(valid, qn, 0.0)
    kn_scr[...] = jnp.where(valid, kn, 0.0)
    vn_scr[...] = jnp.where(valid, vc, 0.0)
    gb_scr[...] = jnp.where(lane == 0, jnp.where(valid, g, 0.0), jnp.where(valid, beta, 0.0))

    r = _iota((c, c), 0)
    q = _iota((c, c), 1)
    incl = r >= q
    strict = r > q
    nw = nw_ref[...]

    def chunk(ci, carry):
        r0 = pl.multiple_of(ci * c, c)
        qi = qn_scr[pl.ds(r0, c), :]
        ki = kn_scr[pl.ds(r0, c), :]
        vi = vn_scr[pl.ds(r0, c), :]
        gb = gb_scr[pl.ds(r0, c), :]
        gi = gb[:, 0:1]
        bi = gb[:, 1:2]
        g_row = jnp.sum(jnp.where(r == q, gi, 0.0), 0, keepdims=True)
        gcum_col = jnp.sum(jnp.where(incl, g_row, 0.0), 1, keepdims=True)
        gcum_row = jnp.sum(jnp.where(r <= q, gi, 0.0), 0, keepdims=True)
        decay = jnp.where(incl, jnp.exp(jnp.where(incl, gcum_col - gcum_row, 0.0)), 0.0)
        kb = ki * bi
        vb = vi * bi
        lmat = jnp.where(strict, _hdot_nt(kb, ki) * decay, 0.0)
        tm = _tri_inverse(lmat, c)
        eg = jnp.exp(gcum_col)
        u = _hdot(tm, vb)
        w = _hdot(tm, kb * eg)
        amat = jnp.where(incl, _hdot_nt(qi, ki) * decay, 0.0)
        s = s_scr[...]
        v_new = u - _hdot(w, s)
        o = _hdot(qi * eg, s) + _hdot(amat, v_new)
        g_last = gcum_col[c - 1:c, :]
        s_scr[...] = s * jnp.exp(g_last) + _hdot_tn(ki * jnp.exp(g_last - gcum_col), v_new)
        o = o * lax.rsqrt(jnp.mean(o * o, -1, keepdims=True) + RMS_EPS) * nw
        o_ref[0, pl.ds(r0, c), :] = o * _silu(z_ref[0, pl.ds(r0, c), :])
        return carry

    lax.fori_loop(0, tb // c, chunk, 0)

    @pl.when(t == nt - 1)
    def _():
        sout_ref[0, 0] = s_scr[...]


def _deltanet(p3, hist, s0, conv_w, a_log, dt_bias, norm_w, *, t_valid, tb, c):
    bsz, tpad, _ = p3.shape
    assert tpad % tb == 0 and tb % c == 0 and tb % 8 == 0
    nt = tpad // tb
    cw = jnp.concatenate([conv_w, jnp.zeros((8 - CONV_W, conv_w.shape[1]), F32)], 0)
    hp = jnp.zeros((8, LANES), F32).at[0, :DN_HEADS].set(a_log).at[1, :DN_HEADS].set(dt_bias)
    nw = norm_w.reshape(1, DN_HEAD_DIM)
    nb = DN_WIDTH // LANES

    def col(off):
        return lambda b, h, t: (b, t, off + h)

    def hcol(off):
        return lambda b, h, t: (b, 0, off + h)

    def wcol(off):
        return lambda b, h, t: (0, off + h)

    tok = lambda off: pl.BlockSpec((1, tb, LANES), col(off))
    kern = functools.partial(_dn_kernel, tb=tb, c=c, t_valid=t_valid)
    return pl.pallas_call(
        kern,
        grid=(bsz, DN_HEADS, nt),
        in_specs=[tok(0), tok(nb), tok(2 * nb), tok(C_Z // LANES),
                  pl.BlockSpec((1, tb, LANES), lambda b, h, t: (b, t, C_SMALL // LANES)),
                  pl.BlockSpec((1, 8, LANES), hcol(0)), pl.BlockSpec((1, 8, LANES), hcol(nb)),
                  pl.BlockSpec((1, 8, LANES), hcol(2 * nb)),
                  pl.BlockSpec((8, LANES), wcol(0)), pl.BlockSpec((8, LANES), wcol(nb)),
                  pl.BlockSpec((8, LANES), wcol(2 * nb)),
                  pl.BlockSpec((8, LANES), lambda b, h, t: (0, 0)),
                  pl.BlockSpec((1, LANES), lambda b, h, t: (0, 0)),
                  pl.BlockSpec((1, 1, DN_HEAD_DIM, DN_HEAD_DIM), lambda b, h, t: (b, h, 0, 0))],
        out_specs=[pl.BlockSpec((1, tb, LANES), lambda b, h, t: (b, t, h)),
                   pl.BlockSpec((1, 1, DN_HEAD_DIM, DN_HEAD_DIM), lambda b, h, t: (b, h, 0, 0))],
        out_shape=[jax.ShapeDtypeStruct((bsz, tpad, DN_WIDTH), F32),
                   jax.ShapeDtypeStruct((bsz, DN_HEADS, DN_HEAD_DIM, DN_HEAD_DIM), F32)],
        scratch_shapes=[pltpu.VMEM((DN_HEAD_DIM, DN_HEAD_DIM), F32),
                        pltpu.VMEM((3, tb + 8, LANES), F32),
                        pltpu.VMEM((tb, LANES), F32), pltpu.VMEM((tb, LANES), F32),
                        pltpu.VMEM((tb, LANES), F32), pltpu.VMEM((tb, LANES), F32)],
        compiler_params=_cparams(("parallel", "parallel", "arbitrary")),
        name="gated_deltanet",
    )(p3, p3, p3, p3, p3, hist, hist, hist, cw, cw, cw, hp, nw, s0)


def _cmp_weights(w1, w2):
    w1r = w1.reshape(2, CMP_BLOCK // CMP_STRIDE, CMP_STRIDE, NSA_HEAD_DIM, NSA_HEAD_DIM)
    eye = jnp.eye(2, dtype=F32)
    wf = jnp.einsum("srpde,st,hg->pshdrtge", w1r, eye, eye)
    wf = wf.reshape(CMP_STRIDE * 2 * NSA_KV_WIDTH, 2 * 2 * NSA_KV_WIDTH)
    w2bd = jnp.einsum("sef,st,hg->shetgf", w2, eye, eye).reshape(2 * NSA_KV_WIDTH, 2 * NSA_KV_WIDTH)
    return wf.astype(BF16), w2bd.astype(BF16)


def _cmp_epi_kernel(p_ref, pos_ref, w1_ref, w2_ref, o_ref):
    pm = p_ref[0]
    n = pm.shape[0]
    half = 2 * NSA_KV_WIDTH
    nxt = pltpu.roll(pm[:, half:2 * half], n - 1, 0)
    b_k = _hdot(pos_ref[0:1, :], w1_ref[0])
    b_v = _hdot(pos_ref[1:2, :], w1_ref[1])
    bias = jnp.concatenate([b_k, b_k, b_v, b_v], axis=-1)
    h = pm[:, 0:half] + nxt + bias
    o_ref[0] = _bdot(jax.nn.gelu(h), w2_ref[...])


def _cmp_epilogue(pmat, pos, w1, w2bd):
    bsz, n_sub, wid = pmat.shape
    return pl.pallas_call(
        _cmp_epi_kernel,
        grid=(bsz,),
        in_specs=[pl.BlockSpec((1, n_sub, wid), lambda b: (b, 0, 0)),
                  pl.BlockSpec(pos.shape, lambda b: (0, 0)),
                  pl.BlockSpec(w1.shape, lambda b: (0, 0, 0)),
                  pl.BlockSpec(w2bd.shape, lambda b: (0, 0))],
        out_specs=pl.BlockSpec((1, n_sub, wid // 2), lambda b: (b, 0, 0)),
        out_shape=jax.ShapeDtypeStruct((bsz, n_sub, wid // 2), F32),
        compiler_params=_cparams(("parallel",)),
        name="nsa_compress_epilogue",
    )(pmat, pos, w1, w2bd)


KEY_TILE = 256


def _slope3(hk):
    gi = _iota((NSA_GROUP, 1, 1), 0)
    out = jnp.zeros((NSA_GROUP, 1, 1), F32)
    for g in range(NSA_GROUP):
        out = jnp.where(gi == g, 2.0 ** (-8.0 * (hk * NSA_GROUP + g + 1) / NSA_HEADS), out)
    return out


def _masked_softmax3(s, valid):
    s = jnp.where(valid, s, NEG)
    m = jnp.max(s, -1, keepdims=True)
    p = jnp.where(valid, jnp.exp(s - m), 0.0)
    return p / jnp.maximum(jnp.sum(p, -1, keepdims=True), 1e-30)


def _select_blocks(score, n_slc):
    sidx = _iota((1, n_slc), 1)
    rank = jnp.zeros(score.shape, F32)
    for sp in range(n_slc):
        col = score[:, sp:sp + 1]
        beats = (col > score) | ((col == score) & (sp < sidx))
        rank = rank + jnp.where(beats, 1.0, 0.0)
    return jnp.where(rank < N_SELECT, 1.0, 0.0)


def _flash_branch(qs, kv_ref, hk, t_lo, t_hi, mask_fn, slope3, qpos, m_scr, l_scr, acc_scr, tq):
    g = NSA_GROUP
    dh = NSA_HEAD_DIM
    m_scr[...] = jnp.full(m_scr.shape, NEG, F32)
    l_scr[...] = jnp.zeros(l_scr.shape, F32)
    acc_scr[...] = jnp.zeros(acc_scr.shape, F32)

    def body(t, carry):
        k0 = pl.multiple_of(t * KEY_TILE, KEY_TILE)
        kv = kv_ref[0, pl.ds(k0, KEY_TILE), :]
        k = kv[:, hk * dh:(hk + 1) * dh]
        v = kv[:, NSA_KV_WIDTH + hk * dh:NSA_KV_WIDTH + (hk + 1) * dh]
        kpos = k0 + _iota((1, KEY_TILE), 1)
        dist = qpos - kpos.astype(F32)
        valid = mask_fn(k0, dist)[None]
        s = _bdot_nt(qs, k).reshape(g, tq, KEY_TILE) - slope3 * dist[None]
        s = jnp.where(valid, s, NEG)
        m_old = m_scr[...].reshape(g, tq, 1)
        m_new = jnp.maximum(m_old, jnp.max(s, -1, keepdims=True))
        alpha = jnp.exp(m_old - m_new)
        p = jnp.where(valid, jnp.exp(s - m_new), 0.0)
        l_new = alpha * l_scr[...].reshape(g, tq, 1) + jnp.sum(p, -1, keepdims=True)
        l_scr[...] = l_new.reshape(g * tq, 1)
        acc_scr[...] = alpha.reshape(g * tq, 1) * acc_scr[...] + _bdot(p.reshape(g * tq, KEY_TILE), v)
        m_scr[...] = m_new.reshape(g * tq, 1)
        return carry

    lax.fori_loop(t_lo, t_hi, body, 0)
    return acc_scr[...] / jnp.maximum(l_scr[...], 1e-30)


def _nsa_prompt_kernel(q_ref, sm_ref, kvs_ref, kvw_ref, kc_ref, o_ref, m_scr, l_scr, acc_scr,
                       *, tq, seq, n_cmp):
    g = NSA_GROUP
    dh = NSA_HEAD_DIM
    q0 = pl.program_id(1) * tq
    n_sub = kc_ref.shape[1]
    n_slc = seq // SLC_BLOCK
    qpos_i = q0 + _iota((tq, 1), 0)
    qpos = qpos_i.astype(F32)
    sm = sm_ref[0]
    t_hi = (q0 + tq + KEY_TILE - 1) // KEY_TILE
    t_lo_w = jnp.maximum(q0 - (WINDOW - 1), 0) // KEY_TILE

    cidx = _iota((1, n_sub), 1)
    c_end = cidx * CMP_STRIDE + (CMP_BLOCK - 1)
    valid_c = ((c_end <= qpos_i) & (cidx < n_cmp))[None]
    dist_c = (qpos - c_end.astype(F32))[None]
    cr = _iota((n_sub, n_slc), 0) * CMP_STRIDE
    s_st = _iota((n_sub, n_slc), 1) * SLC_BLOCK
    cover = jnp.where((cr < s_st + SLC_BLOCK) & (cr + (CMP_BLOCK - 1) >= s_st), 1.0, 0.0)
    sidx = _iota((1, n_slc), 1)
    cur = qpos_i >> int(math.log2(SLC_BLOCK))
    forced = (sidx == 0) | (sidx == cur) | (sidx == cur - 1)
    bonus = jnp.where(forced, FORCE_BONUS, 0.0)
    past_ok = sidx * SLC_BLOCK <= qpos_i

    for hk in range(NSA_KV_HEADS):
        slope3 = _slope3(hk)
        qs = jnp.concatenate([q_ref[0, :, (hk * g + i) * dh:(hk * g + i + 1) * dh] for i in range(g)], axis=0)
        qs = qs * (dh ** -0.5)
        kc = kc_ref[0, :, hk * dh:(hk + 1) * dh]
        vc = kc_ref[0, :, NSA_KV_WIDTH + hk * dh:NSA_KV_WIDTH + (hk + 1) * dh]
        s_c = _bdot_nt(qs, kc).reshape(g, tq, n_sub) - slope3 * dist_c
        p_c = _masked_softmax3(s_c, valid_c)
        o_c = _bdot(p_c.reshape(g * tq, n_sub), vc)
        imp = _hdot(jnp.sum(p_c, axis=0), cover)
        score = jnp.where(past_ok, imp + bonus, NEG)
        sel = _select_blocks(score, n_slc)

        def slc_mask(k0, dist, sel=sel):
            srow = _iota((n_slc, KEY_TILE), 0)
            kblk = (k0 + _iota((n_slc, KEY_TILE), 1)) >> int(math.log2(SLC_BLOCK))
            expand = jnp.where(srow == kblk, 1.0, 0.0)
            return (_bdot(sel, expand) > 0.5) & (dist >= 0.0)

        def win_mask(k0, dist):
            return (dist >= 0.0) & (dist < float(WINDOW))

        o_s = _flash_branch(qs, kvs_ref, hk, 0, t_hi, slc_mask, slope3, qpos, m_scr, l_scr, acc_scr, tq)
        o_w = _flash_branch(qs, kvw_ref, hk, t_lo_w, t_hi, win_mask, slope3, qpos, m_scr, l_scr, acc_scr, tq)

        outs = []
        for i in range(g):
            c0 = SM_NG + (hk * g + i) * 3
            gt = _sigmoid(sm[:, c0:c0 + 3])
            rows = slice(i * tq, (i + 1) * tq)
            outs.append(gt[:, 0:1] * o_c[rows] + gt[:, 1:2] * o_s[rows] + gt[:, 2:3] * o_w[rows])
        for i in range(0, g, 2):
            c0 = (hk * g + i) * dh
            o_ref[0, :, c0:c0 + 2 * dh] = jnp.concatenate([outs[i], outs[i + 1]], axis=-1)


def _nsa_prompt(p3, kcvc, *, tq):
    bsz, seq, _ = p3.shape
    n_sub = kcvc.shape[1]
    assert seq % KEY_TILE == 0 and seq % tq == 0 and seq % SLC_BLOCK == 0
    kern = functools.partial(_nsa_prompt_kernel, tq=tq, seq=seq, n_cmp=seq // CMP_STRIDE - 1)
    rows = NSA_GROUP * tq
    kvw = 2 * NSA_KV_WIDTH
    return pl.pallas_call(
        kern,
        grid=(bsz, seq // tq),
        in_specs=[pl.BlockSpec((1, tq, NSA_WIDTH), lambda b, j: (b, j, C_NQ // NSA_WIDTH)),
                  pl.BlockSpec((1, tq, LANES), lambda b, j: (b, j, C_SMALL // LANES)),
                  pl.BlockSpec((1, seq, kvw), lambda b, j: (b, 0, C_KVS // kvw)),
                  pl.BlockSpec((1, seq, kvw), lambda b, j: (b, 0, C_KVW // kvw)),
                  pl.BlockSpec((1, n_sub, kvw), lambda b, j: (b, 0, 0))],
        out_specs=pl.BlockSpec((1, tq, NSA_WIDTH), lambda b, j: (b, j, 0)),
        out_shape=jax.ShapeDtypeStruct((bsz, seq, NSA_WIDTH), F32),
        scratch_shapes=[pltpu.VMEM((rows, 1), F32), pltpu.VMEM((rows, 1), F32),
                        pltpu.VMEM((rows, NSA_HEAD_DIM), F32)],
        compiler_params=_cparams(("parallel", "arbitrary")),
        name="nsa_prompt_attention",
    )(p3, p3, p3, p3, kcvc)


def _cmp_paged_kernel(pt_ref, cache_ref, w_ref, o_ref, buf, sem, *, npg):
    i = pl.program_id(0)
    n = pl.num_programs(0)

    def page_copy(page, slot, j):
        return pltpu.make_async_copy(cache_ref.at[page], buf.at[slot, j], sem.at[slot])

    def issue(step, slot):
        for j in range(npg):
            page_copy(pt_ref[step * npg + j], slot, j).start()

    @pl.when(i == 0)
    def _():
        issue(0, 0)

    @pl.when(i + 1 < n)
    def _():
        issue(i + 1, (i + 1) % 2)

    slot = i % 2
    for j in range(npg):
        page_copy(0, slot, j).wait()
    x = buf[slot].reshape(npg * buf.shape[2], buf.shape[3])
    o_ref[...] = jnp.dot(x.astype(BF16), w_ref[...], preferred_element_type=F32)


def _cmp_paged(cache, page_table, wf, *, npg):
    n_pool, spp, feat = cache.shape
    bsz, n_pages = page_table.shape
    total = bsz * n_pages
    assert total % npg == 0
    kern = functools.partial(_cmp_paged_kernel, npg=npg)
    return pl.pallas_call(
        kern,
        grid_spec=pltpu.PrefetchScalarGridSpec(
            num_scalar_prefetch=1,
            grid=(total // npg,),
            in_specs=[pl.BlockSpec(memory_space=pl.ANY),
                      pl.BlockSpec(wf.shape, lambda i, pt: (0, 0))],
            out_specs=pl.BlockSpec((npg * spp, wf.shape[1]), lambda i, pt: (i, 0)),
            scratch_shapes=[pltpu.VMEM((2, npg, spp, feat), F32), pltpu.SemaphoreType.DMA((2,))]),
        out_shape=jax.ShapeDtypeStruct((total * spp, wf.shape[1]), F32),
        compiler_params=_cparams(("arbitrary",)),
        name="nsa_compress_paged",
    )(page_table.reshape(-1), cache, wf)


def _gather_heads(q_ref, hk):
    g = NSA_GROUP
    dh = NSA_HEAD_DIM
    qs = jnp.concatenate([q_ref[0, :, (hk * g + i) * dh:(hk * g + i + 1) * dh] for i in range(g)], axis=0)
    return qs * (dh ** -0.5)


def _nsa_select_kernel(q_ref, kc_ref, oc_ref, sel_ref, *, tq, past, n_cmp, n_slc, n_slc_pad):
    g = NSA_GROUP
    dh = NSA_HEAD_DIM
    n_sub = kc_ref.shape[1]
    qpos_i = past + _iota((tq, 1), 0)
    qpos = qpos_i.astype(F32)
    cidx = _iota((1, n_sub), 1)
    c_end = cidx * CMP_STRIDE + (CMP_BLOCK - 1)
    valid_c = ((c_end <= qpos_i) & (cidx < n_cmp))[None]
    dist_c = (qpos - c_end.astype(F32))[None]
    cr = _iota((n_sub, n_slc_pad), 0) * CMP_STRIDE
    s_st = _iota((n_sub, n_slc_pad), 1) * SLC_BLOCK
    cover = jnp.where((cr < s_st + SLC_BLOCK) & (cr + (CMP_BLOCK - 1) >= s_st), 1.0, 0.0)
    sidx = _iota((1, n_slc_pad), 1)
    sidx_f = sidx.astype(F32)
    cur = qpos_i >> int(math.log2(SLC_BLOCK))
    forced = (sidx == 0) | (sidx == cur) | (sidx == cur - 1)
    bonus = jnp.where(forced, FORCE_BONUS, 0.0)
    past_ok = sidx * SLC_BLOCK <= qpos_i
    lane = _iota((1, LANES), 1)
    for hk in range(NSA_KV_HEADS):
        slope3 = _slope3(hk)
        qs = _gather_heads(q_ref, hk)
        kc = kc_ref[0, :, hk * dh:(hk + 1) * dh]
        vc = kc_ref[0, :, NSA_KV_WIDTH + hk * dh:NSA_KV_WIDTH + (hk + 1) * dh]
        s_c = _bdot_nt(qs, kc).reshape(g, tq, n_sub) - slope3 * dist_c
        p_c = _masked_softmax3(s_c, valid_c)
        oc_ref[0, hk] = _bdot(p_c.reshape(g * tq, n_sub), vc)
        imp = _hdot(jnp.sum(p_c, axis=0), cover)
        score = jnp.where(past_ok, imp + bonus, NEG)
        score = jnp.where(sidx < n_slc, score, -jnp.inf)
        res = jnp.zeros((tq, LANES), F32)
        for it in range(min(N_SELECT, n_slc)):
            m = jnp.max(score, -1, keepdims=True)
            idx = jnp.min(jnp.where(score == m, sidx_f, 1e9), -1, keepdims=True)
            res = jnp.where(lane == it, idx, res)
            score = jnp.where(sidx_f == idx, -jnp.inf, score)
        sel_ref[0, hk] = res.astype(I32)


def _nsa_select(ps3, kcvc, *, past, n_cmp, n_slc):
    bsz, tq, _ = ps3.shape
    n_sub = kcvc.shape[1]
    n_slc_pad = -(-n_slc // LANES) * LANES
    kern = functools.partial(_nsa_select_kernel, tq=tq, past=past, n_cmp=n_cmp, n_slc=n_slc, n_slc_pad=n_slc_pad)
    rows = NSA_GROUP * tq
    return pl.pallas_call(
        kern,
        grid=(bsz,),
        in_specs=[pl.BlockSpec((1, tq, NSA_WIDTH), lambda b: (b, 0, C_NQ // NSA_WIDTH)),
                  pl.BlockSpec((1, n_sub, 2 * NSA_KV_WIDTH), lambda b: (b, 0, 0))],
        out_specs=[pl.BlockSpec((1, NSA_KV_HEADS, rows, NSA_HEAD_DIM), lambda b: (b, 0, 0, 0)),
                   pl.BlockSpec((1, NSA_KV_HEADS, tq, LANES), lambda b: (b, 0, 0, 0))],
        out_shape=[jax.ShapeDtypeStruct((bsz, NSA_KV_HEADS, rows, NSA_HEAD_DIM), F32),
                   jax.ShapeDtypeStruct((bsz, NSA_KV_HEADS, tq, LANES), I32)],
        compiler_params=_cparams(("parallel",)),
        name="nsa_sample_select",
    )(ps3, kcvc)


def _joint_softmax_pv(parts):
    m = None
    for s, valid, _ in parts:
        mi = jnp.max(jnp.where(valid, s, NEG), -1, keepdims=True)
        m = mi if m is None else jnp.maximum(m, mi)
    num = None
    den = None
    for s, valid, v in parts:
        p = jnp.where(valid, jnp.exp(jnp.where(valid, s, NEG) - m), 0.0)
        d = jnp.sum(p, -1, keepdims=True)
        o = _bdot(p.reshape(p.shape[0] * p.shape[1], p.shape[2]), v)
        num = o if num is None else num + o
        den = d if den is None else den + d
    return num / jnp.maximum(den.reshape(num.shape[0], 1), 1e-30)


def _nsa_sample_kernel(phys_ref, q_ref, sm_ref, kpos_ref, tail_ref, wcache_ref, wnew_ref, oc_ref, cache_ref,
                       o_ref, buf, sem, *, tq, t_valid, past, n_gather):
    g = NSA_GROUP
    dh = NSA_HEAD_DIM
    b = pl.program_id(0)
    per_b = NSA_KV_HEADS * n_gather

    def blk_copy(blk, hk, i):
        return pltpu.make_async_copy(cache_ref.at[blk], buf.at[hk, i], sem)

    for hk in range(NSA_KV_HEADS):
        for i in range(n_gather):
            blk_copy(phys_ref[b * per_b + hk * n_gather + i], hk, i).start()

    qpos_i = past + _iota((tq, 1), 0)
    qpos = qpos_i.astype(F32)
    sm = sm_ref[0]
    n_keys = n_gather * SLC_BLOCK
    per_q = n_keys // t_valid
    new_pos = (past + _iota((1, tq), 1)).astype(F32)
    new_ok = _iota((1, tq), 1) < t_valid
    dist_new = qpos - new_pos
    n_win = wcache_ref.shape[1]
    dist_wc = qpos - (past - n_win + _iota((1, n_win), 1)).astype(F32)
    owner = _iota((tq, n_keys), 1) >> int(math.log2(per_q))
    qrow = _iota((tq, n_keys), 0)

    win_parts = []
    for hk in range(NSA_KV_HEADS):
        slope3 = _slope3(hk)
        qs = _gather_heads(q_ref, hk)
        kw = wcache_ref[0, :, hk * dh:(hk + 1) * dh]
        vw = wcache_ref[0, :, NSA_KV_WIDTH + hk * dh:NSA_KV_WIDTH + (hk + 1) * dh]
        kn = wnew_ref[0, :, hk * dh:(hk + 1) * dh]
        vn = wnew_ref[0, :, NSA_KV_WIDTH + hk * dh:NSA_KV_WIDTH + (hk + 1) * dh]
        s_wc = _bdot_nt(qs, kw).reshape(g, tq, n_win) - slope3 * dist_wc[None]
        s_wn = _bdot_nt(qs, kn).reshape(g, tq, tq) - slope3 * dist_new[None]
        ok_wc = ((dist_wc >= 0.0) & (dist_wc < float(WINDOW)))[None]
        ok_wn = ((dist_new >= 0.0) & (dist_new < float(WINDOW)) & new_ok)[None]
        win_parts.append(_joint_softmax_pv([(s_wc, ok_wc, vw), (s_wn, ok_wn, vn)]))

    for hk in range(NSA_KV_HEADS):
        for i in range(n_gather):
            blk_copy(0, hk, i).wait()

    for hk in range(NSA_KV_HEADS):
        slope3 = _slope3(hk)
        qs = _gather_heads(q_ref, hk)
        kv = buf[hk].reshape(n_keys, 2 * NSA_KV_WIDTH)
        kp = kv[:, hk * dh:(hk + 1) * dh]
        vp = kv[:, NSA_KV_WIDTH + hk * dh:NSA_KV_WIDTH + (hk + 1) * dh]
        kt = tail_ref[0, :, hk * dh:(hk + 1) * dh]
        vt = tail_ref[0, :, NSA_KV_WIDTH + hk * dh:NSA_KV_WIDTH + (hk + 1) * dh]
        dist_p = qpos - kpos_ref[0, hk]
        s_p = _bdot_nt(qs, kp).reshape(g, tq, n_keys) - slope3 * dist_p[None]
        s_t = _bdot_nt(qs, kt).reshape(g, tq, tq) - slope3 * dist_new[None]
        ok_p = ((owner == qrow) & (dist_p >= 0.0))[None]
        ok_t = ((dist_new >= 0.0) & new_ok)[None]
        o_s = _joint_softmax_pv([(s_p, ok_p, vp), (s_t, ok_t, vt)])
        o_w = win_parts[hk]
        o_c = oc_ref[0, hk]
        outs = []
        for i in range(g):
            c0 = SM_NG + (hk * g + i) * 3
            gt = _sigmoid(sm[:, c0:c0 + 3])
            rows = slice(i * tq, (i + 1) * tq)
            outs.append(gt[:, 0:1] * o_c[rows] + gt[:, 1:2] * o_s[rows] + gt[:, 2:3] * o_w[rows])
        for i in range(0, g, 2):
            c0 = (hk * g + i) * dh
            o_ref[0, :, c0:c0 + 2 * dh] = jnp.concatenate([outs[i], outs[i + 1]], axis=-1)


def _nsa_sample(ps3, o_c, phys, kpos, cache_blocks, win_cache, *, t_valid, past):
    bsz, tq, _ = ps3.shape
    n_gather = t_valid * N_SELECT
    kvw = 2 * NSA_KV_WIDTH
    rows = NSA_GROUP * tq
    kern = functools.partial(_nsa_sample_kernel, tq=tq, t_valid=t_valid, past=past, n_gather=n_gather)
    return pl.pallas_call(
        kern,
        grid_spec=pltpu.PrefetchScalarGridSpec(
            num_scalar_prefetch=1,
            grid=(bsz,),
            in_specs=[pl.BlockSpec((1, tq, NSA_WIDTH), lambda b, ph: (b, 0, C_NQ // NSA_WIDTH)),
                      pl.BlockSpec((1, tq, LANES), lambda b, ph: (b, 0, C_SMALL // LANES)),
                      pl.BlockSpec((1, NSA_KV_HEADS, 1, n_gather * SLC_BLOCK), lambda b, ph: (b, 0, 0, 0)),
                      pl.BlockSpec((1, tq, kvw), lambda b, ph: (b, 0, C_KVS // kvw)),
                      pl.BlockSpec((1, win_cache.shape[1], kvw), lambda b, ph: (b, 0, 0)),
                      pl.BlockSpec((1, tq, kvw), lambda b, ph: (b, 0, C_KVW // kvw)),
                      pl.BlockSpec((1, NSA_KV_HEADS, rows, NSA_HEAD_DIM), lambda b, ph: (b, 0, 0, 0)),
                      pl.BlockSpec(memory_space=pl.ANY)],
            out_specs=pl.BlockSpec((1, tq, NSA_WIDTH), lambda b, ph: (b, 0, 0)),
            scratch_shapes=[pltpu.VMEM((NSA_KV_HEADS, n_gather, SLC_BLOCK, kvw), F32),
                            pltpu.SemaphoreType.DMA(())]),
        out_shape=jax.ShapeDtypeStruct((bsz, tq, NSA_WIDTH), F32),
        compiler_params=_cparams(("arbitrary",)),
        name="nsa_sample_attention",
    )(phys, ps3, ps3, kpos, ps3, win_cache, ps3, o_c, cache_blocks)


def _prompt_mixers(x, w_r, conv_w, a_log, dt_bias, norm_w, cmp_wf, cmp_w2bd, cmp_pos, cmp_w1, *, tm, tq, tb):
    bsz, seq, _ = x.shape
    p = _matmul(x.reshape(bsz * seq, D_MODEL), w_r, tm, P_COLS // 3)
    p3 = p.reshape(bsz, seq, P_COLS)
    hist = jnp.zeros((bsz, 8, 3 * DN_WIDTH), F32)
    s0 = jnp.zeros((bsz, DN_HEADS, DN_HEAD_DIM, DN_HEAD_DIM), F32)
    o_dn, s_new = _deltanet(p3, hist, s0, conv_w, a_log, dt_bias, norm_w, t_valid=seq, tb=tb, c=DN_CHUNK)
    kvc = p3[:, :, C_KVC:C_KVC + 2 * NSA_KV_WIDTH]
    n_sub = seq // CMP_STRIDE
    sub = kvc.reshape(bsz * n_sub, CMP_STRIDE * 2 * NSA_KV_WIDTH)
    pmat = _matmul(sub, cmp_wf, min(256, bsz * n_sub), cmp_wf.shape[1]).reshape(bsz, n_sub, -1)
    kcvc = _cmp_epilogue(pmat, cmp_pos.reshape(2, -1), cmp_w1, cmp_w2bd)
    o_nsa = _nsa_prompt(p3, kcvc, tq=tq)
    return p3, o_dn, s_new, o_nsa


def _sample_mixers(x, cache_cmp, cache_slc, win_buf, s0, conv_buf, page_table, w_r, conv_w, a_log, dt_bias,
                   norm_w, cmp_wf, cmp_w2bd, cmp_pos, cmp_w1):
    bsz, t, _ = x.shape
    tq = 8
    n_pages = page_table.shape[1]
    past = n_pages * PAGE_SIZE
    kvw = 2 * NSA_KV_WIDTH
    assert t <= tq and t <= SLC_BLOCK and past % SLC_BLOCK == 0
    assert (past + t) // CMP_STRIDE * CMP_STRIDE == past, "new rows never complete a compression sub-block"
    assert (t * N_SELECT * SLC_BLOCK // t) & (t * N_SELECT * SLC_BLOCK // t - 1) == 0
    ps = _matmul(x.reshape(bsz * t, D_MODEL), w_r, bsz * t, P_COLS // 3).reshape(bsz, t, P_COLS)
    ps3 = jnp.pad(ps, ((0, 0), (0, tq - t), (0, 0)))
    hist = jnp.pad(conv_buf, ((0, 0), (8 - (CONV_W - 1), 0), (0, 0)))
    o_dn, s_new = _deltanet(ps3, hist, s0, conv_w, a_log, dt_bias, norm_w, t_valid=t, tb=tq, c=tq)
    n_pool = cache_cmp.shape[0]
    spp = PAGE_SIZE // CMP_STRIDE
    n_sub = past // CMP_STRIDE
    pmat = _cmp_paged(cache_cmp.reshape(n_pool, spp, CMP_STRIDE * kvw), page_table, cmp_wf, npg=32)
    kcvc = _cmp_epilogue(pmat.reshape(bsz, n_sub, -1), cmp_pos.reshape(2, -1), cmp_w1, cmp_w2bd)
    n_past_blocks = past // SLC_BLOCK
    o_c, sel = _nsa_select(ps3, kcvc, past=past, n_cmp=n_sub - 1, n_slc=n_past_blocks + 1)
    sel = sel[:, :, :t, :N_SELECT]
    bpp = PAGE_SIZE // SLC_BLOCK
    jp = jnp.minimum(sel, n_past_blocks - 1)
    phys = page_table[jnp.arange(bsz)[:, None, None, None], jp // bpp] * bpp + jp % bpp
    kpos = jnp.where(sel >= n_past_blocks, 1e9, (sel * SLC_BLOCK).astype(F32))[..., None] + jnp.arange(SLC_BLOCK, dtype=F32)
    kpos = kpos.reshape(bsz, NSA_KV_HEADS, 1, t * N_SELECT * SLC_BLOCK)
    n_win = win_buf.shape[1]
    o_nsa = _nsa_sample(ps3, o_c, phys.reshape(-1).astype(I32), kpos, cache_slc.reshape(n_pool * bpp, SLC_BLOCK, kvw),
                        win_buf.reshape(bsz, n_win, kvw), t_valid=t, past=past)
    return ps, ps3, o_dn, s_new, o_nsa


def _layer_norm(x, g, b):
    xc = x - jnp.mean(x, -1, keepdims=True)
    var = jnp.mean(xc * xc, -1, keepdims=True)
    return xc * lax.rsqrt(var + LN_EPS) * g + b


def _rank_rows(v, n):
    ri = _iota(v.shape, 0)
    rank = jnp.zeros(v.shape, F32)
    for rp in range(n):
        row = v[rp:rp + 1, :]
        beats = (row > v) | ((row == v) & (rp < ri))
        rank = rank + jnp.where(beats, 1.0, 0.0)
    return rank


def _post_mixer_kernel(x_ref, odn_ref, onsa_ref, gdn_ref, gnsa_ref, wo_ref, g_ref, b_ref, wr_ref, br_ref,
                       x1_ref, idx_ref, wt_ref, pos_ref, cnt_ref, run_scr, *, tm, alpha):
    i = pl.program_id(0)

    @pl.when(i == 0)
    def _():
        run_scr[...] = jnp.zeros(run_scr.shape, F32)

    h = _sigmoid(gdn_ref[0]) * odn_ref[...] + _sigmoid(gnsa_ref[0]) * onsa_ref[...]
    x1 = _layer_norm(alpha * x_ref[...] + _bdot(h, wo_ref[...]), g_ref[...], b_ref[...])
    x1_ref[...] = x1

    ne = N_EXPERTS
    per = ne // N_GROUPS
    scores = _sigmoid(_hdot_nt(wr_ref[...], x1))
    s3 = (scores + br_ref[...]).reshape(N_GROUPS, per, tm)
    e3 = _iota((N_GROUPS, per, tm), 1).astype(F32)
    g1 = jnp.max(s3, axis=1, keepdims=True)
    first = jnp.min(jnp.where(s3 == g1, e3, float(per)), axis=1, keepdims=True)
    g2 = jnp.max(jnp.where(e3 == first, -jnp.inf, s3), axis=1, keepdims=True)
    grank = _rank_rows((g1 + g2).reshape(N_GROUPS, tm), N_GROUPS)
    keep = (grank < TOPK_GROUPS).reshape(N_GROUPS, 1, tm)
    selm = jnp.where(keep, s3, NEG).reshape(ne, tm)
    erank = _rank_rows(selm, ne)
    ei = _iota((ne, tm), 0).astype(F32)
    chosen = jnp.where(erank < TOP_K, 1.0, 0.0)
    tr = _iota((tm, tm), 0)
    tc = _iota((tm, tm), 1)
    before = jnp.where(tr < tc, 1.0, 0.0)
    pos_full = _bdot(chosen, before) + run_scr[:, 0:1]
    idx_rows, w_rows, pos_rows = [], [], []
    for k in range(TOP_K):
        hit = erank == float(k)
        idx_rows.append(jnp.sum(jnp.where(hit, ei, 0.0), 0, keepdims=True))
        w_rows.append(jnp.sum(jnp.where(hit, scores, 0.0), 0, keepdims=True))
        pos_rows.append(jnp.sum(jnp.where(hit, pos_full, 0.0), 0, keepdims=True))
    wsum = w_rows[0]
    for k in range(1, TOP_K):
        wsum = wsum + w_rows[k]
    zero = jnp.zeros((8 - TOP_K, tm), F32)
    idx_ref[...] = jnp.concatenate(idx_rows + [zero], 0).astype(I32)
    wt_ref[...] = jnp.concatenate([w / wsum * ROUTED_SCALE for w in w_rows] + [zero], 0)
    pos_ref[...] = jnp.concatenate(pos_rows + [zero], 0).astype(I32)
    run_scr[...] = run_scr[...] + jnp.sum(chosen, 1, keepdims=True)
    cnt_ref[...] = run_scr[...]


def _post_mixer(x, o_dn, o_nsa, p3, w_out_bf16, ln_g, ln_b, w_router_t, b_router, *, tm, alpha):
    n, d = x.shape
    assert n % tm == 0
    bsz, seq, _ = p3.shape
    assert seq % tm == 0 or tm % seq == 0
    if seq % tm == 0:
        per_b = seq // tm
        gspec = lambda c: pl.BlockSpec((1, tm, d), lambda i: (i // per_b, i % per_b, c))
        p_in = p3
    else:
        p_in = p3.reshape(1, n, P_COLS)
        gspec = lambda c: pl.BlockSpec((1, tm, d), lambda i: (0, i, c))
    tok = pl.BlockSpec((tm, d), lambda i: (i, 0))
    full = lambda a: pl.BlockSpec(a.shape, lambda i: (0,) * a.ndim)
    rt = pl.BlockSpec((8, tm), lambda i: (0, i))
    kern = functools.partial(_post_mixer_kernel, tm=tm, alpha=alpha)
    g2 = ln_g.reshape(1, d)
    b2 = ln_b.reshape(1, d)
    br = b_router.reshape(N_EXPERTS, 1)
    return pl.pallas_call(
        kern,
        grid=(n // tm,),
        in_specs=[tok, tok, tok, gspec(C_MG // d), gspec(C_MG // d + 1), full(w_out_bf16), full(g2), full(b2),
                  full(w_router_t), full(br)],
        out_specs=[tok, rt, rt, rt, pl.BlockSpec((N_EXPERTS, LANES), lambda i: (0, 0))],
        out_shape=[jax.ShapeDtypeStruct((n, d), F32), jax.ShapeDtypeStruct((8, n), I32),
                   jax.ShapeDtypeStruct((8, n), F32), jax.ShapeDtypeStruct((8, n), I32),
                   jax.ShapeDtypeStruct((N_EXPERTS, LANES), F32)],
        scratch_shapes=[pltpu.VMEM((N_EXPERTS, LANES), F32)],
        compiler_params=_cparams(("arbitrary",)),
        name="merge_outproj_ln_router",
    )(x, o_dn, o_nsa, p_in, p_in, w_out_bf16, g2, b2, w_router_t, br)


def _dispatch_kernel(slot_ref, x_ref, xs_in_ref, xs_ref, sem, *, tm):
    del xs_in_ref

    def row_copy(r, s):
        return pltpu.make_async_copy(x_ref.at[pl.ds(r, 1)], xs_ref.at[pl.ds(s, 1)], sem)

    def issue(r, carry):
        for k in range(TOP_K):
            row_copy(r, slot_ref[k, r]).start()
        return carry

    lax.fori_loop(0, tm, issue, 0)

    def drain(r, carry):
        for k in range(TOP_K):
            row_copy(0, 0).wait()
        return carry

    lax.fori_loop(0, tm, drain, 0)


def _dispatch(x1, slot, n_slots, *, tm):
    n, d = x1.shape
    assert n % tm == 0
    kern = functools.partial(_dispatch_kernel, tm=tm)
    xs0 = jnp.zeros((n_slots, d), F32)
    return pl.pallas_call(
        kern,
        grid=(n // tm,),
        in_specs=[pl.BlockSpec((8, tm), lambda i: (0, i), memory_space=pltpu.SMEM),
                  pl.BlockSpec((tm, d), lambda i: (i, 0)),
                  pl.BlockSpec(memory_space=pl.ANY)],
        out_specs=pl.BlockSpec(memory_space=pl.ANY),
        out_shape=jax.ShapeDtypeStruct((n_slots, d), F32),
        scratch_shapes=[pltpu.SemaphoreType.DMA(())],
        input_output_aliases={2: 0},
        compiler_params=_cparams(("arbitrary",)),
        name="moe_dispatch",
    )(slot, x1, xs0)


def _expert_kernel(be_ref, nu_ref, x_ref, wg_ref, wu_ref, wd_ref, y_ref):
    i = pl.program_id(0)

    @pl.when(i < nu_ref[0])
    def _():
        x = x_ref[...].astype(BF16)
        hg = jnp.dot(x, wg_ref[0].astype(BF16), preferred_element_type=F32)
        hu = jnp.dot(x, wu_ref[0].astype(BF16), preferred_element_type=F32)
        y_ref[...] = _bdot(_silu(hg) * hu, wd_ref[0])

    @pl.when(i >= nu_ref[0])
    def _():
        y_ref[...] = jnp.zeros(y_ref.shape, F32)


def _experts(xs, blk_exp, n_used, w_gate, w_up, w_down, *, blk):
    n_slots, d = xs.shape
    de = w_gate.shape[2]
    n_blocks = n_slots // blk
    return pl.pallas_call(
        _expert_kernel,
        grid_spec=pltpu.PrefetchScalarGridSpec(
            num_scalar_prefetch=2,
            grid=(n_blocks,),
            in_specs=[pl.BlockSpec((blk, d), lambda i, be, nu: (jnp.minimum(i, nu[0] - 1), 0)),
                      pl.BlockSpec((1, d, de), lambda i, be, nu: (be[i], 0, 0)),
                      pl.BlockSpec((1, d, de), lambda i, be, nu: (be[i], 0, 0)),
                      pl.BlockSpec((1, de, d), lambda i, be, nu: (be[i], 0, 0))],
            out_specs=pl.BlockSpec((blk, d), lambda i, be, nu: (i, 0))),
        out_shape=jax.ShapeDtypeStruct((n_slots, d), F32),
        compiler_params=_cparams(("arbitrary",)),
        name="moe_experts",
    )(blk_exp, n_used, xs, w_gate, w_up, w_down)


def _combine_kernel(slot_ref, x_ref, w_ref, ys_ref, wsg_ref, wsu_ref, wsd_ref, g_ref, b_ref, o_ref, buf, sem,
                    *, tm, alpha):
    def row_copy(s, k, r):
        return pltpu.make_async_copy(ys_ref.at[pl.ds(s, 1)], buf.at[k, pl.ds(r, 1)], sem)

    def issue(r, carry):
        for k in range(TOP_K):
            row_copy(slot_ref[k, r], k, r).start()
        return carry

    lax.fori_loop(0, tm, issue, 0)
    x = x_ref[...]
    xb = x.astype(BF16)
    hs = _silu(jnp.dot(xb, wsg_ref[...], preferred_element_type=F32)) * jnp.dot(xb, wsu_ref[...],
                                                                               preferred_element_type=F32)
    acc = alpha * x + _bdot(hs, wsd_ref[...])

    def drain(r, carry):
        for k in range(TOP_K):
            row_copy(0, k, 0).wait()
        return carry

    lax.fori_loop(0, tm, drain, 0)
    w = w_ref[...]
    for k in range(TOP_K):
        acc = acc + w[:, k:k + 1] * buf[k]
    o_ref[...] = _layer_norm(acc, g_ref[...], b_ref[...])


def _combine(x1, slot, w_tok, ys, ws_gate, ws_up, ws_down, ln_g, ln_b, *, tm, alpha):
    n, d = x1.shape
    assert n % tm == 0
    kern = functools.partial(_combine_kernel, tm=tm, alpha=alpha)
    full = lambda a: pl.BlockSpec(a.shape, lambda i: (0,) * a.ndim)
    g2 = ln_g.reshape(1, d)
    b2 = ln_b.reshape(1, d)
    return pl.pallas_call(
        kern,
        grid=(n // tm,),
        in_specs=[pl.BlockSpec((8, tm), lambda i: (0, i), memory_space=pltpu.SMEM),
                  pl.BlockSpec((tm, d), lambda i: (i, 0)),
                  pl.BlockSpec((tm, 8), lambda i: (i, 0)),
                  pl.BlockSpec(memory_space=pl.ANY),
                  full(ws_gate), full(ws_up), full(ws_down), full(g2), full(b2)],
        out_specs=pl.BlockSpec((tm, d), lambda i: (i, 0)),
        out_shape=jax.ShapeDtypeStruct((n, d), F32),
        scratch_shapes=[pltpu.VMEM((TOP_K, tm, d), F32), pltpu.SemaphoreType.DMA(())],
        compiler_params=_cparams(("arbitrary",)),
        name="moe_combine_ln",
    )(slot, x1, w_tok, ys, ws_gate, ws_up, ws_down, g2, b2)


def _moe_layer(x1, idx, wts, pos, counts, w_gate, w_up, w_down, ws_gate, ws_up, ws_down, ln_g, ln_b,
               *, blk, tm_d, tm_c, alpha):
    n = x1.shape[0]
    cnt = counts[:, 0].astype(I32)
    padded = (cnt + blk - 1) // blk * blk
    pad_end = jnp.cumsum(padded)
    pad_start = pad_end - padded
    slot = jnp.where(_iota((8, n), 0) < TOP_K, pad_start[idx] + pos, 0).astype(I32)
    n_blocks = -(-(n * TOP_K) // blk) + N_EXPERTS
    blk_exp = jnp.minimum(jnp.sum(pad_end[None, :] <= (jnp.arange(n_blocks) * blk)[:, None], axis=1),
                          N_EXPERTS - 1).astype(I32)
    n_used = (pad_end[-1:] // blk).astype(I32)
    xs = _dispatch(x1, slot, n_blocks * blk, tm=tm_d)
    ys = _experts(xs, blk_exp, n_used, w_gate, w_up, w_down, blk=blk)
    return _combine(x1, slot, wts.T, ys, ws_gate.astype(BF16), ws_up.astype(BF16), ws_down.astype(BF16),
                    ln_g, ln_b, tm=tm_c, alpha=alpha)


def kernel(x_prompt, x_sample, cache_cmp_kv, cache_slc_kv, cache_win_kv, state_delta_S, state_delta_conv, page_table, w_in, dn_conv_w, dn_A_log, dn_dt_bias, dn_norm_w, nsa_cmp_w1, nsa_cmp_pos, nsa_cmp_w2, w_out, ln1_g, ln1_b, w_router, b_router, w_exp_gate, w_exp_up, w_exp_down, w_sh_gate, w_sh_up, w_sh_down, ln2_g, ln2_b):
    depth = w_in.shape[0]
    assert depth == 1
    alpha = (2.0 * depth) ** 0.25
    bsz, seq, d = x_prompt.shape
    sb, st, _ = x_sample.shape
    w_r = _reorder_w_in(w_in[0])
    wf, w2bd = _cmp_weights(nsa_cmp_w1[0], nsa_cmp_w2[0])
    mix_w = (w_r, dn_conv_w[0], dn_A_log[0], dn_dt_bias[0], dn_norm_w[0], wf, w2bd, nsa_cmp_pos[0], nsa_cmp_w1[0])
    p3, o_dn, s_p, o_nsa = _prompt_mixers(x_prompt, *mix_w, tm=min(512, bsz * seq), tq=64, tb=min(512, seq))
    ps, _, o_dn_s, s_s, o_nsa_s = _sample_mixers(x_sample, cache_cmp_kv[0], cache_slc_kv[0], cache_win_kv[0],
                                                 state_delta_S[0], state_delta_conv[0], page_table, *mix_w)
    wo = w_out[0].astype(BF16)
    wrt = w_router[0].T

    def ffn(x2, o_dn2, o_nsa2, p_any, tm, blk, tm_d, tm_c):
        x1, idx, wts, pos, counts = _post_mixer(x2, o_dn2, o_nsa2, p_any, wo, ln1_g[0], ln1_b[0], wrt, b_router[0],
                                                tm=tm, alpha=alpha)
        return _moe_layer(x1, idx, wts, pos, counts, w_exp_gate[0], w_exp_up[0], w_exp_down[0],
                          w_sh_gate[0], w_sh_up[0], w_sh_down[0], ln2_g[0], ln2_b[0],
                          blk=blk, tm_d=tm_d, tm_c=tm_c, alpha=alpha)

    y_p = ffn(x_prompt.reshape(-1, d), o_dn.reshape(-1, d), o_nsa.reshape(-1, d), p3, 256, 256, 256, 128)
    y_s = ffn(x_sample.reshape(-1, d), o_dn_s[:, :st].reshape(-1, d), o_nsa_s[:, :st].reshape(-1, d), ps,
              sb * st, 64, sb * st, sb * st)

    kv_shape = (2, NSA_KV_HEADS, NSA_HEAD_DIM)
    kvw = 2 * NSA_KV_WIDTH

    def kv_rows(pp, c0):
        return pp[:, :, c0:c0 + kvw].reshape(pp.shape[:2] + kv_shape)

    nconv = CONV_W - 1
    conv_p = jnp.concatenate([jnp.zeros((bsz, nconv, 3 * DN_WIDTH), F32), p3[:, :, :3 * DN_WIDTH]], 1)[:, -nconv:]
    conv_s = jnp.concatenate([state_delta_conv[0], ps[:, :, :3 * DN_WIDTH]], 1)[:, -nconv:]
    past = page_table.shape[1] * PAGE_SIZE
    win_s = jnp.concatenate([cache_win_kv[0], kv_rows(ps, C_KVW)], 1)[:, -min(WINDOW, past + st):]
    return (y_p.reshape(x_prompt.shape), y_s.reshape(x_sample.shape),
            kv_rows(p3, C_KVC)[None], kv_rows(p3, C_KVS)[None], kv_rows(p3, C_KVW)[:, -min(WINDOW, seq):][None],
            s_p[None], conv_p[None],
            kv_rows(ps, C_KVC)[None], kv_rows(ps, C_KVS)[None], win_s[None], s_s[None], conv_s[None])
```

```python
import functools
import math

import jax
import jax.numpy as jnp
from jax import lax
from jax.experimental import pallas as pl
from jax.experimental.pallas import tpu as pltpu

F32 = jnp.float32
BF16 = jnp.bfloat16
I32 = jnp.int32
HIGHEST = lax.Precision.HIGHEST

D_MODEL = 1024
PAGE_SIZE = 128
DN_HEADS = 8
DN_HEAD_DIM = 128
DN_WIDTH = DN_HEADS * DN_HEAD_DIM
CONV_W = 4
DN_CHUNK = 64
NSA_HEADS = 16
NSA_KV_HEADS = 2
NSA_GROUP = NSA_HEADS // NSA_KV_HEADS
NSA_HEAD_DIM = 64
NSA_WIDTH = NSA_HEADS * NSA_HEAD_DIM
NSA_KV_WIDTH = NSA_KV_HEADS * NSA_HEAD_DIM
KV_COLS = 2 * NSA_KV_WIDTH
CMP_BLOCK = 32
CMP_STRIDE = 16
SLC_BLOCK = 64
N_SELECT = 16
WINDOW = 512
N_EXPERTS = 64
TOP_K = 6
N_GROUPS = 8
TOPK_GROUPS = 4
ROUTED_SCALE = 2.5
LN_EPS = 1e-5
RMS_EPS = 1e-6
NEG = -1e30
FORCE_BONUS = 1e6

C_QKV = 0
C_Z = 3072
C_NQ = 4096
C_MG = 5120
C_KVC = 7168
C_KVS = 7424
C_KVW = 7680
C_SMALL = 7936
P_COLS = 8064
SM_A = 0
SM_B = DN_HEADS
SM_NG = 2 * DN_HEADS

LANES = 128
VMEM_LIMIT = 48 * 1024 * 1024
KEY_TILE = 256


def _tiles(n_prompt_tokens, seq, n_sample_tokens):
    return dict(
        proj_tm=min(512, n_prompt_tokens), proj_tn=P_COLS // 3,
        dn_tb=min(512, seq), dn_heads=4,
        nsa_tq=64,
        cmp_pages=32,
        post_tm=min(256, n_prompt_tokens),
        moe_blk=256, moe_tm_dispatch=min(256, n_prompt_tokens), moe_tm_combine=min(128, n_prompt_tokens),
        sample_moe_blk=64, sample_tm=n_sample_tokens,
    )


def _cparams(sem):
    return pltpu.CompilerParams(dimension_semantics=sem, vmem_limit_bytes=VMEM_LIMIT)


def _bdot(a, b):
    return jnp.dot(a.astype(BF16), b.astype(BF16), preferred_element_type=F32)


def _bdot_nt(a, b):
    return lax.dot_general(a.astype(BF16), b.astype(BF16), (((1,), (1,)), ((), ())),
                           preferred_element_type=F32)


def _bdot_tn(a, b):
    return lax.dot_general(a.astype(BF16), b.astype(BF16), (((0,), (0,)), ((), ())),
                           preferred_element_type=F32)


def _hdot(a, b):
    return jnp.dot(a, b, precision=HIGHEST, preferred_element_type=F32)


def _hdot_nt(a, b):
    return lax.dot_general(a, b, (((1,), (1,)), ((), ())), precision=HIGHEST,
                           preferred_element_type=F32)


def _sigmoid(x):
    return 1.0 / (1.0 + jnp.exp(-x))


def _silu(x):
    return x * _sigmoid(x)


def _softplus(x):
    return jnp.maximum(x, 0.0) + jnp.log(1.0 + jnp.exp(-jnp.abs(x)))


def _iota(shape, dim):
    return lax.broadcasted_iota(I32, shape, dim)


def _log2(n):
    assert n & (n - 1) == 0
    return int(math.log2(n))


def _mm_kernel(x_ref, w_ref, o_ref):
    o_ref[...] = jnp.dot(x_ref[...].astype(BF16), w_ref[...], preferred_element_type=F32)


def _matmul(x, w_bf16, tm, tn):
    m, k = x.shape
    n = w_bf16.shape[1]
    assert m % tm == 0 and n % tn == 0
    return pl.pallas_call(
        _mm_kernel,
        grid=(n // tn, m // tm),
        in_specs=[pl.BlockSpec((tm, k), lambda j, i: (i, 0)),
                  pl.BlockSpec((k, tn), lambda j, i: (0, j))],
        out_specs=pl.BlockSpec((tm, tn), lambda j, i: (i, j)),
        out_shape=jax.ShapeDtypeStruct((m, n), F32),
        compiler_params=_cparams(("parallel", "parallel")),
        name="dense_matmul",
    )(x, w_bf16)


def _reorder_w_in(w_in):
    o = 0
    seg = {}
    for name, size in (("qkv", 3 * DN_WIDTH), ("z", DN_WIDTH), ("a", DN_HEADS), ("b", DN_HEADS),
                       ("nq", NSA_WIDTH), ("kvc", KV_COLS), ("kvs", KV_COLS),
                       ("kvw", KV_COLS), ("ng", 3 * NSA_HEADS), ("mg", 2 * D_MODEL)):
        seg[name] = w_in[:, o:o + size]
        o += size
    assert o == w_in.shape[1]
    pad = jnp.zeros((w_in.shape[0], P_COLS - C_SMALL - SM_NG - 3 * NSA_HEADS), w_in.dtype)
    w = jnp.concatenate([seg["qkv"], seg["z"], seg["nq"], seg["mg"], seg["kvc"], seg["kvs"], seg["kvw"],
                         seg["a"], seg["b"], seg["ng"], pad], axis=1)
    assert w.shape[1] == P_COLS
    return w.astype(BF16)


def _tri_inverse(lmats, c):
    r = _iota((c, c), 0)
    q = _iota((c, c), 1)
    eye = (r == q).astype(F32)
    blk = min(16, c)
    shift = _log2(blk)
    same = (r >> shift) == (q >> shift)
    dmats = [jnp.where(same, lm, 0.0) for lm in lmats]
    prods = [eye - dm for dm in dmats]
    dpows = dmats
    k = 2
    while k < blk:
        dpows = [_bdot(dp, dp) for dp in dpows]
        prods = [pr + _bdot(pr, dp) for pr, dp in zip(prods, dpows)]
        k *= 2
    if c == blk:
        return prods
    mmats = [_bdot(pr, lm - dm) for pr, lm, dm in zip(prods, lmats, dmats)]
    outers = [eye - mm for mm in mmats]
    mpows = mmats
    k = 2
    while k < c // blk:
        mpows = [_bdot(mp, mp) for mp in mpows]
        outers = [ou + _bdot(ou, mp) for ou, mp in zip(outers, mpows)]
        k *= 2
    return [_bdot(ou, pr) for ou, pr in zip(outers, prods)]


def _dn_kernel(q_ref, k_ref, v_ref, z_ref, sm_ref, hq_ref, hk_ref, hv_ref, cwq_ref, cwk_ref, cwv_ref,
               hp_ref, nw_ref, s0_ref, o_ref, sout_ref,
               s_scr, xp_scr, qn_scr, kn_scr, vn_scr, gb_scr, u_scr, w_scr, qe_scr, kd_scr, a_scr, eg_scr,
               *, tb, c, t_valid, hp):
    hb = pl.program_id(1)
    t = pl.program_id(2)
    nt = pl.num_programs(2)
    dk = DN_HEAD_DIM
    nc = tb // c

    @pl.when(t == 0)
    def _():
        s_scr[...] = s0_ref[0]
        xp_scr[0, 0:8, :] = hq_ref[0]
        xp_scr[1, 0:8, :] = hk_ref[0]
        xp_scr[2, 0:8, :] = hv_ref[0]

    rows = t * tb + _iota((tb, 1), 0)
    valid = rows < t_valid

    def conv(i, raw_ref, cw_ref):
        xp_scr[i, 8:8 + tb, :] = raw_ref[0]
        acc = xp_scr[i, 8:8 + tb, :] * cw_ref[CONV_W - 1:CONV_W, :]
        for j in range(CONV_W - 1):
            acc = acc + xp_scr[i, 8 - (CONV_W - 1) + j:8 - (CONV_W - 1) + j + tb, :] * cw_ref[j:j + 1, :]
        tail = xp_scr[i, tb:tb + 8, :]
        xp_scr[i, 0:8, :] = tail
        return _silu(acc)

    qc = conv(0, q_ref, cwq_ref)
    kc = conv(1, k_ref, cwk_ref)
    vc = conv(2, v_ref, cwv_ref)
    lane = _iota((1, LANES), 1)
    sm = sm_ref[0]
    for hh in range(hp):
        h = hb * hp + hh
        cs = slice(hh * dk, (hh + 1) * dk)
        qh = qc[:, cs]
        kh = kc[:, cs]
        qn = qh * lax.rsqrt(jnp.sum(qh * qh, -1, keepdims=True) + 1e-6) * (dk ** -0.5)
        kn = kh * lax.rsqrt(jnp.sum(kh * kh, -1, keepdims=True) + 1e-6)
        a_h = jnp.sum(jnp.where(lane == SM_A + h, sm, 0.0), -1, keepdims=True)
        b_h = jnp.sum(jnp.where(lane == SM_B + h, sm, 0.0), -1, keepdims=True)
        neg_a = -jnp.exp(jnp.sum(jnp.where(lane == h, hp_ref[0:1, :], 0.0), -1, keepdims=True))
        dtb = jnp.sum(jnp.where(lane == h, hp_ref[1:2, :], 0.0), -1, keepdims=True)
        g = neg_a * _softplus(a_h + dtb)
        beta = _sigmoid(b_h)
        qn_scr[hh] = jnp.where(valid, qn, 0.0)
        kn_scr[hh] = jnp.where(valid, kn, 0.0)
        vn_scr[hh] = jnp.where(valid, vc[:, cs], 0.0)
        gb_scr[hh] = jnp.where(lane == 0, jnp.where(valid, g, 0.0), jnp.where(valid, beta, 0.0))

    r = _iota((c, c), 0)
    q = _iota((c, c), 1)
    incl = r >= q
    strict = r > q

    where = [(hh, slice(ci * c, (ci + 1) * c)) for hh in range(hp) for ci in range(nc)]
    lmats, vbs, kbes = [], [], []
    for hh, rs in where:
        qi = qn_scr[hh, rs, :]
        ki = kn_scr[hh, rs, :]
        gb = gb_scr[hh, rs, :]
        gi = gb[:, 0:1]
        bi = gb[:, 1:2]
        g_row = jnp.sum(jnp.where(r == q, gi, 0.0), 0, keepdims=True)
        gcum_col = jnp.sum(jnp.where(incl, g_row, 0.0), 1, keepdims=True)
        gcum_row = jnp.sum(jnp.where(r <= q, gi, 0.0), 0, keepdims=True)
        decay = jnp.where(incl, jnp.exp(jnp.where(incl, gcum_col - gcum_row, 0.0)), 0.0)
        kb = ki * bi
        eg = jnp.exp(gcum_col)
        g_last = gcum_col[c - 1:c, :]
        lmats.append(jnp.where(strict, _bdot_nt(kb, ki) * decay, 0.0))
        vbs.append(vn_scr[hh, rs, :] * bi)
        kbes.append(kb * eg)
        a_scr[hh, rs, :] = jnp.where(incl, _bdot_nt(qi, ki) * decay, 0.0)
        qe_scr[hh, rs, :] = qi * eg
        kd_scr[hh, rs, :] = ki * jnp.exp(g_last - gcum_col)
        e0 = rs.start // c * 8
        eg_scr[hh, e0:e0 + 8, :] = jnp.broadcast_to(jnp.exp(g_last), (8, LANES))
    tms = _tri_inverse(lmats, c)
    for (hh, rs), tm, vb, kbe in zip(where, tms, vbs, kbes):
        u_scr[hh, rs, :] = _bdot(tm, vb)
        w_scr[hh, rs, :] = _bdot(tm, kbe)

    nw = nw_ref[...]

    def chunk(ci, carry):
        r0 = pl.multiple_of(ci * c, c)
        e0 = pl.multiple_of(ci * 8, 8)
        for hh in range(hp):
            s = s_scr[hh]
            wq = jnp.concatenate([w_scr[hh, pl.ds(r0, c), :], qe_scr[hh, pl.ds(r0, c), :]], axis=0)
            ws = _bdot(wq, s)
            v_new = u_scr[hh, pl.ds(r0, c), :] - ws[0:c]
            o = ws[c:2 * c] + _bdot(a_scr[hh, pl.ds(r0, c), :], v_new)
            s_scr[hh] = s * eg_scr[hh, pl.ds(e0, 8), :][0:1, :] + _bdot_tn(kd_scr[hh, pl.ds(r0, c), :], v_new)
            o = o * lax.rsqrt(jnp.mean(o * o, -1, keepdims=True) + RMS_EPS) * nw
            o_ref[0, pl.ds(r0, c), hh * dk:(hh + 1) * dk] = o * _silu(z_ref[0, pl.ds(r0, c), hh * dk:(hh + 1) * dk])
        return carry

    lax.fori_loop(0, nc, chunk, 0)

    @pl.when(t == nt - 1)
    def _():
        sout_ref[0] = s_scr[...]


def _deltanet(p3, hist, s0, conv_w, a_log, dt_bias, norm_w, *, t_valid, tb, c, hp):
    bsz, tpad, _ = p3.shape
    assert tpad % tb == 0 and tb % c == 0 and tb % 8 == 0 and DN_HEADS % hp == 0
    nt = tpad // tb
    dk = DN_HEAD_DIM
    wid = hp * dk
    cw = jnp.concatenate([conv_w, jnp.zeros((8 - CONV_W, conv_w.shape[1]), F32)], 0)
    hpar = jnp.zeros((8, LANES), F32).at[0, :DN_HEADS].set(a_log).at[1, :DN_HEADS].set(dt_bias)
    nw = norm_w.reshape(1, dk)
    nb = DN_WIDTH // wid

    tok = lambda off: pl.BlockSpec((1, tb, wid), lambda b, h, t: (b, t, off + h))
    his = lambda off: pl.BlockSpec((1, 8, wid), lambda b, h, t: (b, 0, off + h))
    cws = lambda off: pl.BlockSpec((8, wid), lambda b, h, t: (0, off + h))
    st = pl.BlockSpec((1, hp, dk, dk), lambda b, h, t: (b, h, 0, 0))
    kern = functools.partial(_dn_kernel, tb=tb, c=c, t_valid=t_valid, hp=hp)
    big = pltpu.VMEM((hp, tb, dk), F32)
    return pl.pallas_call(
        kern,
        grid=(bsz, DN_HEADS // hp, nt),
        in_specs=[tok(0), tok(nb), tok(2 * nb), tok(C_Z // wid),
                  pl.BlockSpec((1, tb, LANES), lambda b, h, t: (b, t, C_SMALL // LANES)),
                  his(0), his(nb), his(2 * nb), cws(0), cws(nb), cws(2 * nb),
                  pl.BlockSpec((8, LANES), lambda b, h, t: (0, 0)),
                  pl.BlockSpec((1, dk), lambda b, h, t: (0, 0)),
                  st],
        out_specs=[pl.BlockSpec((1, tb, wid), lambda b, h, t: (b, t, h)), st],
        out_shape=[jax.ShapeDtypeStruct((bsz, tpad, DN_WIDTH), F32),
                   jax.ShapeDtypeStruct((bsz, DN_HEADS, dk, dk), F32)],
        scratch_shapes=[pltpu.VMEM((hp, dk, dk), F32),
                        pltpu.VMEM((3, tb + 8, wid), F32),
                        big, big, big, big, big, big, big, big,
                        pltpu.VMEM((hp, tb, c), F32),
                        pltpu.VMEM((hp, (tb // c) * 8, LANES), F32)],
        compiler_params=_cparams(("parallel", "parallel", "arbitrary")),
        name="gated_deltanet",
    )(p3, p3, p3, p3, p3, hist, hist, hist, cw, cw, cw, hpar, nw, s0)


def _cmp_weights(w1, w2):
    w1r = w1.reshape(2, CMP_BLOCK // CMP_STRIDE, CMP_STRIDE, NSA_HEAD_DIM, NSA_HEAD_DIM)
    eye = jnp.eye(2, dtype=F32)
    wf = jnp.einsum("srpde,st,hg->pshdrtge", w1r, eye, eye)
    wf = wf.reshape(CMP_STRIDE * KV_COLS, 2 * KV_COLS)
    w2bd = jnp.einsum("sef,st,hg->shetgf", w2, eye, eye).reshape(KV_COLS, KV_COLS)
    return wf.astype(BF16), w2bd.astype(BF16)


def _cmp_epi_kernel(p_ref, pos_ref, w1_ref, w2_ref, o_ref):
    pm = p_ref[0]
    n = pm.shape[0]
    nxt = pltpu.roll(pm[:, KV_COLS:2 * KV_COLS], n - 1, 0)
    b_k = _hdot(pos_ref[0:1, :], w1_ref[0])
    b_v = _hdot(pos_ref[1:2, :], w1_ref[1])
    bias = jnp.concatenate([b_k, b_k, b_v, b_v], axis=-1)
    h = pm[:, 0:KV_COLS] + nxt + bias
    o_ref[0] = _bdot(jax.nn.gelu(h), w2_ref[...])


def _cmp_epilogue(pmat, pos, w1, w2bd):
    bsz, n_sub, wid = pmat.shape
    return pl.pallas_call(
        _cmp_epi_kernel,
        grid=(bsz,),
        in_specs=[pl.BlockSpec((1, n_sub, wid), lambda b: (b, 0, 0)),
                  pl.BlockSpec(pos.shape, lambda b: (0, 0)),
                  pl.BlockSpec(w1.shape, lambda b: (0, 0, 0)),
                  pl.BlockSpec(w2bd.shape, lambda b: (0, 0))],
        out_specs=pl.BlockSpec((1, n_sub, wid // 2), lambda b: (b, 0, 0)),
        out_shape=jax.ShapeDtypeStruct((bsz, n_sub, wid // 2), F32),
        compiler_params=_cparams(("parallel",)),
        name="nsa_compress_epilogue",
    )(pmat, pos, w1, w2bd)


def _cmp_paged_kernel(pt_ref, cache_ref, w_ref, o_ref, buf, rows_scr, sem, *, npg):
    i = pl.program_id(0)
    n = pl.num_programs(0)
    spp = PAGE_SIZE // CMP_STRIDE

    def page_copy(page, slot, j):
        return pltpu.make_async_copy(cache_ref.at[page], buf.at[slot, j], sem.at[slot])

    def issue(step, slot):
        for j in range(npg):
            page_copy(pt_ref[step * npg + j], slot, j).start()

    @pl.when(i == 0)
    def _():
        issue(0, 0)

    @pl.when(i + 1 < n)
    def _():
        issue(i + 1, (i + 1) % 2)

    slot = i % 2
    for j in range(npg):
        page_copy(0, slot, j).wait()
    halves = KV_COLS // LANES
    for j in range(npg):
        for hf in range(halves):
            rows_scr[hf, j * PAGE_SIZE:(j + 1) * PAGE_SIZE, :] = buf[slot, j, hf * LANES:(hf + 1) * LANES, :].T
    acc = jnp.zeros(o_ref.shape, F32)
    for p in range(CMP_STRIDE):
        for hf in range(halves):
            xs = rows_scr[hf, pl.ds(p, npg * spp, stride=CMP_STRIDE), :]
            w0 = p * KV_COLS + hf * LANES
            acc = acc + jnp.dot(xs.astype(BF16), w_ref[w0:w0 + LANES, :], preferred_element_type=F32)
    o_ref[...] = acc


def _cmp_paged(cache_t, page_table, wf, *, npg):
    n_pool, cols, psz = cache_t.shape
    bsz, n_pages = page_table.shape
    total = bsz * n_pages
    spp = psz // CMP_STRIDE
    assert total % npg == 0 and cols == KV_COLS and psz == PAGE_SIZE
    kern = functools.partial(_cmp_paged_kernel, npg=npg)
    return pl.pallas_call(
        kern,
        grid_spec=pltpu.PrefetchScalarGridSpec(
            num_scalar_prefetch=1,
            grid=(total // npg,),
            in_specs=[pl.BlockSpec(memory_space=pl.ANY),
                      pl.BlockSpec(wf.shape, lambda i, pt: (0, 0))],
            out_specs=pl.BlockSpec((npg * spp, wf.shape[1]), lambda i, pt: (i, 0)),
            scratch_shapes=[pltpu.VMEM((2, npg, cols, psz), F32), pltpu.VMEM((cols // LANES, npg * psz, LANES), F32),
                            pltpu.SemaphoreType.DMA((2,))]),
        out_shape=jax.ShapeDtypeStruct((total * spp, wf.shape[1]), F32),
        compiler_params=_cparams(("arbitrary",)),
        name="nsa_compress_paged",
    )(page_table.reshape(-1), cache_t, wf)


def _slope(head):
    return 2.0 ** (-8.0 * (head + 1) / NSA_HEADS)


def _gather_heads(q_ref, hk):
    g = NSA_GROUP
    dh = NSA_HEAD_DIM
    qs = jnp.concatenate([q_ref[0, :, (hk * g + i) * dh:(hk * g + i + 1) * dh] for i in range(g)], axis=0)
    return qs * (dh ** -0.5)


def _cmp_branch(qs, kc, vc, hk, valid_c, dist_c, tq):
    s_all = _bdot_nt(qs, kc)
    ps = []
    psum = None
    for i in range(NSA_GROUP):
        s = s_all[i * tq:(i + 1) * tq] - _slope(hk * NSA_GROUP + i) * dist_c
        s = jnp.where(valid_c, s, NEG)
        m = jnp.max(s, -1, keepdims=True)
        p = jnp.where(valid_c, jnp.exp(s - m), 0.0)
        p = p / jnp.maximum(jnp.sum(p, -1, keepdims=True), 1e-30)
        ps.append(p)
        psum = p if psum is None else psum + p
    return _bdot(jnp.concatenate(ps, axis=0), vc), psum


def _select_blocks(score, n_slc):
    sidx = _iota((1, n_slc), 1)
    rank = jnp.zeros(score.shape, F32)
    for sp in range(n_slc):
        col = score[:, sp:sp + 1]
        beats = (col > score) | ((col == score) & (sp < sidx))
        rank = rank + jnp.where(beats, 1.0, 0.0)
    return jnp.where(rank < N_SELECT, 1.0, 0.0)


def _flash_branch(qs, kv_ref, hk, t_lo, t_hi, mask_fn, qpos, m_scr, acc_scr, tq):
    g = NSA_GROUP
    dh = NSA_HEAD_DIM
    m_scr[...] = jnp.full(m_scr.shape, NEG, F32)
    acc_scr[...] = jnp.zeros(acc_scr.shape, F32)
    qb = qs.astype(BF16)
    ones = jnp.ones((KEY_TILE, dh), F32)

    def body(i, carry):
        t = t_hi - 1 - i
        k0 = pl.multiple_of(t * KEY_TILE, KEY_TILE)
        kv = kv_ref[0, pl.ds(k0, KEY_TILE), :]
        k = kv[:, hk * dh:(hk + 1) * dh].astype(BF16)
        vaug = jnp.concatenate([kv[:, NSA_KV_WIDTH + hk * dh:NSA_KV_WIDTH + (hk + 1) * dh], ones], axis=1).astype(BF16)
        kpos = k0 + _iota((1, KEY_TILE), 1)
        dist = qpos - kpos.astype(F32)
        valid = mask_fn(k0, dist)
        s_all = lax.dot_general(qb, k, (((1,), (1,)), ((), ())), preferred_element_type=F32)
        ps = []
        alphas = []
        for j in range(g):
            rs = slice(j * tq, (j + 1) * tq)
            s = jnp.where(valid, s_all[rs] - _slope(hk * g + j) * dist, NEG)
            m_old = m_scr[rs, :]
            m_new = jnp.maximum(m_old, jnp.max(s, -1, keepdims=True))
            alphas.append(jnp.exp(m_old - m_new))
            ps.append(jnp.exp(s - jnp.concatenate([m_new] * (KEY_TILE // LANES), axis=1)).astype(BF16))
            m_scr[rs, :] = m_new
        pv = jnp.dot(jnp.concatenate(ps, axis=0), vaug, preferred_element_type=F32)
        acc_scr[...] = jnp.concatenate(alphas, axis=0) * acc_scr[...] + pv
        return carry

    lax.fori_loop(0, t_hi - t_lo, body, 0)
    acc = acc_scr[...]
    return acc[:, 0:dh] / jnp.maximum(acc[:, dh:2 * dh], 1e-30)


def _gate_combine(sm, hk, o_c, o_s, o_w, o_ref, tq):
    g = NSA_GROUP
    dh = NSA_HEAD_DIM
    outs = []
    for i in range(g):
        c0 = SM_NG + (hk * g + i) * 3
        gt = _sigmoid(sm[:, c0:c0 + 3])
        rows = slice(i * tq, (i + 1) * tq)
        outs.append(gt[:, 0:1] * o_c[rows] + gt[:, 1:2] * o_s[rows] + gt[:, 2:3] * o_w[rows])
    for i in range(0, g, 2):
        c0 = (hk * g + i) * dh
        o_ref[0, :, c0:c0 + 2 * dh] = jnp.concatenate([outs[i], outs[i + 1]], axis=-1)


def _nsa_prompt_kernel(q_ref, sm_ref, kvs_ref, kvw_ref, kc_ref, o_ref, m_scr, acc_scr, *, tq, seq, n_cmp):
    dh = NSA_HEAD_DIM
    q0 = pl.program_id(1) * tq
    n_sub = kc_ref.shape[1]
    n_slc = seq // SLC_BLOCK
    qpos_i = q0 + _iota((tq, 1), 0)
    qpos = qpos_i.astype(F32)
    sm = sm_ref[0]
    t_hi = (q0 + tq + KEY_TILE - 1) // KEY_TILE
    t_lo_w = jnp.maximum(q0 - (WINDOW - 1), 0) // KEY_TILE

    cidx = _iota((1, n_sub), 1)
    c_end = cidx * CMP_STRIDE + (CMP_BLOCK - 1)
    valid_c = (c_end <= qpos_i) & (cidx < n_cmp)
    dist_c = qpos - c_end.astype(F32)
    cr = _iota((n_sub, n_slc), 0) * CMP_STRIDE
    s_st = _iota((n_sub, n_slc), 1) * SLC_BLOCK
    cover = jnp.where((cr < s_st + SLC_BLOCK) & (cr + (CMP_BLOCK - 1) >= s_st), 1.0, 0.0)
    sidx = _iota((1, n_slc), 1)
    cur = qpos_i >> _log2(SLC_BLOCK)
    forced = (sidx == 0) | (sidx == cur) | (sidx == cur - 1)
    bonus = jnp.where(forced, FORCE_BONUS, 0.0)
    past_ok = sidx * SLC_BLOCK <= qpos_i

    for hk in range(NSA_KV_HEADS):
        qs = _gather_heads(q_ref, hk)
        kc = kc_ref[0, :, hk * dh:(hk + 1) * dh]
        vc = kc_ref[0, :, NSA_KV_WIDTH + hk * dh:NSA_KV_WIDTH + (hk + 1) * dh]
        o_c, psum = _cmp_branch(qs, kc, vc, hk, valid_c, dist_c, tq)
        imp = _hdot(psum, cover)
        score = jnp.where(past_ok, imp + bonus, NEG)
        sel = _select_blocks(score, n_slc)

        def slc_mask(k0, dist, sel=sel):
            srow = _iota((n_slc, KEY_TILE), 0)
            kblk = (k0 + _iota((n_slc, KEY_TILE), 1)) >> _log2(SLC_BLOCK)
            expand = jnp.where(srow == kblk, 1.0, 0.0)
            return (_bdot(sel, expand) > 0.5) & (dist >= 0.0)

        def win_mask(k0, dist):
            return (dist >= 0.0) & (dist < float(WINDOW))

        o_s = _flash_branch(qs, kvs_ref, hk, 0, t_hi, slc_mask, qpos, m_scr, acc_scr, tq)
        o_w = _flash_branch(qs, kvw_ref, hk, t_lo_w, t_hi, win_mask, qpos, m_scr, acc_scr, tq)
        _gate_combine(sm, hk, o_c, o_s, o_w, o_ref, tq)


def _nsa_prompt(p3, kcvc, *, tq):
    bsz, seq, _ = p3.shape
    n_sub = kcvc.shape[1]
    assert seq % KEY_TILE == 0 and seq % tq == 0 and seq % SLC_BLOCK == 0 and KEY_TILE % tq == 0
    kern = functools.partial(_nsa_prompt_kernel, tq=tq, seq=seq, n_cmp=seq // CMP_STRIDE - 1)
    rows = NSA_GROUP * tq
    return pl.pallas_call(
        kern,
        grid=(bsz, seq // tq),
        in_specs=[pl.BlockSpec((1, tq, NSA_WIDTH), lambda b, j: (b, j, C_NQ // NSA_WIDTH)),
                  pl.BlockSpec((1, tq, LANES), lambda b, j: (b, j, C_SMALL // LANES)),
                  pl.BlockSpec((1, seq, KV_COLS), lambda b, j: (b, 0, C_KVS // KV_COLS)),
                  pl.BlockSpec((1, seq, KV_COLS), lambda b, j: (b, 0, C_KVW // KV_COLS)),
                  pl.BlockSpec((1, n_sub, KV_COLS), lambda b, j: (b, 0, 0))],
        out_specs=pl.BlockSpec((1, tq, NSA_WIDTH), lambda b, j: (b, j, 0)),
        out_shape=jax.ShapeDtypeStruct((bsz, seq, NSA_WIDTH), F32),
        scratch_shapes=[pltpu.VMEM((rows, LANES), F32), pltpu.VMEM((rows, 2 * NSA_HEAD_DIM), F32)],
        compiler_params=_cparams(("parallel", "arbitrary")),
        name="nsa_prompt_attention",
    )(p3, p3, p3, p3, kcvc)


def _nsa_select_kernel(q_ref, kc_ref, oc_ref, sel_ref, *, tq, past, n_cmp, n_slc, n_slc_pad):
    dh = NSA_HEAD_DIM
    n_sub = kc_ref.shape[1]
    qpos_i = past + _iota((tq, 1), 0)
    qpos = qpos_i.astype(F32)
    cidx = _iota((1, n_sub), 1)
    c_end = cidx * CMP_STRIDE + (CMP_BLOCK - 1)
    valid_c = (c_end <= qpos_i) & (cidx < n_cmp)
    dist_c = qpos - c_end.astype(F32)
    cr = _iota((n_sub, n_slc_pad), 0) * CMP_STRIDE
    s_st = _iota((n_sub, n_slc_pad), 1) * SLC_BLOCK
    cover = jnp.where((cr < s_st + SLC_BLOCK) & (cr + (CMP_BLOCK - 1) >= s_st), 1.0, 0.0)
    sidx = _iota((1, n_slc_pad), 1)
    sidx_f = sidx.astype(F32)
    cur = qpos_i >> _log2(SLC_BLOCK)
    forced = (sidx == 0) | (sidx == cur) | (sidx == cur - 1)
    bonus = jnp.where(forced, FORCE_BONUS, 0.0)
    past_ok = sidx * SLC_BLOCK <= qpos_i
    lane = _iota((1, LANES), 1)
    for hk in range(NSA_KV_HEADS):
        qs = _gather_heads(q_ref, hk)
        kc = kc_ref[0, :, hk * dh:(hk + 1) * dh]
        vc = kc_ref[0, :, NSA_KV_WIDTH + hk * dh:NSA_KV_WIDTH + (hk + 1) * dh]
        o_c, psum = _cmp_branch(qs, kc, vc, hk, valid_c, dist_c, tq)
        oc_ref[0, hk] = o_c
        imp = _hdot(psum, cover)
        score = jnp.where(past_ok, imp + bonus, NEG)
        score = jnp.where(sidx < n_slc, score, -jnp.inf)
        res = jnp.zeros((tq, LANES), F32)
        for it in range(min(N_SELECT, n_slc)):
            m = jnp.max(score, -1, keepdims=True)
            idx = jnp.min(jnp.where(score == m, sidx_f, 1e9), -1, keepdims=True)
            res = jnp.where(lane == it, idx, res)
            score = jnp.where(sidx_f == idx, -jnp.inf, score)
        sel_ref[0, hk] = res.astype(I32)


def _nsa_select(ps3, kcvc, *, past, n_cmp, n_slc):
    bsz, tq, _ = ps3.shape
    n_sub = kcvc.shape[1]
    n_slc_pad = -(-n_slc // LANES) * LANES
    kern = functools.partial(_nsa_select_kernel, tq=tq, past=past, n_cmp=n_cmp, n_slc=n_slc, n_slc_pad=n_slc_pad)
    rows = NSA_GROUP * tq
    return pl.pallas_call(
        kern,
        grid=(bsz,),
        in_specs=[pl.BlockSpec((1, tq, NSA_WIDTH), lambda b: (b, 0, C_NQ // NSA_WIDTH)),
                  pl.BlockSpec((1, n_sub, KV_COLS), lambda b: (b, 0, 0))],
        out_specs=[pl.BlockSpec((1, NSA_KV_HEADS, rows, NSA_HEAD_DIM), lambda b: (b, 0, 0, 0)),
                   pl.BlockSpec((1, NSA_KV_HEADS, tq, LANES), lambda b: (b, 0, 0, 0))],
        out_shape=[jax.ShapeDtypeStruct((bsz, NSA_KV_HEADS, rows, NSA_HEAD_DIM), F32),
                   jax.ShapeDtypeStruct((bsz, NSA_KV_HEADS, tq, LANES), I32)],
        compiler_params=_cparams(("parallel",)),
        name="nsa_sample_select",
    )(ps3, kcvc)


def _joint_softmax_pv(parts, hk, tq):
    g = NSA_GROUP
    outs = []
    for j in range(g):
        rs = slice(j * tq, (j + 1) * tq)
        slope = _slope(hk * g + j)
        ss = [jnp.where(valid, s_all[rs] - slope * dist, NEG) for s_all, valid, dist, _, _ in parts]
        m = None
        for s in ss:
            mi = jnp.max(s, -1, keepdims=True)
            m = mi if m is None else jnp.maximum(m, mi)
        num = None
        den = None
        for s, (_, valid, _, v, v_t) in zip(ss, parts):
            p = jnp.where(valid, jnp.exp(s - m), 0.0)
            d = jnp.sum(p, -1, keepdims=True)
            o = _bdot_nt(p, v) if v_t else _bdot(p, v)
            num = o if num is None else num + o
            den = d if den is None else den + d
        outs.append(num / jnp.maximum(den, 1e-30))
    return jnp.concatenate(outs, axis=0)


def _nsa_sample_kernel(phys_ref, q_ref, sm_ref, kpos_ref, tail_ref, wcache_ref, wnew_ref, oc_ref, cache_ref,
                       o_ref, kbuf, vbuf, sem, *, tq, t_valid, past, n_gather):
    dh = NSA_HEAD_DIM
    b = pl.program_id(0)
    per_b = NSA_KV_HEADS * n_gather

    def page_copies(page, hk, i):
        dst = pl.ds(i * PAGE_SIZE, PAGE_SIZE)
        return (pltpu.make_async_copy(cache_ref.at[page, pl.ds(hk * dh, dh), :], kbuf.at[hk, :, dst], sem),
                pltpu.make_async_copy(cache_ref.at[page, pl.ds(NSA_KV_WIDTH + hk * dh, dh), :], vbuf.at[hk, :, dst], sem))

    for hk in range(NSA_KV_HEADS):
        for i in range(n_gather):
            for cp in page_copies(phys_ref[b * per_b + hk * n_gather + i], hk, i):
                cp.start()

    qpos_i = past + _iota((tq, 1), 0)
    qpos = qpos_i.astype(F32)
    sm = sm_ref[0]
    n_keys = n_gather * PAGE_SIZE
    per_q = n_keys // t_valid
    new_ok = _iota((1, tq), 1) < t_valid
    dist_new = qpos - (past + _iota((1, tq), 1)).astype(F32)
    n_win = wcache_ref.shape[2]
    dist_wc = qpos - (past - n_win + _iota((1, n_win), 1)).astype(F32)
    ok_wc = (dist_wc >= 0.0) & (dist_wc < float(WINDOW))
    ok_wn = (dist_new >= 0.0) & (dist_new < float(WINDOW)) & new_ok
    ok_t = (dist_new >= 0.0) & new_ok
    own = (_iota((tq, n_keys), 1) >> _log2(per_q)) == _iota((tq, n_keys), 0)

    qss = [_gather_heads(q_ref, hk) for hk in range(NSA_KV_HEADS)]
    win = []
    for hk in range(NSA_KV_HEADS):
        kw_t = wcache_ref[0, hk * dh:(hk + 1) * dh, :]
        vw_t = wcache_ref[0, NSA_KV_WIDTH + hk * dh:NSA_KV_WIDTH + (hk + 1) * dh, :]
        kn = wnew_ref[0, :, hk * dh:(hk + 1) * dh]
        vn = wnew_ref[0, :, NSA_KV_WIDTH + hk * dh:NSA_KV_WIDTH + (hk + 1) * dh]
        win.append(_joint_softmax_pv([(_bdot(qss[hk], kw_t), ok_wc, dist_wc, vw_t, True),
                                      (_bdot_nt(qss[hk], kn), ok_wn, dist_new, vn, False)], hk, tq))

    for hk in range(NSA_KV_HEADS):
        for i in range(n_gather):
            for cp in page_copies(0, hk, i):
                cp.wait()

    for hk in range(NSA_KV_HEADS):
        kt = tail_ref[0, :, hk * dh:(hk + 1) * dh]
        vt = tail_ref[0, :, NSA_KV_WIDTH + hk * dh:NSA_KV_WIDTH + (hk + 1) * dh]
        dist_p = qpos - kpos_ref[0, hk]
        ok_p = own & (dist_p >= 0.0)
        o_s = _joint_softmax_pv([(_bdot(qss[hk], kbuf[hk]), ok_p, dist_p, vbuf[hk], True),
                                 (_bdot_nt(qss[hk], kt), ok_t, dist_new, vt, False)], hk, tq)
        _gate_combine(sm, hk, oc_ref[0, hk], o_s, win[hk], o_ref, tq)


def _nsa_sample(ps3, o_c, phys, kpos, cache_t, win_t, *, t_valid, past):
    bsz, tq, _ = ps3.shape
    n_gather = t_valid * N_SELECT
    rows = NSA_GROUP * tq
    n_keys = n_gather * PAGE_SIZE
    kern = functools.partial(_nsa_sample_kernel, tq=tq, t_valid=t_valid, past=past, n_gather=n_gather)
    return pl.pallas_call(
        kern,
        grid_spec=pltpu.PrefetchScalarGridSpec(
            num_scalar_prefetch=1,
            grid=(bsz,),
            in_specs=[pl.BlockSpec((1, tq, NSA_WIDTH), lambda b, ph: (b, 0, C_NQ // NSA_WIDTH)),
                      pl.BlockSpec((1, tq, LANES), lambda b, ph: (b, 0, C_SMALL // LANES)),
                      pl.BlockSpec((1, NSA_KV_HEADS, 1, n_keys), lambda b, ph: (b, 0, 0, 0)),
                      pl.BlockSpec((1, tq, KV_COLS), lambda b, ph: (b, 0, C_KVS // KV_COLS)),
                      pl.BlockSpec((1,) + win_t.shape[1:], lambda b, ph: (b, 0, 0)),
                      pl.BlockSpec((1, tq, KV_COLS), lambda b, ph: (b, 0, C_KVW // KV_COLS)),
                      pl.BlockSpec((1, NSA_KV_HEADS, rows, NSA_HEAD_DIM), lambda b, ph: (b, 0, 0, 0)),
                      pl.BlockSpec(memory_space=pl.ANY)],
            out_specs=pl.BlockSpec((1, tq, NSA_WIDTH), lambda b, ph: (b, 0, 0)),
            scratch_shapes=[pltpu.VMEM((NSA_KV_HEADS, NSA_HEAD_DIM, n_keys), F32),
                            pltpu.VMEM((NSA_KV_HEADS, NSA_HEAD_DIM, n_keys), F32),
                            pltpu.SemaphoreType.DMA(())]),
        out_shape=jax.ShapeDtypeStruct((bsz, tq, NSA_WIDTH), F32),
        compiler_params=_cparams(("arbitrary",)),
        name="nsa_sample_attention",
    )(phys, ps3, ps3, kpos, ps3, win_t, ps3, o_c, cache_t)


def _rows_transposed(cache):
    nd = cache.ndim
    perm = tuple(range(nd - 4)) + (nd - 3, nd - 2, nd - 1, nd - 4)
    t = jnp.transpose(cache, perm)
    return t.reshape(t.shape[:nd - 4] + (KV_COLS, cache.shape[nd - 4]))


def _prompt_mixers(x, w_r, conv_w, a_log, dt_bias, norm_w, cmp_wf, cmp_w2bd, cmp_pos, cmp_w1, tl):
    bsz, seq, _ = x.shape
    p = _matmul(x.reshape(bsz * seq, D_MODEL), w_r, tl["proj_tm"], tl["proj_tn"])
    p3 = p.reshape(bsz, seq, P_COLS)
    hist = jnp.zeros((bsz, 8, 3 * DN_WIDTH), F32)
    s0 = jnp.zeros((bsz, DN_HEADS, DN_HEAD_DIM, DN_HEAD_DIM), F32)
    o_dn, s_new = _deltanet(p3, hist, s0, conv_w, a_log, dt_bias, norm_w, t_valid=seq, tb=tl["dn_tb"], c=DN_CHUNK,
                            hp=tl["dn_heads"])
    kvc = p3[:, :, C_KVC:C_KVC + KV_COLS]
    n_sub = seq // CMP_STRIDE
    sub = kvc.reshape(bsz * n_sub, CMP_STRIDE * KV_COLS)
    pmat = _matmul(sub, cmp_wf, min(256, bsz * n_sub), cmp_wf.shape[1]).reshape(bsz, n_sub, -1)
    kcvc = _cmp_epilogue(pmat, cmp_pos.reshape(2, -1), cmp_w1, cmp_w2bd)
    o_nsa = _nsa_prompt(p3, kcvc, tq=tl["nsa_tq"])
    return p3, o_dn, s_new, o_nsa


def _sample_mixers(x, cache_cmp, cache_slc, win_buf, s0, conv_buf, page_table, w_r, conv_w, a_log, dt_bias,
                   norm_w, cmp_wf, cmp_w2bd, cmp_pos, cmp_w1, tl):
    bsz, t, _ = x.shape
    tq = 8
    n_pages = page_table.shape[1]
    past = n_pages * PAGE_SIZE
    assert t <= tq and t <= SLC_BLOCK and past % SLC_BLOCK == 0 and cache_cmp.shape[1] == PAGE_SIZE
    assert (past + t) // CMP_STRIDE * CMP_STRIDE == past, "new rows never complete a compression sub-block"
    ps = _matmul(x.reshape(bsz * t, D_MODEL), w_r, bsz * t, tl["proj_tn"]).reshape(bsz, t, P_COLS)
    ps3 = jnp.pad(ps, ((0, 0), (0, tq - t), (0, 0)))
    hist = jnp.pad(conv_buf, ((0, 0), (8 - (CONV_W - 1), 0), (0, 0)))
    o_dn, s_new = _deltanet(ps3, hist, s0, conv_w, a_log, dt_bias, norm_w, t_valid=t, tb=tq, c=tq, hp=DN_HEADS)
    n_sub = past // CMP_STRIDE
    pmat = _cmp_paged(_rows_transposed(cache_cmp), page_table, cmp_wf, npg=tl["cmp_pages"])
    kcvc = _cmp_epilogue(pmat.reshape(bsz, n_sub, -1), cmp_pos.reshape(2, -1), cmp_w1, cmp_w2bd)
    n_past_blocks = past // SLC_BLOCK
    o_c, sel = _nsa_select(ps3, kcvc, past=past, n_cmp=n_sub - 1, n_slc=n_past_blocks + 1)
    sel = sel[:, :, :t, :N_SELECT]
    bpp = PAGE_SIZE // SLC_BLOCK
    jp = jnp.minimum(sel, n_past_blocks - 1)
    page = jp // bpp
    phys = page_table[jnp.arange(bsz)[:, None, None, None], page]
    row = jnp.arange(PAGE_SIZE)
    in_blk = (row // SLC_BLOCK == (jp % bpp)[..., None]) & (sel < n_past_blocks)[..., None]
    kpos = jnp.where(in_blk, (page[..., None] * PAGE_SIZE + row).astype(F32), 1e9)
    kpos = kpos.reshape(bsz, NSA_KV_HEADS, 1, t * N_SELECT * PAGE_SIZE)
    o_nsa = _nsa_sample(ps3, o_c, phys.reshape(-1).astype(I32), kpos, _rows_transposed(cache_slc),
                        _rows_transposed(win_buf), t_valid=t, past=past)
    return ps, o_dn, s_new, o_nsa


def _layer_norm(x, g, b):
    xc = x - jnp.mean(x, -1, keepdims=True)
    var = jnp.mean(xc * xc, -1, keepdims=True)
    return xc * lax.rsqrt(var + LN_EPS) * g + b


def _rank_rows(v, n):
    ri = _iota(v.shape, 0)
    rank = jnp.zeros(v.shape, F32)
    for rp in range(n):
        row = v[rp:rp + 1, :]
        beats = (row > v) | ((row == v) & (rp < ri))
        rank = rank + jnp.where(beats, 1.0, 0.0)
    return rank


def _post_mixer_kernel(x_ref, odn_ref, onsa_ref, gdn_ref, gnsa_ref, wo_ref, g_ref, b_ref, wr_ref, br_ref,
                       x1_ref, idx_ref, wt_ref, pos_ref, cnt_ref, run_scr, *, tm, alpha):
    i = pl.program_id(0)

    @pl.when(i == 0)
    def _():
        run_scr[...] = jnp.zeros(run_scr.shape, F32)

    h = _sigmoid(gdn_ref[0]) * odn_ref[...] + _sigmoid(gnsa_ref[0]) * onsa_ref[...]
    x1 = _layer_norm(alpha * x_ref[...] + _bdot(h, wo_ref[...]), g_ref[...], b_ref[...])
    x1_ref[...] = x1

    ne = N_EXPERTS
    per = ne // N_GROUPS
    scores = _sigmoid(_hdot_nt(wr_ref[...], x1))
    s3 = (scores + br_ref[...]).reshape(N_GROUPS, per, tm)
    e3 = _iota((N_GROUPS, per, tm), 1).astype(F32)
    g1 = jnp.max(s3, axis=1, keepdims=True)
    first = jnp.min(jnp.where(s3 == g1, e3, float(per)), axis=1, keepdims=True)
    g2 = jnp.max(jnp.where(e3 == first, -jnp.inf, s3), axis=1, keepdims=True)
    grank = _rank_rows((g1 + g2).reshape(N_GROUPS, tm), N_GROUPS)
    keep = (grank < TOPK_GROUPS).reshape(N_GROUPS, 1, tm)
    selm = jnp.where(keep, s3, NEG).reshape(ne, tm)
    erank = _rank_rows(selm, ne)
    ei = _iota((ne, tm), 0).astype(F32)
    chosen = jnp.where(erank < TOP_K, 1.0, 0.0)
    tr = _iota((tm, tm), 0)
    tc = _iota((tm, tm), 1)
    before = jnp.where(tr < tc, 1.0, 0.0)
    pos_full = _bdot(chosen, before) + run_scr[:, 0:1]
    idx_rows, w_rows, pos_rows = [], [], []
    for k in range(TOP_K):
        hit = erank == float(k)
        idx_rows.append(jnp.sum(jnp.where(hit, ei, 0.0), 0, keepdims=True))
        w_rows.append(jnp.sum(jnp.where(hit, scores, 0.0), 0, keepdims=True))
        pos_rows.append(jnp.sum(jnp.where(hit, pos_full, 0.0), 0, keepdims=True))
    wsum = w_rows[0]
    for k in range(1, TOP_K):
        wsum = wsum + w_rows[k]
    zero = jnp.zeros((8 - TOP_K, tm), F32)
    idx_ref[...] = jnp.concatenate(idx_rows + [zero], 0).astype(I32)
    wt_ref[...] = jnp.concatenate([w / wsum * ROUTED_SCALE for w in w_rows] + [zero], 0)
    pos_ref[...] = jnp.concatenate(pos_rows + [zero], 0).astype(I32)
    run_scr[...] = run_scr[...] + jnp.sum(chosen, 1, keepdims=True)
    cnt_ref[...] = run_scr[...]


def _post_mixer(x, o_dn, o_nsa, p3, w_out_bf16, ln_g, ln_b, w_router_t, b_router, *, tm, alpha):
    n, d = x.shape
    assert n % tm == 0
    bsz, seq, _ = p3.shape
    assert seq % tm == 0 or tm % seq == 0
    if seq % tm == 0:
        per_b = seq // tm
        gspec = lambda c: pl.BlockSpec((1, tm, d), lambda i: (i // per_b, i % per_b, c))
        p_in = p3
    else:
        p_in = p3.reshape(1, n, P_COLS)
        gspec = lambda c: pl.BlockSpec((1, tm, d), lambda i: (0, i, c))
    tok = pl.BlockSpec((tm, d), lambda i: (i, 0))
    full = lambda a: pl.BlockSpec(a.shape, lambda i: (0,) * a.ndim)
    rt = pl.BlockSpec((8, tm), lambda i: (0, i))
    kern = functools.partial(_post_mixer_kernel, tm=tm, alpha=alpha)
    g2 = ln_g.reshape(1, d)
    b2 = ln_b.reshape(1, d)
    br = b_router.reshape(N_EXPERTS, 1)
    return pl.pallas_call(
        kern,
        grid=(n // tm,),
        in_specs=[tok, tok, tok, gspec(C_MG // d), gspec(C_MG // d + 1), full(w_out_bf16), full(g2), full(b2),
                  full(w_router_t), full(br)],
        out_specs=[tok, rt, rt, rt, pl.BlockSpec((N_EXPERTS, LANES), lambda i: (0, 0))],
        out_shape=[jax.ShapeDtypeStruct((n, d), F32), jax.ShapeDtypeStruct((8, n), I32),
                   jax.ShapeDtypeStruct((8, n), F32), jax.ShapeDtypeStruct((8, n), I32),
                   jax.ShapeDtypeStruct((N_EXPERTS, LANES), F32)],
        scratch_shapes=[pltpu.VMEM((N_EXPERTS, LANES), F32)],
        compiler_params=_cparams(("arbitrary",)),
        name="merge_outproj_ln_router",
    )(x, o_dn, o_nsa, p_in, p_in, w_out_bf16, g2, b2, w_router_t, br)


def _slot_kernel(ps_ref, idx_ref, pos_ref, slot_ref):
    idx = idx_ref[...]
    acc = pos_ref[...]
    for e in range(N_EXPERTS):
        acc = acc + jnp.where(idx == e, ps_ref[e], 0)
    slot_ref[...] = jnp.where(_iota(idx.shape, 0) < TOP_K, acc, 0)


def _slots(pad_start, idx, pos):
    n = idx.shape[1]
    blk = pl.BlockSpec((8, n), lambda i, ps: (0, 0))
    return pl.pallas_call(
        _slot_kernel,
        grid_spec=pltpu.PrefetchScalarGridSpec(num_scalar_prefetch=1, grid=(1,), in_specs=[blk, blk], out_specs=blk),
        out_shape=jax.ShapeDtypeStruct((8, n), I32),
        compiler_params=_cparams(("arbitrary",)),
        name="moe_slots",
    )(pad_start, idx, pos)


def _dispatch_kernel(slot_ref, x_ref, xs_in_ref, xs_ref, sem, *, tm):
    del xs_in_ref

    def row_copy(r, s):
        return pltpu.make_async_copy(x_ref.at[pl.ds(r, 1)], xs_ref.at[pl.ds(s, 1)], sem)

    def issue(r, carry):
        for k in range(TOP_K):
            row_copy(r, slot_ref[k, r]).start()
        return carry

    lax.fori_loop(0, tm, issue, 0)

    def drain(r, carry):
        for k in range(TOP_K):
            row_copy(0, 0).wait()
        return carry

    lax.fori_loop(0, tm, drain, 0)


def _dispatch(x1, slot, n_slots, *, tm):
    n, d = x1.shape
    assert n % tm == 0
    kern = functools.partial(_dispatch_kernel, tm=tm)
    xs0 = jnp.zeros((n_slots, d), F32)
    return pl.pallas_call(
        kern,
        grid=(n // tm,),
        in_specs=[pl.BlockSpec((8, tm), lambda i: (0, i), memory_space=pltpu.SMEM),
                  pl.BlockSpec((tm, d), lambda i: (i, 0)),
                  pl.BlockSpec(memory_space=pl.ANY)],
        out_specs=pl.BlockSpec(memory_space=pl.ANY),
        out_shape=jax.ShapeDtypeStruct((n_slots, d), F32),
        scratch_shapes=[pltpu.SemaphoreType.DMA(())],
        input_output_aliases={2: 0},
        compiler_params=_cparams(("arbitrary",)),
        name="moe_dispatch",
    )(slot, x1, xs0)


def _expert_kernel(be_ref, nu_ref, x_ref, wg_ref, wu_ref, wd_ref, y_ref):
    i = pl.program_id(0)

    @pl.when(i < nu_ref[0])
    def _():
        x = x_ref[...].astype(BF16)
        hg = jnp.dot(x, wg_ref[0].astype(BF16), preferred_element_type=F32)
        hu = jnp.dot(x, wu_ref[0].astype(BF16), preferred_element_type=F32)
        y_ref[...] = _bdot(_silu(hg) * hu, wd_ref[0])

    @pl.when(i >= nu_ref[0])
    def _():
        y_ref[...] = jnp.zeros(y_ref.shape, F32)


def _experts(xs, blk_exp, n_used, w_gate, w_up, w_down, *, blk):
    n_slots, d = xs.shape
    de = w_gate.shape[2]
    n_blocks = n_slots // blk
    return pl.pallas_call(
        _expert_kernel,
        grid_spec=pltpu.PrefetchScalarGridSpec(
            num_scalar_prefetch=2,
            grid=(n_blocks,),
            in_specs=[pl.BlockSpec((blk, d), lambda i, be, nu: (jnp.minimum(i, nu[0] - 1), 0)),
                      pl.BlockSpec((1, d, de), lambda i, be, nu: (be[i], 0, 0)),
                      pl.BlockSpec((1, d, de), lambda i, be, nu: (be[i], 0, 0)),
                      pl.BlockSpec((1, de, d), lambda i, be, nu: (be[i], 0, 0))],
            out_specs=pl.BlockSpec((blk, d), lambda i, be, nu: (i, 0))),
        out_shape=jax.ShapeDtypeStruct((n_slots, d), F32),
        compiler_params=_cparams(("arbitrary",)),
        name="moe_experts",
    )(blk_exp, n_used, xs, w_gate, w_up, w_down)


def _combine_kernel(slot_ref, x_ref, w_ref, ys_ref, wsg_ref, wsu_ref, wsd_ref, g_ref, b_ref, o_ref, buf, sem,
                    *, tm, alpha):
    def row_copy(s, k, r):
        return pltpu.make_async_copy(ys_ref.at[pl.ds(s, 1)], buf.at[k, pl.ds(r, 1)], sem)

    def issue(r, carry):
        for k in range(TOP_K):
            row_copy(slot_ref[k, r], k, r).start()
        return carry

    lax.fori_loop(0, tm, issue, 0)
    x = x_ref[...]
    xb = x.astype(BF16)
    hs = _silu(jnp.dot(xb, wsg_ref[...], preferred_element_type=F32)) * jnp.dot(xb, wsu_ref[...],
                                                                               preferred_element_type=F32)
    acc = alpha * x + _bdot(hs, wsd_ref[...])

    def drain(r, carry):
        for k in range(TOP_K):
            row_copy(0, k, 0).wait()
        return carry

    lax.fori_loop(0, tm, drain, 0)
    w = w_ref[...]
    for k in range(TOP_K):
        acc = acc + w[:, k:k + 1] * buf[k]
    o_ref[...] = _layer_norm(acc, g_ref[...], b_ref[...])


def _combine(x1, slot, w_tok, ys, ws_gate, ws_up, ws_down, ln_g, ln_b, *, tm, alpha):
    n, d = x1.shape
    assert n % tm == 0
    kern = functools.partial(_combine_kernel, tm=tm, alpha=alpha)
    full = lambda a: pl.BlockSpec(a.shape, lambda i: (0,) * a.ndim)
    g2 = ln_g.reshape(1, d)
    b2 = ln_b.reshape(1, d)
    return pl.pallas_call(
        kern,
        grid=(n // tm,),
        in_specs=[pl.BlockSpec((8, tm), lambda i: (0, i), memory_space=pltpu.SMEM),
                  pl.BlockSpec((tm, d), lambda i: (i, 0)),
                  pl.BlockSpec((tm, 8), lambda i: (i, 0)),
                  pl.BlockSpec(memory_space=pl.ANY),
                  full(ws_gate), full(ws_up), full(ws_down), full(g2), full(b2)],
        out_specs=pl.BlockSpec((tm, d), lambda i: (i, 0)),
        out_shape=jax.ShapeDtypeStruct((n, d), F32),
        scratch_shapes=[pltpu.VMEM((TOP_K, tm, d), F32), pltpu.SemaphoreType.DMA(())],
        compiler_params=_cparams(("arbitrary",)),
        name="moe_combine_ln",
    )(slot, x1, w_tok, ys, ws_gate, ws_up, ws_down, g2, b2)


def _moe_layer(x1, idx, wts, pos, counts, w_gate, w_up, w_down, ws_gate, ws_up, ws_down, ln_g, ln_b,
               *, blk, tm_d, tm_c, alpha):
    n = x1.shape[0]
    cnt = counts[:, 0].astype(I32)
    padded = (cnt + blk - 1) // blk * blk
    pad_end = jnp.cumsum(padded)
    slot = _slots((pad_end - padded).astype(I32), idx, pos)
    n_blocks = -(-(n * TOP_K) // blk) + N_EXPERTS
    blk_exp = jnp.minimum(jnp.sum(pad_end[None, :] <= (jnp.arange(n_blocks) * blk)[:, None], axis=1),
                          N_EXPERTS - 1).astype(I32)
    n_used = (pad_end[-1:] // blk).astype(I32)
    xs = _dispatch(x1, slot, n_blocks * blk, tm=tm_d)
    ys = _experts(xs, blk_exp, n_used, w_gate, w_up, w_down, blk=blk)
    return _combine(x1, slot, wts.T, ys, ws_gate.astype(BF16), ws_up.astype(BF16), ws_down.astype(BF16),
                    ln_g, ln_b, tm=tm_c, alpha=alpha)


def kernel(x_prompt, x_sample, cache_cmp_kv, cache_slc_kv, cache_win_kv, state_delta_S, state_delta_conv, page_table, w_in, dn_conv_w, dn_A_log, dn_dt_bias, dn_norm_w, nsa_cmp_w1, nsa_cmp_pos, nsa_cmp_w2, w_out, ln1_g, ln1_b, w_router, b_router, w_exp_gate, w_exp_up, w_exp_down, w_sh_gate, w_sh_up, w_sh_down, ln2_g, ln2_b):
    depth = w_in.shape[0]
    assert depth == 1
    alpha = (2.0 * depth) ** 0.25
    bsz, seq, d = x_prompt.shape
    sb, st, _ = x_sample.shape
    tl = _tiles(bsz * seq, seq, sb * st)
    w_r = _reorder_w_in(w_in[0])
    wf, w2bd = _cmp_weights(nsa_cmp_w1[0], nsa_cmp_w2[0])
    mix_w = (w_r, dn_conv_w[0], dn_A_log[0], dn_dt_bias[0], dn_norm_w[0], wf, w2bd, nsa_cmp_pos[0], nsa_cmp_w1[0])
    p3, o_dn, s_p, o_nsa = _prompt_mixers(x_prompt, *mix_w, tl)
    ps, o_dn_s, s_s, o_nsa_s = _sample_mixers(x_sample, cache_cmp_kv[0], cache_slc_kv[0], cache_win_kv[0],
                                              state_delta_S[0], state_delta_conv[0], page_table, *mix_w, tl)
    wo = w_out[0].astype(BF16)
    wrt = w_router[0].T

    def ffn(x2, o_dn2, o_nsa2, p_any, tm, blk, tm_d, tm_c):
        x1, idx, wts, pos, counts = _post_mixer(x2, o_dn2, o_nsa2, p_any, wo, ln1_g[0], ln1_b[0], wrt, b_router[0],
                                                tm=tm, alpha=alpha)
        return _moe_layer(x1, idx, wts, pos, counts, w_exp_gate[0], w_exp_up[0], w_exp_down[0],
                          w_sh_gate[0], w_sh_up[0], w_sh_down[0], ln2_g[0], ln2_b[0],
                          blk=blk, tm_d=tm_d, tm_c=tm_c, alpha=alpha)

    y_p = ffn(x_prompt.reshape(-1, d), o_dn.reshape(-1, d), o_nsa.reshape(-1, d), p3,
              tl["post_tm"], tl["moe_blk"], tl["moe_tm_dispatch"], tl["moe_tm_combine"])
    y_s = ffn(x_sample.reshape(-1, d), o_dn_s[:, :st].reshape(-1, d), o_nsa_s[:, :st].reshape(-1, d), ps,
              tl["sample_tm"], tl["sample_moe_blk"], tl["sample_tm"], tl["sample_tm"])

    kv_shape = (2, NSA_KV_HEADS, NSA_HEAD_DIM)

    def kv_rows(pp, c0):
        return pp[:, :, c0:c0 + KV_COLS].reshape(pp.shape[:2] + kv_shape)

    nconv = CONV_W - 1
    conv_p = jnp.concatenate([jnp.zeros((bsz, nconv, 3 * DN_WIDTH), F32), p3[:, :, :3 * DN_WIDTH]], 1)[:, -nconv:]
    conv_s = jnp.concatenate([state_delta_conv[0], ps[:, :, :3 * DN_WIDTH]], 1)[:, -nconv:]
    past = page_table.shape[1] * PAGE_SIZE
    win_s = jnp.concatenate([cache_win_kv[0], kv_rows(ps, C_KVW)], 1)[:, -min(WINDOW, past + st):]
    return (y_p.reshape(x_prompt.shape), y_s.reshape(x_sample.shape),
            kv_rows(p3, C_KVC)[None], kv_rows(p3, C_KVS)[None], kv_rows(p3, C_KVW)[:, -min(WINDOW, seq):][None],
            s_p[None], conv_p[None],
            kv_rows(ps, C_KVC)[None], kv_rows(ps, C_KVS)[None], win_s[None], s_s[None], conv_s[None])
```

```python
import functools
import math

import jax
import jax.numpy as jnp
import numpy as np
from jax import lax
from jax.experimental import pallas as pl
from jax.experimental.pallas import tpu as pltpu

F32 = jnp.float32
BF16 = jnp.bfloat16
I32 = jnp.int32
HIGHEST = lax.Precision.HIGHEST

D_MODEL = 1024
PAGE_SIZE = 128
DN_HEADS = 8
DN_HEAD_DIM = 128
DN_WIDTH = DN_HEADS * DN_HEAD_DIM
CONV_W = 4
DN_CHUNK = 64
NSA_HEADS = 16
NSA_KV_HEADS = 2
NSA_GROUP = NSA_HEADS // NSA_KV_HEADS
NSA_HEAD_DIM = 64
NSA_WIDTH = NSA_HEADS * NSA_HEAD_DIM
NSA_KV_WIDTH = NSA_KV_HEADS * NSA_HEAD_DIM
KV_COLS = 2 * NSA_KV_WIDTH
CMP_BLOCK = 32
CMP_STRIDE = 16
SLC_BLOCK = 64
N_SELECT = 16
WINDOW = 512
N_EXPERTS = 64
TOP_K = 6
N_GROUPS = 8
TOPK_GROUPS = 4
ROUTED_SCALE = 2.5
LN_EPS = 1e-5
RMS_EPS = 1e-6
NEG = -1e30
LOG2E = math.log2(math.e)
FORCE_BONUS = 1e6

C_QKV = 0
C_Z = 3072
C_NQ = 4096
C_MG = 5120
C_KVC = 7168
C_KVS = 7424
C_KVW = 7680
C_SMALL = 7936
P_COLS = 8064
SM_A = 0
SM_B = DN_HEADS
SM_NG = 2 * DN_HEADS

LANES = 128
VMEM_LIMIT = 48 * 1024 * 1024
KEY_TILE = 256


def _tiles(n_prompt_tokens, seq, n_sample_tokens):
    return dict(
        proj_tm=min(512, n_prompt_tokens), proj_tn=P_COLS // 3,
        dn_tb=min(512, seq), dn_heads=4,
        nsa_tq=256,
        cmp_pages=32,
        post_tm=min(256, n_prompt_tokens),
        moe_blk=256, moe_tm_dispatch=min(256, n_prompt_tokens), moe_tm_combine=min(128, n_prompt_tokens),
        sample_moe_blk=64, sample_tm=n_sample_tokens,
    )


def _cparams(sem):
    return pltpu.CompilerParams(dimension_semantics=sem, vmem_limit_bytes=VMEM_LIMIT)


def _bdot(a, b):
    return jnp.dot(a.astype(BF16), b.astype(BF16), preferred_element_type=F32)


def _bdot_nt(a, b):
    return lax.dot_general(a.astype(BF16), b.astype(BF16), (((1,), (1,)), ((), ())),
                           preferred_element_type=F32)


def _bdot_tn(a, b):
    return lax.dot_general(a.astype(BF16), b.astype(BF16), (((0,), (0,)), ((), ())),
                           preferred_element_type=F32)


def _hdot(a, b):
    return jnp.dot(a, b, precision=HIGHEST, preferred_element_type=F32)


def _hdot_nt(a, b):
    return lax.dot_general(a, b, (((1,), (1,)), ((), ())), precision=HIGHEST,
                           preferred_element_type=F32)


def _sigmoid(x):
    return 1.0 / (1.0 + jnp.exp(-x))


def _silu(x):
    return x * _sigmoid(x)


def _softplus(x):
    return jnp.maximum(x, 0.0) + jnp.log(1.0 + jnp.exp(-jnp.abs(x)))


def _iota(shape, dim):
    return lax.broadcasted_iota(I32, shape, dim)


def _log2(n):
    assert n & (n - 1) == 0
    return int(math.log2(n))


def _mm_kernel(x_ref, w_ref, o_ref):
    o_ref[...] = jnp.dot(x_ref[...].astype(BF16), w_ref[...], preferred_element_type=F32)


def _matmul(x, w_bf16, tm, tn):
    m, k = x.shape
    n = w_bf16.shape[1]
    assert m % tm == 0 and n % tn == 0
    return pl.pallas_call(
        _mm_kernel,
        grid=(n // tn, m // tm),
        in_specs=[pl.BlockSpec((tm, k), lambda j, i: (i, 0)),
                  pl.BlockSpec((k, tn), lambda j, i: (0, j))],
        out_specs=pl.BlockSpec((tm, tn), lambda j, i: (i, j)),
        out_shape=jax.ShapeDtypeStruct((m, n), F32),
        compiler_params=_cparams(("parallel", "parallel")),
        name="dense_matmul",
    )(x, w_bf16)


def _reorder_w_in(w_in):
    o = 0
    seg = {}
    for name, size in (("qkv", 3 * DN_WIDTH), ("z", DN_WIDTH), ("a", DN_HEADS), ("b", DN_HEADS),
                       ("nq", NSA_WIDTH), ("kvc", KV_COLS), ("kvs", KV_COLS),
                       ("kvw", KV_COLS), ("ng", 3 * NSA_HEADS), ("mg", 2 * D_MODEL)):
        seg[name] = w_in[:, o:o + size]
        o += size
    assert o == w_in.shape[1]
    pad = jnp.zeros((w_in.shape[0], P_COLS - C_SMALL - SM_NG - 3 * NSA_HEADS), w_in.dtype)
    w = jnp.concatenate([seg["qkv"], seg["z"], seg["nq"], seg["mg"], seg["kvc"], seg["kvs"], seg["kvw"],
                         seg["a"], seg["b"], seg["ng"], pad], axis=1)
    assert w.shape[1] == P_COLS
    return w.astype(BF16)


def _tri_inverse(lmats, c):
    r = _iota((c, c), 0)
    q = _iota((c, c), 1)
    eye = (r == q).astype(F32)
    blk = min(16, c)
    shift = _log2(blk)
    same = (r >> shift) == (q >> shift)
    dmats = [jnp.where(same, lm, 0.0) for lm in lmats]
    prods = [eye - dm for dm in dmats]
    dpows = dmats
    k = 2
    while k < blk:
        dpows = [_bdot(dp, dp) for dp in dpows]
        prods = [pr + _bdot(pr, dp) for pr, dp in zip(prods, dpows)]
        k *= 2
    if c == blk:
        return prods
    mmats = [_bdot(pr, lm - dm) for pr, lm, dm in zip(prods, lmats, dmats)]
    outers = [eye - mm for mm in mmats]
    mpows = mmats
    k = 2
    while k < c // blk:
        mpows = [_bdot(mp, mp) for mp in mpows]
        outers = [ou + _bdot(ou, mp) for ou, mp in zip(outers, mpows)]
        k *= 2
    return [_bdot(ou, pr) for ou, pr in zip(outers, prods)]


def _dn_kernel(q_ref, k_ref, v_ref, z_ref, sm_ref, hq_ref, hk_ref, hv_ref, cwq_ref, cwk_ref, cwv_ref,
               hp_ref, nw_ref, s0_ref, o_ref, sout_ref,
               s_scr, xp_scr, qn_scr, kn_scr, vn_scr, gb_scr, u_scr, w_scr, qe_scr, kd_scr, a_scr, eg_scr,
               *, tb, c, t_valid, hp):
    hb = pl.program_id(1)
    t = pl.program_id(2)
    nt = pl.num_programs(2)
    dk = DN_HEAD_DIM
    nc = tb // c

    @pl.when(t == 0)
    def _():
        s_scr[...] = s0_ref[0]
        xp_scr[0, 0:8, :] = hq_ref[0]
        xp_scr[1, 0:8, :] = hk_ref[0]
        xp_scr[2, 0:8, :] = hv_ref[0]

    rows = t * tb + _iota((tb, 1), 0)
    valid = rows < t_valid

    def conv(i, raw_ref, cw_ref):
        xp_scr[i, 8:8 + tb, :] = raw_ref[0]
        acc = xp_scr[i, 8:8 + tb, :] * cw_ref[CONV_W - 1:CONV_W, :]
        for j in range(CONV_W - 1):
            acc = acc + xp_scr[i, 8 - (CONV_W - 1) + j:8 - (CONV_W - 1) + j + tb, :] * cw_ref[j:j + 1, :]
        tail = xp_scr[i, tb:tb + 8, :]
        xp_scr[i, 0:8, :] = tail
        return _silu(acc)

    qc = conv(0, q_ref, cwq_ref)
    kc = conv(1, k_ref, cwk_ref)
    vc = conv(2, v_ref, cwv_ref)
    lane = _iota((1, LANES), 1)
    sm = sm_ref[0]
    for hh in range(hp):
        h = hb * hp + hh
        cs = slice(hh * dk, (hh + 1) * dk)
        qh = qc[:, cs]
        kh = kc[:, cs]
        qn = qh * lax.rsqrt(jnp.sum(qh * qh, -1, keepdims=True) + 1e-6) * (dk ** -0.5)
        kn = kh * lax.rsqrt(jnp.sum(kh * kh, -1, keepdims=True) + 1e-6)
        a_h = jnp.sum(jnp.where(lane == SM_A + h, sm, 0.0), -1, keepdims=True)
        b_h = jnp.sum(jnp.where(lane == SM_B + h, sm, 0.0), -1, keepdims=True)
        neg_a = -jnp.exp(jnp.sum(jnp.where(lane == h, hp_ref[0:1, :], 0.0), -1, keepdims=True))
        dtb = jnp.sum(jnp.where(lane == h, hp_ref[1:2, :], 0.0), -1, keepdims=True)
        g = neg_a * _softplus(a_h + dtb)
        beta = _sigmoid(b_h)
        qn_scr[hh] = jnp.where(valid, qn, 0.0)
        kn_scr[hh] = jnp.where(valid, kn, 0.0)
        vn_scr[hh] = jnp.where(valid, vc[:, cs], 0.0)
        gb_scr[hh] = jnp.where(lane == 0, jnp.where(valid, g, 0.0), jnp.where(valid, beta, 0.0))

    r = _iota((c, c), 0)
    q = _iota((c, c), 1)
    incl = r >= q
    strict = r > q

    where = [(hh, slice(ci * c, (ci + 1) * c)) for hh in range(hp) for ci in range(nc)]
    lmats, vbs, kbes = [], [], []
    for hh, rs in where:
        qi = qn_scr[hh, rs, :]
        ki = kn_scr[hh, rs, :]
        gb = gb_scr[hh, rs, :]
        gi = gb[:, 0:1]
        bi = gb[:, 1:2]
        g_row = jnp.sum(jnp.where(r == q, gi, 0.0), 0, keepdims=True)
        gcum_col = jnp.sum(jnp.where(incl, g_row, 0.0), 1, keepdims=True)
        gcum_row = jnp.sum(jnp.where(r <= q, gi, 0.0), 0, keepdims=True)
        decay = jnp.where(incl, jnp.exp(jnp.where(incl, gcum_col - gcum_row, 0.0)), 0.0)
        kb = ki * bi
        eg = jnp.exp(gcum_col)
        g_last = gcum_col[c - 1:c, :]
        lmats.append(jnp.where(strict, _bdot_nt(kb, ki) * decay, 0.0))
        vbs.append(vn_scr[hh, rs, :] * bi)
        kbes.append(kb * eg)
        a_scr[hh, rs, :] = jnp.where(incl, _bdot_nt(qi, ki) * decay, 0.0)
        qe_scr[hh, rs, :] = qi * eg
        kd_scr[hh, rs, :] = ki * jnp.exp(g_last - gcum_col)
        e0 = rs.start // c * 8
        eg_scr[hh, e0:e0 + 8, :] = jnp.broadcast_to(jnp.exp(g_last), (8, LANES))
    tms = _tri_inverse(lmats, c)
    for (hh, rs), tm, vb, kbe in zip(where, tms, vbs, kbes):
        u_scr[hh, rs, :] = _bdot(tm, vb)
        w_scr[hh, rs, :] = _bdot(tm, kbe)

    nw = nw_ref[...]

    def chunk(ci, carry):
        r0 = pl.multiple_of(ci * c, c)
        e0 = pl.multiple_of(ci * 8, 8)
        for hh in range(hp):
            s = s_scr[hh]
            wq = jnp.concatenate([w_scr[hh, pl.ds(r0, c), :], qe_scr[hh, pl.ds(r0, c), :]], axis=0)
            ws = _bdot(wq, s)
            v_new = u_scr[hh, pl.ds(r0, c), :] - ws[0:c]
            o = ws[c:2 * c] + _bdot(a_scr[hh, pl.ds(r0, c), :], v_new)
            s_scr[hh] = s * eg_scr[hh, pl.ds(e0, 8), :][0:1, :] + _bdot_tn(kd_scr[hh, pl.ds(r0, c), :], v_new)
            o = o * lax.rsqrt(jnp.mean(o * o, -1, keepdims=True) + RMS_EPS) * nw
            o_ref[0, pl.ds(r0, c), hh * dk:(hh + 1) * dk] = o * _silu(z_ref[0, pl.ds(r0, c), hh * dk:(hh + 1) * dk])
        return carry

    lax.fori_loop(0, nc, chunk, 0)

    @pl.when(t == nt - 1)
    def _():
        sout_ref[0] = s_scr[...]


def _deltanet(p3, hist, s0, conv_w, a_log, dt_bias, norm_w, *, t_valid, tb, c, hp):
    bsz, tpad, _ = p3.shape
    assert tpad % tb == 0 and tb % c == 0 and tb % 8 == 0 and DN_HEADS % hp == 0
    nt = tpad // tb
    dk = DN_HEAD_DIM
    wid = hp * dk
    cw = jnp.concatenate([conv_w, jnp.zeros((8 - CONV_W, conv_w.shape[1]), F32)], 0)
    hpar = jnp.zeros((8, LANES), F32).at[0, :DN_HEADS].set(a_log).at[1, :DN_HEADS].set(dt_bias)
    nw = norm_w.reshape(1, dk)
    nb = DN_WIDTH // wid

    tok = lambda off: pl.BlockSpec((1, tb, wid), lambda b, h, t: (b, t, off + h))
    his = lambda off: pl.BlockSpec((1, 8, wid), lambda b, h, t: (b, 0, off + h))
    cws = lambda off: pl.BlockSpec((8, wid), lambda b, h, t: (0, off + h))
    st = pl.BlockSpec((1, hp, dk, dk), lambda b, h, t: (b, h, 0, 0))
    kern = functools.partial(_dn_kernel, tb=tb, c=c, t_valid=t_valid, hp=hp)
    big = pltpu.VMEM((hp, tb, dk), F32)
    return pl.pallas_call(
        kern,
        grid=(bsz, DN_HEADS // hp, nt),
        in_specs=[tok(0), tok(nb), tok(2 * nb), tok(C_Z // wid),
                  pl.BlockSpec((1, tb, LANES), lambda b, h, t: (b, t, C_SMALL // LANES)),
                  his(0), his(nb), his(2 * nb), cws(0), cws(nb), cws(2 * nb),
                  pl.BlockSpec((8, LANES), lambda b, h, t: (0, 0)),
                  pl.BlockSpec((1, dk), lambda b, h, t: (0, 0)),
                  st],
        out_specs=[pl.BlockSpec((1, tb, wid), lambda b, h, t: (b, t, h)), st],
        out_shape=[jax.ShapeDtypeStruct((bsz, tpad, DN_WIDTH), F32),
                   jax.ShapeDtypeStruct((bsz, DN_HEADS, dk, dk), F32)],
        scratch_shapes=[pltpu.VMEM((hp, dk, dk), F32),
                        pltpu.VMEM((3, tb + 8, wid), F32),
                        big, big, big, big, big, big, big, big,
                        pltpu.VMEM((hp, tb, c), F32),
                        pltpu.VMEM((hp, (tb // c) * 8, LANES), F32)],
        compiler_params=_cparams(("parallel", "parallel", "arbitrary")),
        name="gated_deltanet",
    )(p3, p3, p3, p3, p3, hist, hist, hist, cw, cw, cw, hpar, nw, s0)


def _cmp_weights(w1, w2):
    w1r = w1.reshape(2, CMP_BLOCK // CMP_STRIDE, CMP_STRIDE, NSA_HEAD_DIM, NSA_HEAD_DIM)
    eye = jnp.eye(2, dtype=F32)
    wf = jnp.einsum("srpde,st,hg->pshdrtge", w1r, eye, eye)
    wf = wf.reshape(CMP_STRIDE * KV_COLS, 2 * KV_COLS)
    w2bd = jnp.einsum("sef,st,hg->shetgf", w2, eye, eye).reshape(KV_COLS, KV_COLS)
    return wf.astype(BF16), w2bd.astype(BF16)


def _cmp_epi_kernel(p_ref, pos_ref, w1_ref, w2_ref, o_ref):
    pm = p_ref[0]
    n = pm.shape[0]
    nxt = pltpu.roll(pm[:, KV_COLS:2 * KV_COLS], n - 1, 0)
    b_k = _hdot(pos_ref[0:1, :], w1_ref[0])
    b_v = _hdot(pos_ref[1:2, :], w1_ref[1])
    bias = jnp.concatenate([b_k, b_k, b_v, b_v], axis=-1)
    h = pm[:, 0:KV_COLS] + nxt + bias
    o_ref[0] = _bdot(jax.nn.gelu(h), w2_ref[...])


def _cmp_epilogue(pmat, pos, w1, w2bd):
    bsz, n_sub, wid = pmat.shape
    return pl.pallas_call(
        _cmp_epi_kernel,
        grid=(bsz,),
        in_specs=[pl.BlockSpec((1, n_sub, wid), lambda b: (b, 0, 0)),
                  pl.BlockSpec(pos.shape, lambda b: (0, 0)),
                  pl.BlockSpec(w1.shape, lambda b: (0, 0, 0)),
                  pl.BlockSpec(w2bd.shape, lambda b: (0, 0))],
        out_specs=pl.BlockSpec((1, n_sub, wid // 2), lambda b: (b, 0, 0)),
        out_shape=jax.ShapeDtypeStruct((bsz, n_sub, wid // 2), F32),
        compiler_params=_cparams(("parallel",)),
        name="nsa_compress_epilogue",
    )(pmat, pos, w1, w2bd)


def _cmp_paged_kernel(pt_ref, cache_ref, w_ref, o_ref, buf, rows_scr, sem, *, npg):
    i = pl.program_id(0)
    n = pl.num_programs(0)
    spp = PAGE_SIZE // CMP_STRIDE

    def page_copy(page, slot, j):
        return pltpu.make_async_copy(cache_ref.at[page], buf.at[slot, j], sem.at[slot])

    def issue(step, slot):
        for j in range(npg):
            page_copy(pt_ref[step * npg + j], slot, j).start()

    @pl.when(i == 0)
    def _():
        issue(0, 0)

    @pl.when(i + 1 < n)
    def _():
        issue(i + 1, (i + 1) % 2)

    slot = i % 2
    for j in range(npg):
        page_copy(0, slot, j).wait()
    halves = KV_COLS // LANES
    for j in range(npg):
        for hf in range(halves):
            rows_scr[hf, j * PAGE_SIZE:(j + 1) * PAGE_SIZE, :] = buf[slot, j, hf * LANES:(hf + 1) * LANES, :].T
    acc = jnp.zeros(o_ref.shape, F32)
    for p in range(CMP_STRIDE):
        for hf in range(halves):
            xs = rows_scr[hf, pl.ds(p, npg * spp, stride=CMP_STRIDE), :]
            w0 = p * KV_COLS + hf * LANES
            acc = acc + jnp.dot(xs.astype(BF16), w_ref[w0:w0 + LANES, :], preferred_element_type=F32)
    o_ref[...] = acc


def _cmp_paged(cache_t, page_table, wf, *, npg):
    n_pool, cols, psz = cache_t.shape
    bsz, n_pages = page_table.shape
    total = bsz * n_pages
    spp = psz // CMP_STRIDE
    assert total % npg == 0 and cols == KV_COLS and psz == PAGE_SIZE
    kern = functools.partial(_cmp_paged_kernel, npg=npg)
    return pl.pallas_call(
        kern,
        grid_spec=pltpu.PrefetchScalarGridSpec(
            num_scalar_prefetch=1,
            grid=(total // npg,),
            in_specs=[pl.BlockSpec(memory_space=pl.ANY),
                      pl.BlockSpec(wf.shape, lambda i, pt: (0, 0))],
            out_specs=pl.BlockSpec((npg * spp, wf.shape[1]), lambda i, pt: (i, 0)),
            scratch_shapes=[pltpu.VMEM((2, npg, cols, psz), F32), pltpu.VMEM((cols // LANES, npg * psz, LANES), F32),
                            pltpu.SemaphoreType.DMA((2,))]),
        out_shape=jax.ShapeDtypeStruct((total * spp, wf.shape[1]), F32),
        compiler_params=_cparams(("arbitrary",)),
        name="nsa_compress_paged",
    )(page_table.reshape(-1), cache_t, wf)


def _slope(head):
    return 2.0 ** (-8.0 * (head + 1) / NSA_HEADS)


def _gather_heads(q_ref, hk):
    g = NSA_GROUP
    dh = NSA_HEAD_DIM
    qs = jnp.concatenate([q_ref[0, :, (hk * g + i) * dh:(hk * g + i + 1) * dh] for i in range(g)], axis=0)
    return qs * (dh ** -0.5)


def _cmp_branch(qs, kc, vc, hk, valid_c, dist_c, tq):
    s_all = _bdot_nt(qs, kc)
    ps = []
    psum = None
    for i in range(NSA_GROUP):
        s = s_all[i * tq:(i + 1) * tq] - _slope(hk * NSA_GROUP + i) * dist_c
        s = jnp.where(valid_c, s, NEG)
        m = jnp.max(s, -1, keepdims=True)
        p = jnp.where(valid_c, jnp.exp(s - m), 0.0)
        p = p / jnp.maximum(jnp.sum(p, -1, keepdims=True), 1e-30)
        ps.append(p)
        psum = p if psum is None else psum + p
    return _bdot(jnp.concatenate(ps, axis=0), vc), psum


def _select_blocks(scores, n_slc):
    nh = len(scores)
    assert nh * n_slc <= LANES
    packed = jnp.concatenate(scores, axis=1)
    lane = _iota((1, nh * n_slc), 1)
    head = lane // n_slc
    sidx = lane - head * n_slc
    rank = jnp.zeros(packed.shape, F32)
    for sp in range(n_slc):
        col = packed[:, sp:sp + 1]
        for h in range(1, nh):
            col = jnp.where(head == h, packed[:, h * n_slc + sp:h * n_slc + sp + 1], col)
        beats = (col > packed) | ((col == packed) & (sp < sidx))
        rank = rank + jnp.where(beats, 1.0, 0.0)
    sel = jnp.where(rank < N_SELECT, 1.0, 0.0)
    return [sel[:, h * n_slc:(h + 1) * n_slc] for h in range(nh)]


def _slope_features(tq):
    out = np.zeros((NSA_KV_HEADS, NSA_GROUP * tq, NSA_HEAD_DIM), np.float32)
    for hk in range(NSA_KV_HEADS):
        for g in range(NSA_GROUP):
            rem = _slope(hk * NSA_GROUP + g) * LOG2E
            for i in range(3):
                piece = float(np.float32(rem).astype(jnp.bfloat16))
                out[hk, g * tq:(g + 1) * tq, 2 * i:2 * i + 2] = piece
                rem -= piece
    return jnp.asarray(out)


def _position_features(pos):
    lo = (pos % 256).astype(F32)
    hi = (pos - pos % 256).astype(F32)
    cols = jnp.stack([hi, lo, hi, lo, hi, lo], axis=1)
    return jnp.pad(cols, ((0, 0), (0, NSA_HEAD_DIM - 6)))


_NT = (((1,), (1,)), ((), ()))


def _cmp_branch_aug(q_aug, kc_aug, vc, valid_c, tq):
    raw = lax.dot_general(q_aug, kc_aug, _NT, preferred_element_type=F32)
    bias = jnp.where(valid_c, 0.0, NEG)
    ps = []
    psum = None
    for i in range(NSA_GROUP):
        s = raw[i * tq:(i + 1) * tq] + bias
        p = jnp.where(valid_c, jnp.exp2(s - jnp.max(s, -1, keepdims=True)), 0.0)
        p = p * (1.0 / jnp.maximum(jnp.sum(p, -1, keepdims=True), 1e-30))
        ps.append(p)
        psum = p if psum is None else psum + p
    return _bdot(jnp.concatenate(ps, axis=0), vc), psum


def _flash_branch(q_aug, kf, kv_ref, hk, t_lo, t_hi, bias_fn, m_scr, acc_scr, tq):
    g = NSA_GROUP
    dh = NSA_HEAD_DIM
    m_scr[...] = jnp.full(m_scr.shape, NEG, F32)
    acc_scr[...] = jnp.zeros(acc_scr.shape, F32)
    ones = jnp.ones((KEY_TILE, dh), F32)

    def body(i, carry):
        t = t_hi - 1 - i
        k0 = pl.multiple_of(t * KEY_TILE, KEY_TILE)
        k_aug = jnp.concatenate([kv_ref[0, pl.ds(k0, KEY_TILE), hk * dh:(hk + 1) * dh], kf], axis=1).astype(BF16)
        v = kv_ref[0, pl.ds(k0, KEY_TILE), NSA_KV_WIDTH + hk * dh:NSA_KV_WIDTH + (hk + 1) * dh]
        vaug = jnp.concatenate([v, ones], axis=1).astype(BF16)
        bias = bias_fn(k0)
        k0f = k0.astype(F32)
        half = g // 2
        raws = [lax.dot_general(q_aug[h * half * tq:(h + 1) * half * tq], k_aug, _NT, preferred_element_type=F32)
                for h in range(2)]
        for h in range(2):
            ps = []
            alphas = []
            for jj in range(half):
                j = h * half + jj
                rs = slice(j * tq, (j + 1) * tq)
                shift = k0f * (_slope(hk * g + j) * LOG2E)
                s = raws[h][jj * tq:(jj + 1) * tq] + bias
                m_old = m_scr[rs, :]
                m_new = jnp.maximum(m_old, jnp.max(s, -1, keepdims=True) + shift)
                alphas.append(jnp.exp2(m_old - m_new))
                ps.append(jnp.exp2(s - jnp.concatenate([m_new - shift] * (KEY_TILE // LANES), axis=1)).astype(BF16))
                m_scr[rs, :] = m_new
            hs = slice(h * half * tq, (h + 1) * half * tq)
            pv = jnp.dot(jnp.concatenate(ps, axis=0), vaug, preferred_element_type=F32)
            acc_scr[hs, :] = jnp.concatenate(alphas, axis=0) * acc_scr[hs, :] + pv
        return carry

    lax.fori_loop(0, t_hi - t_lo, body, 0)
    acc = acc_scr[...]
    return acc[:, 0:dh] / jnp.maximum(acc[:, dh:2 * dh], 1e-30)


def _gate_combine(sm, hk, o_c, o_s, o_w, o_ref, tq):
    g = NSA_GROUP
    dh = NSA_HEAD_DIM
    outs = []
    for i in range(g):
        c0 = SM_NG + (hk * g + i) * 3
        gt = _sigmoid(sm[:, c0:c0 + 3])
        rows = slice(i * tq, (i + 1) * tq)
        outs.append(gt[:, 0:1] * o_c[rows] + gt[:, 1:2] * o_s[rows] + gt[:, 2:3] * o_w[rows])
    for i in range(0, g, 2):
        c0 = (hk * g + i) * dh
        o_ref[0, :, c0:c0 + 2 * dh] = jnp.concatenate([outs[i], outs[i + 1]], axis=-1)


def _nsa_prompt_kernel(q_ref, sm_ref, kvs_ref, kvw_ref, kc_ref, qsl_ref, kf_ref, kfc_ref, o_ref, m_scr, acc_scr,
                       *, tq, seq, n_cmp):
    dh = NSA_HEAD_DIM
    q0 = pl.program_id(1) * tq
    n_sub = kc_ref.shape[1]
    n_slc = seq // SLC_BLOCK
    qpos_i = q0 + _iota((tq, 1), 0)
    sm = sm_ref[0]
    t_hi = (q0 + tq + KEY_TILE - 1) // KEY_TILE
    t_lo_w = jnp.maximum(q0 - (WINDOW - 1), 0) // KEY_TILE

    cidx = _iota((1, n_sub), 1)
    valid_c = (cidx * CMP_STRIDE + (CMP_BLOCK - 1) <= qpos_i) & (cidx < n_cmp)
    cr = _iota((n_sub, n_slc), 0) * CMP_STRIDE
    s_st = _iota((n_sub, n_slc), 1) * SLC_BLOCK
    cover = jnp.where((cr < s_st + SLC_BLOCK) & (cr + (CMP_BLOCK - 1) >= s_st), 1.0, 0.0)
    sidx = _iota((1, n_slc), 1)
    cur = qpos_i >> _log2(SLC_BLOCK)
    forced = (sidx == 0) | (sidx == cur) | (sidx == cur - 1)
    bonus = jnp.where(forced, FORCE_BONUS, 0.0)
    past_ok = sidx * SLC_BLOCK <= qpos_i
    kf = kf_ref[...]

    q_augs, o_cs, scores = [], [], []
    for hk in range(NSA_KV_HEADS):
        q_aug = jnp.concatenate([_gather_heads(q_ref, hk) * LOG2E, qsl_ref[hk]], axis=1).astype(BF16)
        kc_aug = jnp.concatenate([kc_ref[0, :, hk * dh:(hk + 1) * dh], kfc_ref[...]], axis=1).astype(BF16)
        vc = kc_ref[0, :, NSA_KV_WIDTH + hk * dh:NSA_KV_WIDTH + (hk + 1) * dh]
        o_c, psum = _cmp_branch_aug(q_aug, kc_aug, vc, valid_c, tq)
        q_augs.append(q_aug)
        o_cs.append(o_c)
        scores.append(jnp.where(past_ok, _hdot(psum, cover) + bonus, NEG))
    sels = _select_blocks(scores, n_slc)

    for hk in range(NSA_KV_HEADS):
        def slc_bias(k0, sel=sels[hk]):
            srow = _iota((n_slc, KEY_TILE), 0)
            kblk = (k0 + _iota((n_slc, KEY_TILE), 1)) >> _log2(SLC_BLOCK)
            expand = jnp.where(srow == kblk, 1.0, 0.0)
            dist = qpos_i - (k0 + _iota((1, KEY_TILE), 1))
            return jnp.where((_bdot(sel, expand) > 0.5) & (dist >= 0), 0.0, NEG)

        def win_bias(k0):
            dist = qpos_i - (k0 + _iota((1, KEY_TILE), 1))
            return jnp.where((dist >= 0) & (dist < WINDOW), 0.0, NEG)

        o_s = _flash_branch(q_augs[hk], kf, kvs_ref, hk, 0, t_hi, slc_bias, m_scr, acc_scr, tq)
        o_w = _flash_branch(q_augs[hk], kf, kvw_ref, hk, t_lo_w, t_hi, win_bias, m_scr, acc_scr, tq)
        _gate_combine(sm, hk, o_cs[hk], o_s, o_w, o_ref, tq)


def _nsa_prompt(p3, kcvc, *, tq):
    bsz, seq, _ = p3.shape
    n_sub = kcvc.shape[1]
    assert seq % KEY_TILE == 0 and seq % tq == 0 and seq % SLC_BLOCK == 0 and KEY_TILE % tq == 0
    assert seq + CMP_BLOCK < 256 * 256, "positions are split into two bf16-exact parts"
    kern = functools.partial(_nsa_prompt_kernel, tq=tq, seq=seq, n_cmp=seq // CMP_STRIDE - 1)
    rows = NSA_GROUP * tq
    qsl = _slope_features(tq)
    kf = _position_features(jnp.arange(KEY_TILE))
    kfc = _position_features(jnp.arange(n_sub) * CMP_STRIDE + (CMP_BLOCK - 1))
    full = lambda a: pl.BlockSpec(a.shape, lambda b, j: (0,) * a.ndim)
    return pl.pallas_call(
        kern,
        grid=(bsz, seq // tq),
        in_specs=[pl.BlockSpec((1, tq, NSA_WIDTH), lambda b, j: (b, j, C_NQ // NSA_WIDTH)),
                  pl.BlockSpec((1, tq, LANES), lambda b, j: (b, j, C_SMALL // LANES)),
                  pl.BlockSpec((1, seq, KV_COLS), lambda b, j: (b, 0, C_KVS // KV_COLS)),
                  pl.BlockSpec((1, seq, KV_COLS), lambda b, j: (b, 0, C_KVW // KV_COLS)),
                  pl.BlockSpec((1, n_sub, KV_COLS), lambda b, j: (b, 0, 0)),
                  full(qsl), full(kf), full(kfc)],
        out_specs=pl.BlockSpec((1, tq, NSA_WIDTH), lambda b, j: (b, j, 0)),
        out_shape=jax.ShapeDtypeStruct((bsz, seq, NSA_WIDTH), F32),
        scratch_shapes=[pltpu.VMEM((rows, LANES), F32), pltpu.VMEM((rows, 2 * NSA_HEAD_DIM), F32)],
        compiler_params=_cparams(("parallel", "arbitrary")),
        name="nsa_prompt_attention",
    )(p3, p3, p3, p3, kcvc, qsl, kf, kfc)


def _nsa_select_kernel(q_ref, kc_ref, oc_ref, sel_ref, *, tq, past, n_cmp, n_slc, n_slc_pad):
    dh = NSA_HEAD_DIM
    n_sub = kc_ref.shape[1]
    qpos_i = past + _iota((tq, 1), 0)
    qpos = qpos_i.astype(F32)
    cidx = _iota((1, n_sub), 1)
    c_end = cidx * CMP_STRIDE + (CMP_BLOCK - 1)
    valid_c = (c_end <= qpos_i) & (cidx < n_cmp)
    dist_c = qpos - c_end.astype(F32)
    cr = _iota((n_sub, n_slc_pad), 0) * CMP_STRIDE
    s_st = _iota((n_sub, n_slc_pad), 1) * SLC_BLOCK
    cover = jnp.where((cr < s_st + SLC_BLOCK) & (cr + (CMP_BLOCK - 1) >= s_st), 1.0, 0.0)
    sidx = _iota((1, n_slc_pad), 1)
    sidx_f = sidx.astype(F32)
    cur = qpos_i >> _log2(SLC_BLOCK)
    forced = (sidx == 0) | (sidx == cur) | (sidx == cur - 1)
    bonus = jnp.where(forced, FORCE_BONUS, 0.0)
    past_ok = sidx * SLC_BLOCK <= qpos_i
    lane = _iota((1, LANES), 1)
    for hk in range(NSA_KV_HEADS):
        qs = _gather_heads(q_ref, hk)
        kc = kc_ref[0, :, hk * dh:(hk + 1) * dh]
        vc = kc_ref[0, :, NSA_KV_WIDTH + hk * dh:NSA_KV_WIDTH + (hk + 1) * dh]
        o_c, psum = _cmp_branch(qs, kc, vc, hk, valid_c, dist_c, tq)
        oc_ref[0, hk] = o_c
        imp = _hdot(psum, cover)
        score = jnp.where(past_ok, imp + bonus, NEG)
        score = jnp.where(sidx < n_slc, score, -jnp.inf)
        res = jnp.zeros((tq, LANES), F32)
        for it in range(min(N_SELECT, n_slc)):
            m = jnp.max(score, -1, keepdims=True)
            idx = jnp.min(jnp.where(score == m, sidx_f, 1e9), -1, keepdims=True)
            res = jnp.where(lane == it, idx, res)
            score = jnp.where(sidx_f == idx, -jnp.inf, score)
        sel_ref[0, hk] = res.astype(I32)


def _nsa_select(ps3, kcvc, *, past, n_cmp, n_slc):
    bsz, tq, _ = ps3.shape
    n_sub = kcvc.shape[1]
    n_slc_pad = -(-n_slc // LANES) * LANES
    kern = functools.partial(_nsa_select_kernel, tq=tq, past=past, n_cmp=n_cmp, n_slc=n_slc, n_slc_pad=n_slc_pad)
    rows = NSA_GROUP * tq
    return pl.pallas_call(
        kern,
        grid=(bsz,),
        in_specs=[pl.BlockSpec((1, tq, NSA_WIDTH), lambda b: (b, 0, C_NQ // NSA_WIDTH)),
                  pl.BlockSpec((1, n_sub, KV_COLS), lambda b: (b, 0, 0))],
        out_specs=[pl.BlockSpec((1, NSA_KV_HEADS, rows, NSA_HEAD_DIM), lambda b: (b, 0, 0, 0)),
                   pl.BlockSpec((1, NSA_KV_HEADS, tq, LANES), lambda b: (b, 0, 0, 0))],
        out_shape=[jax.ShapeDtypeStruct((bsz, NSA_KV_HEADS, rows, NSA_HEAD_DIM), F32),
                   jax.ShapeDtypeStruct((bsz, NSA_KV_HEADS, tq, LANES), I32)],
        compiler_params=_cparams(("parallel",)),
        name="nsa_sample_select",
    )(ps3, kcvc)


def _joint_softmax_pv(parts, hk, tq):
    g = NSA_GROUP
    outs = []
    for j in range(g):
        rs = slice(j * tq, (j + 1) * tq)
        slope = _slope(hk * g + j)
        ss = [jnp.where(valid, s_all[rs] - slope * dist, NEG) for s_all, valid, dist, _, _ in parts]
        m = None
        for s in ss:
            mi = jnp.max(s, -1, keepdims=True)
            m = mi if m is None else jnp.maximum(m, mi)
        num = None
        den = None
        for s, (_, valid, _, v, v_t) in zip(ss, parts):
            p = jnp.where(valid, jnp.exp(s - m), 0.0)
            d = jnp.sum(p, -1, keepdims=True)
            o = _bdot_nt(p, v) if v_t else _bdot(p, v)
            num = o if num is None else num + o
            den = d if den is None else den + d
        outs.append(num / jnp.maximum(den, 1e-30))
    return jnp.concatenate(outs, axis=0)


def _nsa_sample_kernel(phys_ref, q_ref, sm_ref, kpos_ref, tail_ref, wcache_ref, wnew_ref, oc_ref, cache_ref,
                       o_ref, kbuf, vbuf, sem, *, tq, t_valid, past, n_gather):
    dh = NSA_HEAD_DIM
    b = pl.program_id(0)
    per_b = NSA_KV_HEADS * n_gather

    def page_copies(page, hk, i):
        dst = pl.ds(i * PAGE_SIZE, PAGE_SIZE)
        return (pltpu.make_async_copy(cache_ref.at[page, pl.ds(hk * dh, dh), :], kbuf.at[hk, :, dst], sem),
                pltpu.make_async_copy(cache_ref.at[page, pl.ds(NSA_KV_WIDTH + hk * dh, dh), :], vbuf.at[hk, :, dst], sem))

    for hk in range(NSA_KV_HEADS):
        for i in range(n_gather):
            for cp in page_copies(phys_ref[b * per_b + hk * n_gather + i], hk, i):
                cp.start()

    qpos_i = past + _iota((tq, 1), 0)
    qpos = qpos_i.astype(F32)
    sm = sm_ref[0]
    n_keys = n_gather * PAGE_SIZE
    per_q = n_keys // t_valid
    new_ok = _iota((1, tq), 1) < t_valid
    dist_new = qpos - (past + _iota((1, tq), 1)).astype(F32)
    n_win = wcache_ref.shape[2]
    dist_wc = qpos - (past - n_win + _iota((1, n_win), 1)).astype(F32)
    ok_wc = (dist_wc >= 0.0) & (dist_wc < float(WINDOW))
    ok_wn = (dist_new >= 0.0) & (dist_new < float(WINDOW)) & new_ok
    ok_t = (dist_new >= 0.0) & new_ok
    own = (_iota((tq, n_keys), 1) >> _log2(per_q)) == _iota((tq, n_keys), 0)

    qss = [_gather_heads(q_ref, hk) for hk in range(NSA_KV_HEADS)]
    win = []
    for hk in range(NSA_KV_HEADS):
        kw_t = wcache_ref[0, hk * dh:(hk + 1) * dh, :]
        vw_t = wcache_ref[0, NSA_KV_WIDTH + hk * dh:NSA_KV_WIDTH + (hk + 1) * dh, :]
        kn = wnew_ref[0, :, hk * dh:(hk + 1) * dh]
        vn = wnew_ref[0, :, NSA_KV_WIDTH + hk * dh:NSA_KV_WIDTH + (hk + 1) * dh]
        win.append(_joint_softmax_pv([(_bdot(qss[hk], kw_t), ok_wc, dist_wc, vw_t, True),
                                      (_bdot_nt(qss[hk], kn), ok_wn, dist_new, vn, False)], hk, tq))

    for hk in range(NSA_KV_HEADS):
        for i in range(n_gather):
            for cp in page_copies(0, hk, i):
                cp.wait()

    for hk in range(NSA_KV_HEADS):
        kt = tail_ref[0, :, hk * dh:(hk + 1) * dh]
        vt = tail_ref[0, :, NSA_KV_WIDTH + hk * dh:NSA_KV_WIDTH + (hk + 1) * dh]
        dist_p = qpos - kpos_ref[0, hk]
        ok_p = own & (dist_p >= 0.0)
        o_s = _joint_softmax_pv([(_bdot(qss[hk], kbuf[hk]), ok_p, dist_p, vbuf[hk], True),
                                 (_bdot_nt(qss[hk], kt), ok_t, dist_new, vt, False)], hk, tq)
        _gate_combine(sm, hk, oc_ref[0, hk], o_s, win[hk], o_ref, tq)


def _nsa_sample(ps3, o_c, phys, kpos, cache_t, win_t, *, t_valid, past):
    bsz, tq, _ = ps3.shape
    n_gather = t_valid * N_SELECT
    rows = NSA_GROUP * tq
    n_keys = n_gather * PAGE_SIZE
    kern = functools.partial(_nsa_sample_kernel, tq=tq, t_valid=t_valid, past=past, n_gather=n_gather)
    return pl.pallas_call(
        kern,
        grid_spec=pltpu.PrefetchScalarGridSpec(
            num_scalar_prefetch=1,
            grid=(bsz,),
            in_specs=[pl.BlockSpec((1, tq, NSA_WIDTH), lambda b, ph: (b, 0, C_NQ // NSA_WIDTH)),
                      pl.BlockSpec((1, tq, LANES), lambda b, ph: (b, 0, C_SMALL // LANES)),
                      pl.BlockSpec((1, NSA_KV_HEADS, 1, n_keys), lambda b, ph: (b, 0, 0, 0)),
                      pl.BlockSpec((1, tq, KV_COLS), lambda b, ph: (b, 0, C_KVS // KV_COLS)),
                      pl.BlockSpec((1,) + win_t.shape[1:], lambda b, ph: (b, 0, 0)),
                      pl.BlockSpec((1, tq, KV_COLS), lambda b, ph: (b, 0, C_KVW // KV_COLS)),
                      pl.BlockSpec((1, NSA_KV_HEADS, rows, NSA_HEAD_DIM), lambda b, ph: (b, 0, 0, 0)),
                      pl.BlockSpec(memory_space=pl.ANY)],
            out_specs=pl.BlockSpec((1, tq, NSA_WIDTH), lambda b, ph: (b, 0, 0)),
            scratch_shapes=[pltpu.VMEM((NSA_KV_HEADS, NSA_HEAD_DIM, n_keys), F32),
                            pltpu.VMEM((NSA_KV_HEADS, NSA_HEAD_DIM, n_keys), F32),
                            pltpu.SemaphoreType.DMA(())]),
        out_shape=jax.ShapeDtypeStruct((bsz, tq, NSA_WIDTH), F32),
        compiler_params=_cparams(("arbitrary",)),
        name="nsa_sample_attention",
    )(phys, ps3, ps3, kpos, ps3, win_t, ps3, o_c, cache_t)


def _rows_transposed(cache):
    nd = cache.ndim
    perm = tuple(range(nd - 4)) + (nd - 3, nd - 2, nd - 1, nd - 4)
    t = jnp.transpose(cache, perm)
    return t.reshape(t.shape[:nd - 4] + (KV_COLS, cache.shape[nd - 4]))


def _prompt_mixers(x, w_r, conv_w, a_log, dt_bias, norm_w, cmp_wf, cmp_w2bd, cmp_pos, cmp_w1, tl):
    bsz, seq, _ = x.shape
    p = _matmul(x.reshape(bsz * seq, D_MODEL), w_r, tl["proj_tm"], tl["proj_tn"])
    p3 = p.reshape(bsz, seq, P_COLS)
    hist = jnp.zeros((bsz, 8, 3 * DN_WIDTH), F32)
    s0 = jnp.zeros((bsz, DN_HEADS, DN_HEAD_DIM, DN_HEAD_DIM), F32)
    o_dn, s_new = _deltanet(p3, hist, s0, conv_w, a_log, dt_bias, norm_w, t_valid=seq, tb=tl["dn_tb"], c=DN_CHUNK,
                            hp=tl["dn_heads"])
    kvc = p3[:, :, C_KVC:C_KVC + KV_COLS]
    n_sub = seq // CMP_STRIDE
    sub = kvc.reshape(bsz * n_sub, CMP_STRIDE * KV_COLS)
    pmat = _matmul(sub, cmp_wf, min(256, bsz * n_sub), cmp_wf.shape[1]).reshape(bsz, n_sub, -1)
    kcvc = _cmp_epilogue(pmat, cmp_pos.reshape(2, -1), cmp_w1, cmp_w2bd)
    o_nsa = _nsa_prompt(p3, kcvc, tq=tl["nsa_tq"])
    return p3, o_dn, s_new, o_nsa


def _sample_mixers(x, cache_cmp, cache_slc, win_buf, s0, conv_buf, page_table, w_r, conv_w, a_log, dt_bias,
                   norm_w, cmp_wf, cmp_w2bd, cmp_pos, cmp_w1, tl):
    bsz, t, _ = x.shape
    tq = 8
    n_pages = page_table.shape[1]
    past = n_pages * PAGE_SIZE
    assert t <= tq and t <= SLC_BLOCK and past % SLC_BLOCK == 0 and cache_cmp.shape[1] == PAGE_SIZE
    assert (past + t) // CMP_STRIDE * CMP_STRIDE == past, "new rows never complete a compression sub-block"
    ps = _matmul(x.reshape(bsz * t, D_MODEL), w_r, bsz * t, tl["proj_tn"]).reshape(bsz, t, P_COLS)
    ps3 = jnp.pad(ps, ((0, 0), (0, tq - t), (0, 0)))
    hist = jnp.pad(conv_buf, ((0, 0), (8 - (CONV_W - 1), 0), (0, 0)))
    o_dn, s_new = _deltanet(ps3, hist, s0, conv_w, a_log, dt_bias, norm_w, t_valid=t, tb=tq, c=tq, hp=DN_HEADS)
    n_sub = past // CMP_STRIDE
    pmat = _cmp_paged(_rows_transposed(cache_cmp), page_table, cmp_wf, npg=tl["cmp_pages"])
    kcvc = _cmp_epilogue(pmat.reshape(bsz, n_sub, -1), cmp_pos.reshape(2, -1), cmp_w1, cmp_w2bd)
    n_past_blocks = past // SLC_BLOCK
    o_c, sel = _nsa_select(ps3, kcvc, past=past, n_cmp=n_sub - 1, n_slc=n_past_blocks + 1)
    sel = sel[:, :, :t, :N_SELECT]
    bpp = PAGE_SIZE // SLC_BLOCK
    jp = jnp.minimum(sel, n_past_blocks - 1)
    page = jp // bpp
    phys = page_table[jnp.arange(bsz)[:, None, None, None], page]
    row = jnp.arange(PAGE_SIZE)
    in_blk = (row // SLC_BLOCK == (jp % bpp)[..., None]) & (sel < n_past_blocks)[..., None]
    kpos = jnp.where(in_blk, (page[..., None] * PAGE_SIZE + row).astype(F32), 1e9)
    kpos = kpos.reshape(bsz, NSA_KV_HEADS, 1, t * N_SELECT * PAGE_SIZE)
    o_nsa = _nsa_sample(ps3, o_c, phys.reshape(-1).astype(I32), kpos, _rows_transposed(cache_slc),
                        _rows_transposed(win_buf), t_valid=t, past=past)
    return ps, o_dn, s_new, o_nsa


def _layer_norm(x, g, b):
    xc = x - jnp.mean(x, -1, keepdims=True)
    var = jnp.mean(xc * xc, -1, keepdims=True)
    return xc * lax.rsqrt(var + LN_EPS) * g + b


def _rank_rows(v, n):
    ri = _iota(v.shape, 0)
    rank = jnp.zeros(v.shape, F32)
    for rp in range(n):
        row = v[rp:rp + 1, :]
        beats = (row > v) | ((row == v) & (rp < ri))
        rank = rank + jnp.where(beats, 1.0, 0.0)
    return rank


def _post_mixer_kernel(x_ref, odn_ref, onsa_ref, gdn_ref, gnsa_ref, wo_ref, g_ref, b_ref, wr_ref, br_ref,
                       x1_ref, idx_ref, wt_ref, pos_ref, cnt_ref, run_scr, *, tm, alpha):
    i = pl.program_id(0)

    @pl.when(i == 0)
    def _():
        run_scr[...] = jnp.zeros(run_scr.shape, F32)

    h = _sigmoid(gdn_ref[0]) * odn_ref[...] + _sigmoid(gnsa_ref[0]) * onsa_ref[...]
    x1 = _layer_norm(alpha * x_ref[...] + _bdot(h, wo_ref[...]), g_ref[...], b_ref[...])
    x1_ref[...] = x1

    ne = N_EXPERTS
    per = ne // N_GROUPS
    scores = _sigmoid(_hdot_nt(wr_ref[...], x1))
    s3 = (scores + br_ref[...]).reshape(N_GROUPS, per, tm)
    e3 = _iota((N_GROUPS, per, tm), 1).astype(F32)
    g1 = jnp.max(s3, axis=1, keepdims=True)
    first = jnp.min(jnp.where(s3 == g1, e3, float(per)), axis=1, keepdims=True)
    g2 = jnp.max(jnp.where(e3 == first, -jnp.inf, s3), axis=1, keepdims=True)
    grank = _rank_rows((g1 + g2).reshape(N_GROUPS, tm), N_GROUPS)
    keep = (grank < TOPK_GROUPS).reshape(N_GROUPS, 1, tm)
    selm = jnp.where(keep, s3, NEG).reshape(ne, tm)
    erank = _rank_rows(selm, ne)
    ei = _iota((ne, tm), 0).astype(F32)
    chosen = jnp.where(erank < TOP_K, 1.0, 0.0)
    tr = _iota((tm, tm), 0)
    tc = _iota((tm, tm), 1)
    before = jnp.where(tr < tc, 1.0, 0.0)
    pos_full = _bdot(chosen, before) + run_scr[:, 0:1]
    idx_rows, w_rows, pos_rows = [], [], []
    for k in range(TOP_K):
        hit = erank == float(k)
        idx_rows.append(jnp.sum(jnp.where(hit, ei, 0.0), 0, keepdims=True))
        w_rows.append(jnp.sum(jnp.where(hit, scores, 0.0), 0, keepdims=True))
        pos_rows.append(jnp.sum(jnp.where(hit, pos_full, 0.0), 0, keepdims=True))
    wsum = w_rows[0]
    for k in range(1, TOP_K):
        wsum = wsum + w_rows[k]
    zero = jnp.zeros((8 - TOP_K, tm), F32)
    idx_ref[...] = jnp.concatenate(idx_rows + [zero], 0).astype(I32)
    wt_ref[...] = jnp.concatenate([w / wsum * ROUTED_SCALE for w in w_rows] + [zero], 0)
    pos_ref[...] = jnp.concatenate(pos_rows + [zero], 0).astype(I32)
    run_scr[...] = run_scr[...] + jnp.sum(chosen, 1, keepdims=True)
    cnt_ref[...] = run_scr[...]


def _post_mixer(x, o_dn, o_nsa, p3, w_out_bf16, ln_g, ln_b, w_router_t, b_router, *, tm, alpha):
    n, d = x.shape
    assert n % tm == 0
    bsz, seq, _ = p3.shape
    assert seq % tm == 0 or tm % seq == 0
    if seq % tm == 0:
        per_b = seq // tm
        gspec = lambda c: pl.BlockSpec((1, tm, d), lambda i: (i // per_b, i % per_b, c))
        p_in = p3
    else:
        p_in = p3.reshape(1, n, P_COLS)
        gspec = lambda c: pl.BlockSpec((1, tm, d), lambda i: (0, i, c))
    tok = pl.BlockSpec((tm, d), lambda i: (i, 0))
    full = lambda a: pl.BlockSpec(a.shape, lambda i: (0,) * a.ndim)
    rt = pl.BlockSpec((8, tm), lambda i: (0, i))
    kern = functools.partial(_post_mixer_kernel, tm=tm, alpha=alpha)
    g2 = ln_g.reshape(1, d)
    b2 = ln_b.reshape(1, d)
    br = b_router.reshape(N_EXPERTS, 1)
    return pl.pallas_call(
        kern,
        grid=(n // tm,),
        in_specs=[tok, tok, tok, gspec(C_MG // d), gspec(C_MG // d + 1), full(w_out_bf16), full(g2), full(b2),
                  full(w_router_t), full(br)],
        out_specs=[tok, rt, rt, rt, pl.BlockSpec((N_EXPERTS, LANES), lambda i: (0, 0))],
        out_shape=[jax.ShapeDtypeStruct((n, d), F32), jax.ShapeDtypeStruct((8, n), I32),
                   jax.ShapeDtypeStruct((8, n), F32), jax.ShapeDtypeStruct((8, n), I32),
                   jax.ShapeDtypeStruct((N_EXPERTS, LANES), F32)],
        scratch_shapes=[pltpu.VMEM((N_EXPERTS, LANES), F32)],
        compiler_params=_cparams(("arbitrary",)),
        name="merge_outproj_ln_router",
    )(x, o_dn, o_nsa, p_in, p_in, w_out_bf16, g2, b2, w_router_t, br)


def _slot_kernel(ps_ref, idx_ref, pos_ref, slot_ref):
    idx = idx_ref[...]
    acc = pos_ref[...]
    for e in range(N_EXPERTS):
        acc = acc + jnp.where(idx == e, ps_ref[e], 0)
    slot_ref[...] = jnp.where(_iota(idx.shape, 0) < TOP_K, acc, 0)


def _slots(pad_start, idx, pos):
    n = idx.shape[1]
    blk = pl.BlockSpec((8, n), lambda i, ps: (0, 0))
    return pl.pallas_call(
        _slot_kernel,
        grid_spec=pltpu.PrefetchScalarGridSpec(num_scalar_prefetch=1, grid=(1,), in_specs=[blk, blk], out_specs=blk),
        out_shape=jax.ShapeDtypeStruct((8, n), I32),
        compiler_params=_cparams(("arbitrary",)),
        name="moe_slots",
    )(pad_start, idx, pos)


def _dispatch_kernel(slot_ref, x_ref, xs_in_ref, xs_ref, sem, *, tm):
    del xs_in_ref

    def row_copy(r, s):
        return pltpu.make_async_copy(x_ref.at[pl.ds(r, 1)], xs_ref.at[pl.ds(s, 1)], sem)

    def issue(r, carry):
        for k in range(TOP_K):
            row_copy(r, slot_ref[k, r]).start()
        return carry

    lax.fori_loop(0, tm, issue, 0, unroll=8)

    def drain(r, carry):
        for k in range(TOP_K):
            row_copy(0, 0).wait()
        return carry

    lax.fori_loop(0, tm, drain, 0, unroll=8)


def _dispatch(x1, slot, n_slots, *, tm):
    n, d = x1.shape
    assert n % tm == 0
    kern = functools.partial(_dispatch_kernel, tm=tm)
    xs0 = jnp.zeros((n_slots, d), F32)
    return pl.pallas_call(
        kern,
        grid=(n // tm,),
        in_specs=[pl.BlockSpec((8, tm), lambda i: (0, i), memory_space=pltpu.SMEM),
                  pl.BlockSpec((tm, d), lambda i: (i, 0)),
                  pl.BlockSpec(memory_space=pl.ANY)],
        out_specs=pl.BlockSpec(memory_space=pl.ANY),
        out_shape=jax.ShapeDtypeStruct((n_slots, d), F32),
        scratch_shapes=[pltpu.SemaphoreType.DMA(())],
        input_output_aliases={2: 0},
        compiler_params=_cparams(("arbitrary",)),
        name="moe_dispatch",
    )(slot, x1, xs0)


def _expert_kernel(be_ref, nu_ref, x_ref, wg_ref, wu_ref, wd_ref, y_ref, wg_b, wu_b, wd_b):
    i = pl.program_id(0)

    @pl.when((i == 0) | (be_ref[i] != be_ref[jnp.maximum(i - 1, 0)]))
    def _():
        wg_b[...] = wg_ref[0].astype(BF16)
        wu_b[...] = wu_ref[0].astype(BF16)
        wd_b[...] = wd_ref[0].astype(BF16)

    @pl.when(i < nu_ref[0])
    def _():
        x = x_ref[...].astype(BF16)
        hg = jnp.dot(x, wg_b[...], preferred_element_type=F32)
        hu = jnp.dot(x, wu_b[...], preferred_element_type=F32)
        y_ref[...] = jnp.dot((_silu(hg) * hu).astype(BF16), wd_b[...], preferred_element_type=F32)

    @pl.when(i >= nu_ref[0])
    def _():
        y_ref[...] = jnp.zeros(y_ref.shape, F32)


def _experts(xs, blk_exp, n_used, w_gate, w_up, w_down, *, blk):
    n_slots, d = xs.shape
    de = w_gate.shape[2]
    n_blocks = n_slots // blk
    return pl.pallas_call(
        _expert_kernel,
        grid_spec=pltpu.PrefetchScalarGridSpec(
            num_scalar_prefetch=2,
            grid=(n_blocks,),
            in_specs=[pl.BlockSpec((blk, d), lambda i, be, nu: (jnp.maximum(jnp.minimum(i, nu[0] - 1), 0), 0)),
                      pl.BlockSpec((1, d, de), lambda i, be, nu: (be[i], 0, 0)),
                      pl.BlockSpec((1, d, de), lambda i, be, nu: (be[i], 0, 0)),
                      pl.BlockSpec((1, de, d), lambda i, be, nu: (be[i], 0, 0))],
            out_specs=pl.BlockSpec((blk, d), lambda i, be, nu: (i, 0)),
            scratch_shapes=[pltpu.VMEM((d, de), BF16), pltpu.VMEM((d, de), BF16), pltpu.VMEM((de, d), BF16)]),
        out_shape=jax.ShapeDtypeStruct((n_slots, d), F32),
        compiler_params=_cparams(("arbitrary",)),
        name="moe_experts",
    )(blk_exp, n_used, xs, w_gate, w_up, w_down)


def _combine_kernel(slot_ref, x_ref, w_ref, ys_ref, wsg_ref, wsu_ref, wsd_ref, g_ref, b_ref, o_ref, buf, sem,
                    *, tm, alpha):
    def row_copy(s, k, r):
        return pltpu.make_async_copy(ys_ref.at[pl.ds(s, 1)], buf.at[k, pl.ds(r, 1)], sem)

    def issue(r, carry):
        for k in range(TOP_K):
            row_copy(slot_ref[k, r], k, r).start()
        return carry

    lax.fori_loop(0, tm, issue, 0, unroll=8)
    x = x_ref[...]
    xb = x.astype(BF16)
    hs = _silu(jnp.dot(xb, wsg_ref[...], preferred_element_type=F32)) * jnp.dot(xb, wsu_ref[...],
                                                                               preferred_element_type=F32)
    acc = alpha * x + _bdot(hs, wsd_ref[...])

    def drain(r, carry):
        for k in range(TOP_K):
            row_copy(0, k, 0).wait()
        return carry

    lax.fori_loop(0, tm, drain, 0, unroll=8)
    w = w_ref[...]
    for k in range(TOP_K):
        acc = acc + w[:, k:k + 1] * buf[k]
    o_ref[...] = _layer_norm(acc, g_ref[...], b_ref[...])


def _combine(x1, slot, w_tok, ys, ws_gate, ws_up, ws_down, ln_g, ln_b, *, tm, alpha):
    n, d = x1.shape
    assert n % tm == 0
    kern = functools.partial(_combine_kernel, tm=tm, alpha=alpha)
    full = lambda a: pl.BlockSpec(a.shape, lambda i: (0,) * a.ndim)
    g2 = ln_g.reshape(1, d)
    b2 = ln_b.reshape(1, d)
    return pl.pallas_call(
        kern,
        grid=(n // tm,),
        in_specs=[pl.BlockSpec((8, tm), lambda i: (0, i), memory_space=pltpu.SMEM),
                  pl.BlockSpec((tm, d), lambda i: (i, 0)),
                  pl.BlockSpec((tm, 8), lambda i: (i, 0)),
                  pl.BlockSpec(memory_space=pl.ANY),
                  full(ws_gate), full(ws_up), full(ws_down), full(g2), full(b2)],
        out_specs=pl.BlockSpec((tm, d), lambda i: (i, 0)),
        out_shape=jax.ShapeDtypeStruct((n, d), F32),
        scratch_shapes=[pltpu.VMEM((TOP_K, tm, d), F32), pltpu.SemaphoreType.DMA(())],
        compiler_params=_cparams(("arbitrary",)),
        name="moe_combine_ln",
    )(slot, x1, w_tok, ys, ws_gate, ws_up, ws_down, g2, b2)


def _moe_layer(x1, idx, wts, pos, counts, w_gate, w_up, w_down, ws_gate, ws_up, ws_down, ln_g, ln_b,
               *, blk, tm_d, tm_c, alpha):
    n = x1.shape[0]
    cnt = counts[:, 0].astype(I32)
    padded = (cnt + blk - 1) // blk * blk
    pad_end = jnp.cumsum(padded)
    slot = _slots((pad_end - padded).astype(I32), idx, pos)
    n_blocks = -(-(n * TOP_K) // blk) + N_EXPERTS
    blk_exp = jnp.minimum(jnp.sum(pad_end[None, :] <= (jnp.arange(n_blocks) * blk)[:, None], axis=1),
                          N_EXPERTS - 1).astype(I32)
    n_used = (pad_end[-1:] // blk).astype(I32)
    xs = _dispatch(x1, slot, n_blocks * blk, tm=tm_d)
    ys = _experts(xs, blk_exp, n_used, w_gate, w_up, w_down, blk=blk)
    return _combine(x1, slot, wts.T, ys, ws_gate.astype(BF16), ws_up.astype(BF16), ws_down.astype(BF16),
                    ln_g, ln_b, tm=tm_c, alpha=alpha)


def kernel(x_prompt, x_sample, cache_cmp_kv, cache_slc_kv, cache_win_kv, state_delta_S, state_delta_conv, page_table, w_in, dn_conv_w, dn_A_log, dn_dt_bias, dn_norm_w, nsa_cmp_w1, nsa_cmp_pos, nsa_cmp_w2, w_out, ln1_g, ln1_b, w_router, b_router, w_exp_gate, w_exp_up, w_exp_down, w_sh_gate, w_sh_up, w_sh_down, ln2_g, ln2_b):
    depth = w_in.shape[0]
    assert depth == 1
    alpha = (2.0 * depth) ** 0.25
    bsz, seq, d = x_prompt.shape
    sb, st, _ = x_sample.shape
    tl = _tiles(bsz * seq, seq, sb * st)
    w_r = _reorder_w_in(w_in[0])
    wf, w2bd = _cmp_weights(nsa_cmp_w1[0], nsa_cmp_w2[0])
    mix_w = (w_r, dn_conv_w[0], dn_A_log[0], dn_dt_bias[0], dn_norm_w[0], wf, w2bd, nsa_cmp_pos[0], nsa_cmp_w1[0])
    p3, o_dn, s_p, o_nsa = _prompt_mixers(x_prompt, *mix_w, tl)
    ps, o_dn_s, s_s, o_nsa_s = _sample_mixers(x_sample, cache_cmp_kv[0], cache_slc_kv[0], cache_win_kv[0],
                                              state_delta_S[0], state_delta_conv[0], page_table, *mix_w, tl)
    wo = w_out[0].astype(BF16)
    wrt = w_router[0].T

    def ffn(x2, o_dn2, o_nsa2, p_any, tm, blk, tm_d, tm_c):
        x1, idx, wts, pos, counts = _post_mixer(x2, o_dn2, o_nsa2, p_any, wo, ln1_g[0], ln1_b[0], wrt, b_router[0],
                                                tm=tm, alpha=alpha)
        return _moe_layer(x1, idx, wts, pos, counts, w_exp_gate[0], w_exp_up[0], w_exp_down[0],
                          w_sh_gate[0], w_sh_up[0], w_sh_down[0], ln2_g[0], ln2_b[0],
                          blk=blk, tm_d=tm_d, tm_c=tm_c, alpha=alpha)

    y_p = ffn(x_prompt.reshape(-1, d), o_dn.reshape(-1, d), o_nsa.reshape(-1, d), p3,
              tl["post_tm"], tl["moe_blk"], tl["moe_tm_dispatch"], tl["moe_tm_combine"])
    y_s = ffn(x_sample.reshape(-1, d), o_dn_s[:, :st].reshape(-1, d), o_nsa_s[:, :st].reshape(-1, d), ps,
              tl["sample_tm"], tl["sample_moe_blk"], tl["sample_tm"], tl["sample_tm"])

    kv_shape = (2, NSA_KV_HEADS, NSA_HEAD_DIM)

    def kv_rows(pp, c0):
        return pp[:, :, c0:c0 + KV_COLS].reshape(pp.shape[:2] + kv_shape)

    nconv = CONV_W - 1
    conv_p = jnp.concatenate([jnp.zeros((bsz, nconv, 3 * DN_WIDTH), F32), p3[:, :, :3 * DN_WIDTH]], 1)[:, -nconv:]
    conv_s = jnp.concatenate([state_delta_conv[0], ps[:, :, :3 * DN_WIDTH]], 1)[:, -nconv:]
    past = page_table.shape[1] * PAGE_SIZE
    win_s = jnp.concatenate([cache_win_kv[0], kv_rows(ps, C_KVW)], 1)[:, -min(WINDOW, past + st):]
    return (y_p.reshape(x_prompt.shape), y_s.reshape(x_sample.shape),
            kv_rows(p3, C_KVC)[None], kv_rows(p3, C_KVS)[None], kv_rows(p3, C_KVW)[:, -min(WINDOW, seq):][None],
            s_p[None], conv_p[None],
            kv_rows(ps, C_KVC)[None], kv_rows(ps, C_KVS)[None], win_s[None], s_s[None], conv_s[None])
```

```python
import functools
import math

import jax
import jax.numpy as jnp
import numpy as np
from jax import lax
from jax.experimental import pallas as pl
from jax.experimental.pallas import tpu as pltpu

F32 = jnp.float32
BF16 = jnp.bfloat16
I32 = jnp.int32
HIGHEST = lax.Precision.HIGHEST

D_MODEL = 1024
PAGE_SIZE = 128
DN_HEADS = 8
DN_HEAD_DIM = 128
DN_WIDTH = DN_HEADS * DN_HEAD_DIM
CONV_W = 4
DN_CHUNK = 64
NSA_HEADS = 16
NSA_KV_HEADS = 2
NSA_GROUP = NSA_HEADS // NSA_KV_HEADS
NSA_HEAD_DIM = 64
NSA_WIDTH = NSA_HEADS * NSA_HEAD_DIM
NSA_KV_WIDTH = NSA_KV_HEADS * NSA_HEAD_DIM
KV_COLS = 2 * NSA_KV_WIDTH
CMP_BLOCK = 32
CMP_STRIDE = 16
SLC_BLOCK = 64
N_SELECT = 16
WINDOW = 512
N_EXPERTS = 64
TOP_K = 6
N_GROUPS = 8
TOPK_GROUPS = 4
ROUTED_SCALE = 2.5
LN_EPS = 1e-5
RMS_EPS = 1e-6
NEG = -1e30
LOG2E = math.log2(math.e)
FORCE_BONUS = 1e6

C_QKV = 0
C_Z = 3072
C_NQ = 4096
C_MG = 5120
C_KVC = 7168
C_KVS = 7424
C_KVW = 7680
C_SMALL = 7936
P_COLS = 8064
SM_A = 0
SM_B = DN_HEADS
SM_NG = 2 * DN_HEADS

LANES = 128
VMEM_LIMIT = 48 * 1024 * 1024
KEY_TILE = 256


def _tiles(n_prompt_tokens, seq, n_sample_tokens):
    return dict(
        proj_tm=min(512, n_prompt_tokens), proj_tn=P_COLS // 3,
        dn_tb=min(256, seq), dn_heads=8,
        nsa_tq=256,
        cmp_pages=32,
        post_tm=min(256, n_prompt_tokens),
        moe_blk=256, moe_tm_dispatch=min(256, n_prompt_tokens), moe_tm_combine=min(128, n_prompt_tokens),
        sample_moe_blk=64, sample_tm=n_sample_tokens,
    )


def _cparams(sem):
    return pltpu.CompilerParams(dimension_semantics=sem, vmem_limit_bytes=VMEM_LIMIT)


def _bdot(a, b):
    return jnp.dot(a.astype(BF16), b.astype(BF16), preferred_element_type=F32)


def _bdot_nt(a, b):
    return lax.dot_general(a.astype(BF16), b.astype(BF16), (((1,), (1,)), ((), ())),
                           preferred_element_type=F32)


def _bdot_tn(a, b):
    return lax.dot_general(a.astype(BF16), b.astype(BF16), (((0,), (0,)), ((), ())),
                           preferred_element_type=F32)


def _hdot(a, b):
    return jnp.dot(a, b, precision=HIGHEST, preferred_element_type=F32)


def _hdot_nt(a, b):
    return lax.dot_general(a, b, (((1,), (1,)), ((), ())), precision=HIGHEST,
                           preferred_element_type=F32)


def _sigmoid(x):
    return 1.0 / (1.0 + jnp.exp(-x))


def _silu(x):
    return x * _sigmoid(x)


def _softplus(x):
    return jnp.maximum(x, 0.0) + jnp.log(1.0 + jnp.exp(-jnp.abs(x)))


def _iota(shape, dim):
    return lax.broadcasted_iota(I32, shape, dim)


def _log2(n):
    assert n & (n - 1) == 0
    return int(math.log2(n))


def _mm_kernel(x_ref, w_ref, o_ref):
    o_ref[...] = jnp.dot(x_ref[...].astype(BF16), w_ref[...], preferred_element_type=F32)


def _matmul(x, w_bf16, tm, tn):
    m, k = x.shape
    n = w_bf16.shape[1]
    assert m % tm == 0 and n % tn == 0
    return pl.pallas_call(
        _mm_kernel,
        grid=(n // tn, m // tm),
        in_specs=[pl.BlockSpec((tm, k), lambda j, i: (i, 0)),
                  pl.BlockSpec((k, tn), lambda j, i: (0, j))],
        out_specs=pl.BlockSpec((tm, tn), lambda j, i: (i, j)),
        out_shape=jax.ShapeDtypeStruct((m, n), F32),
        compiler_params=_cparams(("parallel", "parallel")),
        name="dense_matmul",
    )(x, w_bf16)


def _reorder_w_in(w_in):
    o = 0
    seg = {}
    for name, size in (("qkv", 3 * DN_WIDTH), ("z", DN_WIDTH), ("a", DN_HEADS), ("b", DN_HEADS),
                       ("nq", NSA_WIDTH), ("kvc", KV_COLS), ("kvs", KV_COLS),
                       ("kvw", KV_COLS), ("ng", 3 * NSA_HEADS), ("mg", 2 * D_MODEL)):
        seg[name] = w_in[:, o:o + size]
        o += size
    assert o == w_in.shape[1]
    pad = jnp.zeros((w_in.shape[0], P_COLS - C_SMALL - SM_NG - 3 * NSA_HEADS), w_in.dtype)
    w = jnp.concatenate([seg["qkv"], seg["z"], seg["nq"], seg["mg"], seg["kvc"], seg["kvs"], seg["kvw"],
                         seg["a"], seg["b"], seg["ng"], pad], axis=1)
    assert w.shape[1] == P_COLS
    return w.astype(BF16)


def _tri_inverse(lmats, c):
    r = _iota((c, c), 0)
    q = _iota((c, c), 1)
    eye = (r == q).astype(F32)
    blk = min(16, c)
    shift = _log2(blk)
    same = (r >> shift) == (q >> shift)
    dmats = [jnp.where(same, lm, 0.0) for lm in lmats]
    prods = [eye - dm for dm in dmats]
    dpows = dmats
    k = 2
    while k < blk:
        dpows = [_bdot(dp, dp) for dp in dpows]
        prods = [pr + _bdot(pr, dp) for pr, dp in zip(prods, dpows)]
        k *= 2
    if c == blk:
        return prods
    mmats = [_bdot(pr, lm - dm) for pr, lm, dm in zip(prods, lmats, dmats)]
    outers = [eye - mm for mm in mmats]
    mpows = mmats
    k = 2
    while k < c // blk:
        mpows = [_bdot(mp, mp) for mp in mpows]
        outers = [ou + _bdot(ou, mp) for ou, mp in zip(outers, mpows)]
        k *= 2
    return [_bdot(ou, pr) for ou, pr in zip(outers, prods)]


def _dn_kernel(q_ref, k_ref, v_ref, z_ref, sm_ref, hq_ref, hk_ref, hv_ref, cwq_ref, cwk_ref, cwv_ref,
               hp_ref, nw_ref, s0_ref, o_ref, sout_ref,
               s_scr, xp_scr, qn_scr, kn_scr, vn_scr, gb_scr, u_scr, w_scr, qe_scr, kd_scr, a_scr, eg_scr,
               *, tb, c, t_valid, hp):
    hb = pl.program_id(1)
    t = pl.program_id(2)
    nt = pl.num_programs(2)
    dk = DN_HEAD_DIM
    nc = tb // c

    @pl.when(t == 0)
    def _():
        s_scr[...] = s0_ref[0]
        xp_scr[0, 0:8, :] = hq_ref[0]
        xp_scr[1, 0:8, :] = hk_ref[0]
        xp_scr[2, 0:8, :] = hv_ref[0]

    rows = t * tb + _iota((tb, 1), 0)
    valid = rows < t_valid

    def conv(i, raw_ref, cw_ref):
        xp_scr[i, 8:8 + tb, :] = raw_ref[0]
        acc = xp_scr[i, 8:8 + tb, :] * cw_ref[CONV_W - 1:CONV_W, :]
        for j in range(CONV_W - 1):
            acc = acc + xp_scr[i, 8 - (CONV_W - 1) + j:8 - (CONV_W - 1) + j + tb, :] * cw_ref[j:j + 1, :]
        tail = xp_scr[i, tb:tb + 8, :]
        xp_scr[i, 0:8, :] = tail
        return _silu(acc)

    qc = conv(0, q_ref, cwq_ref)
    kc = conv(1, k_ref, cwk_ref)
    vc = conv(2, v_ref, cwv_ref)
    lane = _iota((1, LANES), 1)
    sm = sm_ref[0]
    for hh in range(hp):
        h = hb * hp + hh
        cs = slice(hh * dk, (hh + 1) * dk)
        qh = qc[:, cs]
        kh = kc[:, cs]
        qn = qh * lax.rsqrt(jnp.sum(qh * qh, -1, keepdims=True) + 1e-6) * (dk ** -0.5)
        kn = kh * lax.rsqrt(jnp.sum(kh * kh, -1, keepdims=True) + 1e-6)
        a_h = jnp.sum(jnp.where(lane == SM_A + h, sm, 0.0), -1, keepdims=True)
        b_h = jnp.sum(jnp.where(lane == SM_B + h, sm, 0.0), -1, keepdims=True)
        neg_a = -jnp.exp(jnp.sum(jnp.where(lane == h, hp_ref[0:1, :], 0.0), -1, keepdims=True))
        dtb = jnp.sum(jnp.where(lane == h, hp_ref[1:2, :], 0.0), -1, keepdims=True)
        g = neg_a * _softplus(a_h + dtb)
        beta = _sigmoid(b_h)
        qn_scr[hh] = jnp.where(valid, qn, 0.0)
        kn_scr[hh] = jnp.where(valid, kn, 0.0)
        vn_scr[hh] = jnp.where(valid, vc[:, cs], 0.0)
        gb_scr[hh] = jnp.where(lane == 0, jnp.where(valid, g, 0.0), jnp.where(valid, beta, 0.0))

    r = _iota((c, c), 0)
    q = _iota((c, c), 1)
    incl = r >= q
    strict = r > q

    where = [(hh, slice(ci * c, (ci + 1) * c)) for hh in range(hp) for ci in range(nc)]
    lmats, vbs, kbes = [], [], []
    for hh, rs in where:
        qi = qn_scr[hh, rs, :]
        ki = kn_scr[hh, rs, :]
        gb = gb_scr[hh, rs, :]
        gi = gb[:, 0:1]
        bi = gb[:, 1:2]
        g_row = jnp.sum(jnp.where(r == q, gi, 0.0), 0, keepdims=True)
        gcum_col = jnp.sum(jnp.where(incl, g_row, 0.0), 1, keepdims=True)
        gcum_row = jnp.sum(jnp.where(r <= q, gi, 0.0), 0, keepdims=True)
        decay = jnp.where(incl, jnp.exp(jnp.where(incl, gcum_col - gcum_row, 0.0)), 0.0)
        kb = ki * bi
        eg = jnp.exp(gcum_col)
        g_last = gcum_col[c - 1:c, :]
        lmats.append(jnp.where(strict, _bdot_nt(kb, ki) * decay, 0.0))
        vbs.append(vn_scr[hh, rs, :] * bi)
        kbes.append(kb * eg)
        a_scr[hh, rs, :] = jnp.where(incl, _bdot_nt(qi, ki) * decay, 0.0)
        qe_scr[hh, rs, :] = qi * eg
        kd_scr[hh, rs, :] = ki * jnp.exp(g_last - gcum_col)
        e0 = rs.start // c * 8
        eg_scr[hh, e0:e0 + 8, :] = jnp.broadcast_to(jnp.exp(g_last), (8, LANES))
    tms = _tri_inverse(lmats, c)
    for (hh, rs), tm, vb, kbe in zip(where, tms, vbs, kbes):
        u_scr[hh, rs, :] = _bdot(tm, vb)
        w_scr[hh, rs, :] = _bdot(tm, kbe)

    nw = nw_ref[...]

    def chunk(ci, carry):
        r0 = pl.multiple_of(ci * c, c)
        e0 = pl.multiple_of(ci * 8, 8)
        for hh in range(hp):
            s = s_scr[hh]
            wq = jnp.concatenate([w_scr[hh, pl.ds(r0, c), :], qe_scr[hh, pl.ds(r0, c), :]], axis=0)
            ws = _bdot(wq, s)
            v_new = u_scr[hh, pl.ds(r0, c), :] - ws[0:c]
            o = ws[c:2 * c] + _bdot(a_scr[hh, pl.ds(r0, c), :], v_new)
            s_scr[hh] = s * eg_scr[hh, pl.ds(e0, 8), :][0:1, :] + _bdot_tn(kd_scr[hh, pl.ds(r0, c), :], v_new)
            o = o * lax.rsqrt(jnp.mean(o * o, -1, keepdims=True) + RMS_EPS) * nw
            o_ref[0, pl.ds(r0, c), hh * dk:(hh + 1) * dk] = o * _silu(z_ref[0, pl.ds(r0, c), hh * dk:(hh + 1) * dk])
        return carry

    lax.fori_loop(0, nc, chunk, 0)

    @pl.when(t == nt - 1)
    def _():
        sout_ref[0] = s_scr[...]


def _deltanet(p3, hist, s0, conv_w, a_log, dt_bias, norm_w, *, t_valid, tb, c, hp):
    bsz, tpad, _ = p3.shape
    assert tpad % tb == 0 and tb % c == 0 and tb % 8 == 0 and DN_HEADS % hp == 0
    nt = tpad // tb
    dk = DN_HEAD_DIM
    wid = hp * dk
    cw = jnp.concatenate([conv_w, jnp.zeros((8 - CONV_W, conv_w.shape[1]), F32)], 0)
    hpar = jnp.zeros((8, LANES), F32).at[0, :DN_HEADS].set(a_log).at[1, :DN_HEADS].set(dt_bias)
    nw = norm_w.reshape(1, dk)
    nb = DN_WIDTH // wid

    tok = lambda off: pl.BlockSpec((1, tb, wid), lambda b, h, t: (b, t, off + h))
    his = lambda off: pl.BlockSpec((1, 8, wid), lambda b, h, t: (b, 0, off + h))
    cws = lambda off: pl.BlockSpec((8, wid), lambda b, h, t: (0, off + h))
    st = pl.BlockSpec((1, hp, dk, dk), lambda b, h, t: (b, h, 0, 0))
    kern = functools.partial(_dn_kernel, tb=tb, c=c, t_valid=t_valid, hp=hp)
    big = pltpu.VMEM((hp, tb, dk), F32)
    return pl.pallas_call(
        kern,
        grid=(bsz, DN_HEADS // hp, nt),
        in_specs=[tok(0), tok(nb), tok(2 * nb), tok(C_Z // wid),
                  pl.BlockSpec((1, tb, LANES), lambda b, h, t: (b, t, C_SMALL // LANES)),
                  his(0), his(nb), his(2 * nb), cws(0), cws(nb), cws(2 * nb),
                  pl.BlockSpec((8, LANES), lambda b, h, t: (0, 0)),
                  pl.BlockSpec((1, dk), lambda b, h, t: (0, 0)),
                  st],
        out_specs=[pl.BlockSpec((1, tb, wid), lambda b, h, t: (b, t, h)), st],
        out_shape=[jax.ShapeDtypeStruct((bsz, tpad, DN_WIDTH), F32),
                   jax.ShapeDtypeStruct((bsz, DN_HEADS, dk, dk), F32)],
        scratch_shapes=[pltpu.VMEM((hp, dk, dk), F32),
                        pltpu.VMEM((3, tb + 8, wid), F32),
                        big, big, big, big, big, big, big, big,
                        pltpu.VMEM((hp, tb, c), F32),
                        pltpu.VMEM((hp, (tb // c) * 8, LANES), F32)],
        compiler_params=_cparams(("parallel", "parallel", "arbitrary")),
        name="gated_deltanet",
    )(p3, p3, p3, p3, p3, hist, hist, hist, cw, cw, cw, hpar, nw, s0)


def _cmp_weights(w1, w2):
    w1r = w1.reshape(2, CMP_BLOCK // CMP_STRIDE, CMP_STRIDE, NSA_HEAD_DIM, NSA_HEAD_DIM)
    eye = jnp.eye(2, dtype=F32)
    wf = jnp.einsum("srpde,st,hg->pshdrtge", w1r, eye, eye)
    wf = wf.reshape(CMP_STRIDE * KV_COLS, 2 * KV_COLS)
    w2bd = jnp.einsum("sef,st,hg->shetgf", w2, eye, eye).reshape(KV_COLS, KV_COLS)
    return wf.astype(BF16), w2bd.astype(BF16)


def _cmp_epi_kernel(p_ref, pos_ref, w1_ref, w2_ref, o_ref):
    pm = p_ref[0]
    n = pm.shape[0]
    nxt = pltpu.roll(pm[:, KV_COLS:2 * KV_COLS], n - 1, 0)
    b_k = _hdot(pos_ref[0:1, :], w1_ref[0])
    b_v = _hdot(pos_ref[1:2, :], w1_ref[1])
    bias = jnp.concatenate([b_k, b_k, b_v, b_v], axis=-1)
    h = pm[:, 0:KV_COLS] + nxt + bias
    o_ref[0] = _bdot(jax.nn.gelu(h), w2_ref[...])


def _cmp_epilogue(pmat, pos, w1, w2bd):
    bsz, n_sub, wid = pmat.shape
    return pl.pallas_call(
        _cmp_epi_kernel,
        grid=(bsz,),
        in_specs=[pl.BlockSpec((1, n_sub, wid), lambda b: (b, 0, 0)),
                  pl.BlockSpec(pos.shape, lambda b: (0, 0)),
                  pl.BlockSpec(w1.shape, lambda b: (0, 0, 0)),
                  pl.BlockSpec(w2bd.shape, lambda b: (0, 0))],
        out_specs=pl.BlockSpec((1, n_sub, wid // 2), lambda b: (b, 0, 0)),
        out_shape=jax.ShapeDtypeStruct((bsz, n_sub, wid // 2), F32),
        compiler_params=_cparams(("parallel",)),
        name="nsa_compress_epilogue",
    )(pmat, pos, w1, w2bd)


def _cmp_paged_kernel(pt_ref, cache_ref, w_ref, o_ref, buf, rows_k, rows_v, sem, *, npg):
    i = pl.program_id(0)
    n = pl.num_programs(0)
    spp = PAGE_SIZE // CMP_STRIDE

    def page_copy(page, slot, j):
        return pltpu.make_async_copy(cache_ref.at[page], buf.at[slot, j], sem.at[slot])

    def issue(step, slot):
        for j in range(npg):
            page_copy(pt_ref[step * npg + j], slot, j).start()

    @pl.when(i == 0)
    def _():
        issue(0, 0)

    @pl.when(i + 1 < n)
    def _():
        issue(i + 1, (i + 1) % 2)

    slot = i % 2
    for j in range(npg):
        page_copy(0, slot, j).wait()
    rows = (rows_k, rows_v)
    for hf in range(2):
        for j in range(npg):
            rows[hf][j * PAGE_SIZE:(j + 1) * PAGE_SIZE, :] = buf[slot, j, hf * LANES:(hf + 1) * LANES, :].T
    for hf in range(2):
        acc = jnp.zeros((o_ref.shape[0], 2 * LANES), F32)
        for p in range(CMP_STRIDE):
            xs = rows[hf][pl.ds(p, npg * spp, stride=CMP_STRIDE), :]
            acc = acc + jnp.dot(xs.astype(BF16), w_ref[hf, p], preferred_element_type=F32)
        for rslot in range(CMP_BLOCK // CMP_STRIDE):
            c0 = rslot * KV_COLS + hf * LANES
            o_ref[:, c0:c0 + LANES] = acc[:, rslot * LANES:(rslot + 1) * LANES]


def _cmp_paged(cache_t, page_table, wf, *, npg):
    n_pool, cols, psz = cache_t.shape
    bsz, n_pages = page_table.shape
    total = bsz * n_pages
    spp = psz // CMP_STRIDE
    assert total % npg == 0 and cols == KV_COLS == 2 * LANES and psz == PAGE_SIZE
    kern = functools.partial(_cmp_paged_kernel, npg=npg)
    w5 = wf.reshape(CMP_STRIDE, 2, LANES, CMP_BLOCK // CMP_STRIDE, 2, LANES)
    wfs = jnp.stack([w5[:, s, :, :, s, :] for s in range(2)]).reshape(2, CMP_STRIDE, LANES, 2 * LANES)
    return pl.pallas_call(
        kern,
        grid_spec=pltpu.PrefetchScalarGridSpec(
            num_scalar_prefetch=1,
            grid=(total // npg,),
            in_specs=[pl.BlockSpec(memory_space=pl.ANY),
                      pl.BlockSpec(wfs.shape, lambda i, pt: (0, 0, 0, 0))],
            out_specs=pl.BlockSpec((npg * spp, wf.shape[1]), lambda i, pt: (i, 0)),
            scratch_shapes=[pltpu.VMEM((2, npg, cols, psz), F32), pltpu.VMEM((npg * psz, LANES), F32),
                            pltpu.VMEM((npg * psz, LANES), F32), pltpu.SemaphoreType.DMA((2,))]),
        out_shape=jax.ShapeDtypeStruct((total * spp, wf.shape[1]), F32),
        compiler_params=_cparams(("arbitrary",)),
        name="nsa_compress_paged",
    )(page_table.reshape(-1), cache_t, wfs)


def _slope(head):
    return 2.0 ** (-8.0 * (head + 1) / NSA_HEADS)


def _gather_heads(q_ref, hk):
    g = NSA_GROUP
    dh = NSA_HEAD_DIM
    qs = jnp.concatenate([q_ref[0, :, (hk * g + i) * dh:(hk * g + i + 1) * dh] for i in range(g)], axis=0)
    return qs * (dh ** -0.5)


def _cmp_branch(qs, kc, vc, hk, valid_c, dist_c, tq):
    s_all = _bdot_nt(qs, kc)
    ps = []
    psum = None
    for i in range(NSA_GROUP):
        s = s_all[i * tq:(i + 1) * tq] - _slope(hk * NSA_GROUP + i) * dist_c
        s = jnp.where(valid_c, s, NEG)
        m = jnp.max(s, -1, keepdims=True)
        p = jnp.where(valid_c, jnp.exp(s - m), 0.0)
        p = p / jnp.maximum(jnp.sum(p, -1, keepdims=True), 1e-30)
        ps.append(p)
        psum = p if psum is None else psum + p
    return _bdot(jnp.concatenate(ps, axis=0), vc), psum


def _slope_features(tq):
    out = np.zeros((NSA_KV_HEADS, NSA_GROUP * tq, NSA_HEAD_DIM), np.float32)
    for hk in range(NSA_KV_HEADS):
        for g in range(NSA_GROUP):
            rem = _slope(hk * NSA_GROUP + g) * LOG2E
            for i in range(3):
                piece = float(np.float32(rem).astype(jnp.bfloat16))
                out[hk, g * tq:(g + 1) * tq, 2 * i:2 * i + 2] = piece
                rem -= piece
    return jnp.asarray(out)


def _position_features(pos):
    lo = (pos % 256).astype(F32)
    hi = (pos - pos % 256).astype(F32)
    cols = jnp.stack([hi, lo, hi, lo, hi, lo], axis=1)
    return jnp.pad(cols, ((0, 0), (0, NSA_HEAD_DIM - 6)))


_NT = (((1,), (1,)), ((), ()))


def _cmp_branch_aug(q_aug, kc_aug, vc, valid_c, tq):
    raw = lax.dot_general(q_aug, kc_aug, _NT, preferred_element_type=F32)
    bias = jnp.where(valid_c, 0.0, NEG)
    ps = []
    psum = None
    for i in range(NSA_GROUP):
        s = raw[i * tq:(i + 1) * tq] + bias
        p = jnp.where(valid_c, jnp.exp2(s - jnp.max(s, -1, keepdims=True)), 0.0)
        p = p * (1.0 / jnp.maximum(jnp.sum(p, -1, keepdims=True), 1e-30))
        ps.append(p)
        psum = p if psum is None else psum + p
    return _bdot(jnp.concatenate(ps, axis=0), vc), psum


def _flash_branch(q_aug, kf, kv_ref, hk, t_lo, t_hi, bias_fn, m_scr, acc_scr, tq):
    g = NSA_GROUP
    dh = NSA_HEAD_DIM
    m_scr[...] = jnp.full(m_scr.shape, NEG, F32)
    acc_scr[...] = jnp.zeros(acc_scr.shape, F32)
    ones = jnp.ones((KEY_TILE, dh), F32)

    def body(i, carry):
        t = t_hi - 1 - i
        k0 = pl.multiple_of(t * KEY_TILE, KEY_TILE)
        k_aug = jnp.concatenate([kv_ref[0, pl.ds(k0, KEY_TILE), hk * dh:(hk + 1) * dh], kf], axis=1).astype(BF16)
        v = kv_ref[0, pl.ds(k0, KEY_TILE), NSA_KV_WIDTH + hk * dh:NSA_KV_WIDTH + (hk + 1) * dh]
        vaug = jnp.concatenate([v, ones], axis=1).astype(BF16)
        bias = bias_fn(k0)
        k0f = k0.astype(F32)
        half = g // 2
        raws = [lax.dot_general(q_aug[h * half * tq:(h + 1) * half * tq], k_aug, _NT, preferred_element_type=F32)
                for h in range(2)]
        for h in range(2):
            ps = []
            alphas = []
            for jj in range(half):
                j = h * half + jj
                rs = slice(j * tq, (j + 1) * tq)
                shift = k0f * (_slope(hk * g + j) * LOG2E)
                s = raws[h][jj * tq:(jj + 1) * tq] + bias
                m_old = m_scr[rs, :]
                m_new = jnp.maximum(m_old, jnp.max(s, -1, keepdims=True) + shift)
                alphas.append(jnp.exp2(m_old - m_new))
                ps.append(jnp.exp2(s - jnp.concatenate([m_new - shift] * (KEY_TILE // LANES), axis=1)).astype(BF16))
                m_scr[rs, :] = m_new
            hs = slice(h * half * tq, (h + 1) * half * tq)
            pv = jnp.dot(jnp.concatenate(ps, axis=0), vaug, preferred_element_type=F32)
            acc_scr[hs, :] = jnp.concatenate(alphas, axis=0) * acc_scr[hs, :] + pv
        return carry

    lax.fori_loop(0, t_hi - t_lo, body, 0)
    acc = acc_scr[...]
    return acc[:, 0:dh] / jnp.maximum(acc[:, dh:2 * dh], 1e-30)


def _gate_combine(sm, hk, o_c, o_s, o_w, o_ref, tq):
    g = NSA_GROUP
    dh = NSA_HEAD_DIM
    outs = []
    for i in range(g):
        c0 = SM_NG + (hk * g + i) * 3
        gt = _sigmoid(sm[:, c0:c0 + 3])
        rows = slice(i * tq, (i + 1) * tq)
        outs.append(gt[:, 0:1] * o_c[rows] + gt[:, 1:2] * o_s[rows] + gt[:, 2:3] * o_w[rows])
    for i in range(0, g, 2):
        c0 = (hk * g + i) * dh
        o_ref[0, :, c0:c0 + 2 * dh] = jnp.concatenate([outs[i], outs[i + 1]], axis=-1)


def _nsa_prompt_kernel(q_ref, sm_ref, kvs_ref, kvw_ref, kc_ref, qsl_ref, kf_ref, kfc_ref, o_ref, m_scr, acc_scr,
                       *, tq, seq, n_cmp):
    dh = NSA_HEAD_DIM
    q0 = pl.program_id(1) * tq
    n_sub = kc_ref.shape[1]
    n_slc = seq // SLC_BLOCK
    qpos_i = q0 + _iota((tq, 1), 0)
    sm = sm_ref[0]
    t_hi = (q0 + tq + KEY_TILE - 1) // KEY_TILE
    t_lo_w = jnp.maximum(q0 - (WINDOW - 1), 0) // KEY_TILE

    cidx = _iota((1, n_sub), 1)
    valid_c = (cidx * CMP_STRIDE + (CMP_BLOCK - 1) <= qpos_i) & (cidx < n_cmp)
    cr = _iota((n_slc, n_sub), 1) * CMP_STRIDE
    s_st = _iota((n_slc, n_sub), 0) * SLC_BLOCK
    cover_t = jnp.where((cr < s_st + SLC_BLOCK) & (cr + (CMP_BLOCK - 1) >= s_st), 1.0, 0.0)
    srow = _iota((n_slc, 1), 0)
    qrow = q0 + _iota((1, tq), 1)
    cur = qrow >> _log2(SLC_BLOCK)
    forced = (srow == 0) | (srow == cur) | (srow == cur - 1)
    bonus = jnp.where(forced, FORCE_BONUS, 0.0)
    past_ok = srow * SLC_BLOCK <= qrow
    kf = kf_ref[...]

    q_augs, o_cs, sels = [], [], []
    for hk in range(NSA_KV_HEADS):
        q_aug = jnp.concatenate([_gather_heads(q_ref, hk) * LOG2E, qsl_ref[hk]], axis=1).astype(BF16)
        kc_aug = jnp.concatenate([kc_ref[0, :, hk * dh:(hk + 1) * dh], kfc_ref[...]], axis=1).astype(BF16)
        vc = kc_ref[0, :, NSA_KV_WIDTH + hk * dh:NSA_KV_WIDTH + (hk + 1) * dh]
        o_c, psum = _cmp_branch_aug(q_aug, kc_aug, vc, valid_c, tq)
        q_augs.append(q_aug)
        o_cs.append(o_c)
        score_t = jnp.where(past_ok, _hdot_nt(cover_t, psum) + bonus, NEG)
        sels.append(jnp.where(_rank_rows(score_t, n_slc) < N_SELECT, 1.0, 0.0).astype(BF16))

    for hk in range(NSA_KV_HEADS):
        def slc_bias(k0, sel_t=sels[hk]):
            kblk = (k0 + _iota((n_slc, KEY_TILE), 1)) >> _log2(SLC_BLOCK)
            expand = jnp.where(_iota((n_slc, KEY_TILE), 0) == kblk, 1.0, 0.0).astype(BF16)
            picked = lax.dot_general(sel_t, expand, (((0,), (0,)), ((), ())), preferred_element_type=F32)
            dist = qpos_i - (k0 + _iota((1, KEY_TILE), 1))
            return jnp.where((picked > 0.5) & (dist >= 0), 0.0, NEG)

        def win_bias(k0):
            dist = qpos_i - (k0 + _iota((1, KEY_TILE), 1))
            return jnp.where((dist >= 0) & (dist < WINDOW), 0.0, NEG)

        o_s = _flash_branch(q_augs[hk], kf, kvs_ref, hk, 0, t_hi, slc_bias, m_scr, acc_scr, tq)
        o_w = _flash_branch(q_augs[hk], kf, kvw_ref, hk, t_lo_w, t_hi, win_bias, m_scr, acc_scr, tq)
        _gate_combine(sm, hk, o_cs[hk], o_s, o_w, o_ref, tq)


def _nsa_prompt(p3, kcvc, *, tq):
    bsz, seq, _ = p3.shape
    n_sub = kcvc.shape[1]
    assert seq % KEY_TILE == 0 and seq % tq == 0 and seq % SLC_BLOCK == 0 and KEY_TILE % tq == 0
    assert seq + CMP_BLOCK < 256 * 256, "positions are split into two bf16-exact parts"
    kern = functools.partial(_nsa_prompt_kernel, tq=tq, seq=seq, n_cmp=seq // CMP_STRIDE - 1)
    rows = NSA_GROUP * tq
    qsl = _slope_features(tq)
    kf = _position_features(jnp.arange(KEY_TILE))
    kfc = _position_features(jnp.arange(n_sub) * CMP_STRIDE + (CMP_BLOCK - 1))
    full = lambda a: pl.BlockSpec(a.shape, lambda b, j: (0,) * a.ndim)
    return pl.pallas_call(
        kern,
        grid=(bsz, seq // tq),
        in_specs=[pl.BlockSpec((1, tq, NSA_WIDTH), lambda b, j: (b, j, C_NQ // NSA_WIDTH)),
                  pl.BlockSpec((1, tq, LANES), lambda b, j: (b, j, C_SMALL // LANES)),
                  pl.BlockSpec((1, seq, KV_COLS), lambda b, j: (b, 0, C_KVS // KV_COLS)),
                  pl.BlockSpec((1, seq, KV_COLS), lambda b, j: (b, 0, C_KVW // KV_COLS)),
                  pl.BlockSpec((1, n_sub, KV_COLS), lambda b, j: (b, 0, 0)),
                  full(qsl), full(kf), full(kfc)],
        out_specs=pl.BlockSpec((1, tq, NSA_WIDTH), lambda b, j: (b, j, 0)),
        out_shape=jax.ShapeDtypeStruct((bsz, seq, NSA_WIDTH), F32),
        scratch_shapes=[pltpu.VMEM((rows, LANES), F32), pltpu.VMEM((rows, 2 * NSA_HEAD_DIM), F32)],
        compiler_params=_cparams(("parallel", "arbitrary")),
        name="nsa_prompt_attention",
    )(p3, p3, p3, p3, kcvc, qsl, kf, kfc)


def _nsa_select_kernel(q_ref, kc_ref, oc_ref, sel_ref, *, tq, past, n_cmp, n_slc, n_slc_pad):
    dh = NSA_HEAD_DIM
    n_sub = kc_ref.shape[1]
    qpos_i = past + _iota((tq, 1), 0)
    qpos = qpos_i.astype(F32)
    cidx = _iota((1, n_sub), 1)
    c_end = cidx * CMP_STRIDE + (CMP_BLOCK - 1)
    valid_c = (c_end <= qpos_i) & (cidx < n_cmp)
    dist_c = qpos - c_end.astype(F32)
    cr = _iota((n_sub, n_slc_pad), 0) * CMP_STRIDE
    s_st = _iota((n_sub, n_slc_pad), 1) * SLC_BLOCK
    cover = jnp.where((cr < s_st + SLC_BLOCK) & (cr + (CMP_BLOCK - 1) >= s_st), 1.0, 0.0)
    sidx = _iota((1, n_slc_pad), 1)
    sidx_f = sidx.astype(F32)
    cur = qpos_i >> _log2(SLC_BLOCK)
    forced = (sidx == 0) | (sidx == cur) | (sidx == cur - 1)
    bonus = jnp.where(forced, FORCE_BONUS, 0.0)
    past_ok = sidx * SLC_BLOCK <= qpos_i
    lane = _iota((1, LANES), 1)
    for hk in range(NSA_KV_HEADS):
        qs = _gather_heads(q_ref, hk)
        kc = kc_ref[0, :, hk * dh:(hk + 1) * dh]
        vc = kc_ref[0, :, NSA_KV_WIDTH + hk * dh:NSA_KV_WIDTH + (hk + 1) * dh]
        o_c, psum = _cmp_branch(qs, kc, vc, hk, valid_c, dist_c, tq)
        oc_ref[0, hk] = o_c
        imp = _hdot(psum, cover)
        score = jnp.where(past_ok, imp + bonus, NEG)
        score = jnp.where(sidx < n_slc, score, -jnp.inf)
        res = jnp.zeros((tq, LANES), F32)
        for it in range(min(N_SELECT, n_slc)):
            m = jnp.max(score, -1, keepdims=True)
            idx = jnp.min(jnp.where(score == m, sidx_f, 1e9), -1, keepdims=True)
            res = jnp.where(lane == it, idx, res)
            score = jnp.where(sidx_f == idx, -jnp.inf, score)
        sel_ref[0, hk] = res.astype(I32)


def _nsa_select(ps3, kcvc, *, past, n_cmp, n_slc):
    bsz, tq, _ = ps3.shape
    n_sub = kcvc.shape[1]
    n_slc_pad = -(-n_slc // LANES) * LANES
    kern = functools.partial(_nsa_select_kernel, tq=tq, past=past, n_cmp=n_cmp, n_slc=n_slc, n_slc_pad=n_slc_pad)
    rows = NSA_GROUP * tq
    return pl.pallas_call(
        kern,
        grid=(bsz,),
        in_specs=[pl.BlockSpec((1, tq, NSA_WIDTH), lambda b: (b, 0, C_NQ // NSA_WIDTH)),
                  pl.BlockSpec((1, n_sub, KV_COLS), lambda b: (b, 0, 0))],
        out_specs=[pl.BlockSpec((1, NSA_KV_HEADS, rows, NSA_HEAD_DIM), lambda b: (b, 0, 0, 0)),
                   pl.BlockSpec((1, NSA_KV_HEADS, tq, LANES), lambda b: (b, 0, 0, 0))],
        out_shape=[jax.ShapeDtypeStruct((bsz, NSA_KV_HEADS, rows, NSA_HEAD_DIM), F32),
                   jax.ShapeDtypeStruct((bsz, NSA_KV_HEADS, tq, LANES), I32)],
        compiler_params=_cparams(("parallel",)),
        name="nsa_sample_select",
    )(ps3, kcvc)


def _joint_softmax_pv(parts, hk, tq):
    g = NSA_GROUP
    outs = []
    for j in range(g):
        rs = slice(j * tq, (j + 1) * tq)
        slope = _slope(hk * g + j)
        ss = [jnp.where(valid, s_all[rs] - slope * dist, NEG) for s_all, valid, dist, _, _ in parts]
        m = None
        for s in ss:
            mi = jnp.max(s, -1, keepdims=True)
            m = mi if m is None else jnp.maximum(m, mi)
        num = None
        den = None
        for s, (_, valid, _, v, v_t) in zip(ss, parts):
            p = jnp.where(valid, jnp.exp(s - m), 0.0)
            d = jnp.sum(p, -1, keepdims=True)
            o = _bdot_nt(p, v) if v_t else _bdot(p, v)
            num = o if num is None else num + o
            den = d if den is None else den + d
        outs.append(num / jnp.maximum(den, 1e-30))
    return jnp.concatenate(outs, axis=0)


def _nsa_sample_kernel(phys_ref, q_ref, sm_ref, kpos_ref, tail_ref, wcache_ref, wnew_ref, oc_ref, cache_ref,
                       o_ref, kbuf, vbuf, sem, *, tq, t_valid, past, n_gather):
    dh = NSA_HEAD_DIM
    b = pl.program_id(0)
    per_b = NSA_KV_HEADS * n_gather

    def page_copies(page, hk, i):
        dst = pl.ds(i * PAGE_SIZE, PAGE_SIZE)
        return (pltpu.make_async_copy(cache_ref.at[page, pl.ds(hk * dh, dh), :], kbuf.at[hk, :, dst], sem),
                pltpu.make_async_copy(cache_ref.at[page, pl.ds(NSA_KV_WIDTH + hk * dh, dh), :], vbuf.at[hk, :, dst], sem))

    for hk in range(NSA_KV_HEADS):
        for i in range(n_gather):
            for cp in page_copies(phys_ref[b * per_b + hk * n_gather + i], hk, i):
                cp.start()

    qpos_i = past + _iota((tq, 1), 0)
    qpos = qpos_i.astype(F32)
    sm = sm_ref[0]
    n_keys = n_gather * PAGE_SIZE
    per_q = n_keys // t_valid
    new_ok = _iota((1, tq), 1) < t_valid
    dist_new = qpos - (past + _iota((1, tq), 1)).astype(F32)
    n_win = wcache_ref.shape[2]
    dist_wc = qpos - (past - n_win + _iota((1, n_win), 1)).astype(F32)
    ok_wc = (dist_wc >= 0.0) & (dist_wc < float(WINDOW))
    ok_wn = (dist_new >= 0.0) & (dist_new < float(WINDOW)) & new_ok
    ok_t = (dist_new >= 0.0) & new_ok
    own = (_iota((tq, n_keys), 1) >> _log2(per_q)) == _iota((tq, n_keys), 0)

    qss = [_gather_heads(q_ref, hk) for hk in range(NSA_KV_HEADS)]
    win = []
    for hk in range(NSA_KV_HEADS):
        kw_t = wcache_ref[0, hk * dh:(hk + 1) * dh, :]
        vw_t = wcache_ref[0, NSA_KV_WIDTH + hk * dh:NSA_KV_WIDTH + (hk + 1) * dh, :]
        kn = wnew_ref[0, :, hk * dh:(hk + 1) * dh]
        vn = wnew_ref[0, :, NSA_KV_WIDTH + hk * dh:NSA_KV_WIDTH + (hk + 1) * dh]
        win.append(_joint_softmax_pv([(_bdot(qss[hk], kw_t), ok_wc, dist_wc, vw_t, True),
                                      (_bdot_nt(qss[hk], kn), ok_wn, dist_new, vn, False)], hk, tq))

    for hk in range(NSA_KV_HEADS):
        for i in range(n_gather):
            for cp in page_copies(0, hk, i):
                cp.wait()

    for hk in range(NSA_KV_HEADS):
        kt = tail_ref[0, :, hk * dh:(hk + 1) * dh]
        vt = tail_ref[0, :, NSA_KV_WIDTH + hk * dh:NSA_KV_WIDTH + (hk + 1) * dh]
        dist_p = qpos - kpos_ref[0, hk]
        ok_p = own & (dist_p >= 0.0)
        o_s = _joint_softmax_pv([(_bdot(qss[hk], kbuf[hk]), ok_p, dist_p, vbuf[hk], True),
                                 (_bdot_nt(qss[hk], kt), ok_t, dist_new, vt, False)], hk, tq)
        _gate_combine(sm, hk, oc_ref[0, hk], o_s, win[hk], o_ref, tq)


def _nsa_sample(ps3, o_c, phys, kpos, cache_t, win_t, *, t_valid, past):
    bsz, tq, _ = ps3.shape
    n_gather = t_valid * N_SELECT
    rows = NSA_GROUP * tq
    n_keys = n_gather * PAGE_SIZE
    kern = functools.partial(_nsa_sample_kernel, tq=tq, t_valid=t_valid, past=past, n_gather=n_gather)
    return pl.pallas_call(
        kern,
        grid_spec=pltpu.PrefetchScalarGridSpec(
            num_scalar_prefetch=1,
            grid=(bsz,),
            in_specs=[pl.BlockSpec((1, tq, NSA_WIDTH), lambda b, ph: (b, 0, C_NQ // NSA_WIDTH)),
                      pl.BlockSpec((1, tq, LANES), lambda b, ph: (b, 0, C_SMALL // LANES)),
                      pl.BlockSpec((1, NSA_KV_HEADS, 1, n_keys), lambda b, ph: (b, 0, 0, 0)),
                      pl.BlockSpec((1, tq, KV_COLS), lambda b, ph: (b, 0, C_KVS // KV_COLS)),
                      pl.BlockSpec((1,) + win_t.shape[1:], lambda b, ph: (b, 0, 0)),
                      pl.BlockSpec((1, tq, KV_COLS), lambda b, ph: (b, 0, C_KVW // KV_COLS)),
                      pl.BlockSpec((1, NSA_KV_HEADS, rows, NSA_HEAD_DIM), lambda b, ph: (b, 0, 0, 0)),
                      pl.BlockSpec(memory_space=pl.ANY)],
            out_specs=pl.BlockSpec((1, tq, NSA_WIDTH), lambda b, ph: (b, 0, 0)),
            scratch_shapes=[pltpu.VMEM((NSA_KV_HEADS, NSA_HEAD_DIM, n_keys), F32),
                            pltpu.VMEM((NSA_KV_HEADS, NSA_HEAD_DIM, n_keys), F32),
                            pltpu.SemaphoreType.DMA(())]),
        out_shape=jax.ShapeDtypeStruct((bsz, tq, NSA_WIDTH), F32),
        compiler_params=_cparams(("arbitrary",)),
        name="nsa_sample_attention",
    )(phys, ps3, ps3, kpos, ps3, win_t, ps3, o_c, cache_t)


def _rows_transposed(cache):
    nd = cache.ndim
    perm = tuple(range(nd - 4)) + (nd - 3, nd - 2, nd - 1, nd - 4)
    t = jnp.transpose(cache, perm)
    return t.reshape(t.shape[:nd - 4] + (KV_COLS, cache.shape[nd - 4]))


def _prompt_mixers(x, w_r, conv_w, a_log, dt_bias, norm_w, cmp_wf, cmp_w2bd, cmp_pos, cmp_w1, tl):
    bsz, seq, _ = x.shape
    p = _matmul(x.reshape(bsz * seq, D_MODEL), w_r, tl["proj_tm"], tl["proj_tn"])
    p3 = p.reshape(bsz, seq, P_COLS)
    hist = jnp.zeros((bsz, 8, 3 * DN_WIDTH), F32)
    s0 = jnp.zeros((bsz, DN_HEADS, DN_HEAD_DIM, DN_HEAD_DIM), F32)
    o_dn, s_new = _deltanet(p3, hist, s0, conv_w, a_log, dt_bias, norm_w, t_valid=seq, tb=tl["dn_tb"], c=DN_CHUNK,
                            hp=tl["dn_heads"])
    kvc = p3[:, :, C_KVC:C_KVC + KV_COLS]
    n_sub = seq // CMP_STRIDE
    sub = kvc.reshape(bsz * n_sub, CMP_STRIDE * KV_COLS)
    pmat = _matmul(sub, cmp_wf, min(256, bsz * n_sub), cmp_wf.shape[1]).reshape(bsz, n_sub, -1)
    kcvc = _cmp_epilogue(pmat, cmp_pos.reshape(2, -1), cmp_w1, cmp_w2bd)
    o_nsa = _nsa_prompt(p3, kcvc, tq=tl["nsa_tq"])
    return p3, o_dn, s_new, o_nsa


def _sample_mixers(x, cache_cmp, cache_slc, win_buf, s0, conv_buf, page_table, w_r, conv_w, a_log, dt_bias,
                   norm_w, cmp_wf, cmp_w2bd, cmp_pos, cmp_w1, tl):
    bsz, t, _ = x.shape
    tq = 8
    n_pages = page_table.shape[1]
    past = n_pages * PAGE_SIZE
    assert t <= tq and t <= SLC_BLOCK and past % SLC_BLOCK == 0 and cache_cmp.shape[1] == PAGE_SIZE
    assert (past + t) // CMP_STRIDE * CMP_STRIDE == past, "new rows never complete a compression sub-block"
    ps = _matmul(x.reshape(bsz * t, D_MODEL), w_r, bsz * t, tl["proj_tn"]).reshape(bsz, t, P_COLS)
    ps3 = jnp.pad(ps, ((0, 0), (0, tq - t), (0, 0)))
    hist = jnp.pad(conv_buf, ((0, 0), (8 - (CONV_W - 1), 0), (0, 0)))
    o_dn, s_new = _deltanet(ps3, hist, s0, conv_w, a_log, dt_bias, norm_w, t_valid=t, tb=tq, c=tq, hp=DN_HEADS)
    n_sub = past // CMP_STRIDE
    pmat = _cmp_paged(_rows_transposed(cache_cmp), page_table, cmp_wf, npg=tl["cmp_pages"])
    kcvc = _cmp_epilogue(pmat.reshape(bsz, n_sub, -1), cmp_pos.reshape(2, -1), cmp_w1, cmp_w2bd)
    n_past_blocks = past // SLC_BLOCK
    o_c, sel = _nsa_select(ps3, kcvc, past=past, n_cmp=n_sub - 1, n_slc=n_past_blocks + 1)
    sel = sel[:, :, :t, :N_SELECT]
    bpp = PAGE_SIZE // SLC_BLOCK
    jp = jnp.minimum(sel, n_past_blocks - 1)
    page = jp // bpp
    phys = page_table[jnp.arange(bsz)[:, None, None, None], page]
    row = jnp.arange(PAGE_SIZE)
    in_blk = (row // SLC_BLOCK == (jp % bpp)[..., None]) & (sel < n_past_blocks)[..., None]
    kpos = jnp.where(in_blk, (page[..., None] * PAGE_SIZE + row).astype(F32), 1e9)
    kpos = kpos.reshape(bsz, NSA_KV_HEADS, 1, t * N_SELECT * PAGE_SIZE)
    o_nsa = _nsa_sample(ps3, o_c, phys.reshape(-1).astype(I32), kpos, _rows_transposed(cache_slc),
                        _rows_transposed(win_buf), t_valid=t, past=past)
    return ps, o_dn, s_new, o_nsa


def _layer_norm(x, g, b):
    xc = x - jnp.mean(x, -1, keepdims=True)
    var = jnp.mean(xc * xc, -1, keepdims=True)
    return xc * lax.rsqrt(var + LN_EPS) * g + b


def _rank_rows(v, n):
    ri = _iota(v.shape, 0)
    rank = jnp.zeros(v.shape, F32)
    for rp in range(n):
        row = v[rp:rp + 1, :]
        beats = (row > v) | ((row == v) & (rp < ri))
        rank = rank + jnp.where(beats, 1.0, 0.0)
    return rank


def _post_mixer_kernel(x_ref, odn_ref, onsa_ref, gdn_ref, gnsa_ref, wo_ref, g_ref, b_ref, wr_ref, br_ref,
                       x1_ref, xp_ref, idx_ref, wt_ref, pos_ref, cnt_ref, run_scr, *, tm, alpha):
    i = pl.program_id(0)

    @pl.when(i == 0)
    def _():
        run_scr[...] = jnp.zeros(run_scr.shape, F32)

    h = _sigmoid(gdn_ref[0]) * odn_ref[...] + _sigmoid(gnsa_ref[0]) * onsa_ref[...]
    x1 = _layer_norm(alpha * x_ref[...] + _bdot(h, wo_ref[...]), g_ref[...], b_ref[...])
    x1_ref[...] = x1
    bits = pltpu.bitcast(x1.astype(BF16).astype(F32), jnp.uint32)
    half = x1.shape[1] // 2
    xp_ref[...] = (bits[:, :half] >> 16) | (bits[:, half:] & jnp.uint32(0xFFFF0000))

    ne = N_EXPERTS
    per = ne // N_GROUPS
    scores = _sigmoid(_hdot_nt(wr_ref[...], x1))
    s3 = (scores + br_ref[...]).reshape(N_GROUPS, per, tm)
    e3 = _iota((N_GROUPS, per, tm), 1).astype(F32)
    g1 = jnp.max(s3, axis=1, keepdims=True)
    first = jnp.min(jnp.where(s3 == g1, e3, float(per)), axis=1, keepdims=True)
    g2 = jnp.max(jnp.where(e3 == first, -jnp.inf, s3), axis=1, keepdims=True)
    grank = _rank_rows((g1 + g2).reshape(N_GROUPS, tm), N_GROUPS)
    keep = (grank < TOPK_GROUPS).reshape(N_GROUPS, 1, tm)
    selm = jnp.where(keep, s3, NEG).reshape(ne, tm)
    erank = _rank_rows(selm, ne)
    ei = _iota((ne, tm), 0).astype(F32)
    chosen = jnp.where(erank < TOP_K, 1.0, 0.0)
    tr = _iota((tm, tm), 0)
    tc = _iota((tm, tm), 1)
    before = jnp.where(tr < tc, 1.0, 0.0)
    pos_full = _bdot(chosen, before) + run_scr[:, 0:1]
    idx_rows, w_rows, pos_rows = [], [], []
    for k in range(TOP_K):
        hit = erank == float(k)
        idx_rows.append(jnp.sum(jnp.where(hit, ei, 0.0), 0, keepdims=True))
        w_rows.append(jnp.sum(jnp.where(hit, scores, 0.0), 0, keepdims=True))
        pos_rows.append(jnp.sum(jnp.where(hit, pos_full, 0.0), 0, keepdims=True))
    wsum = w_rows[0]
    for k in range(1, TOP_K):
        wsum = wsum + w_rows[k]
    zero = jnp.zeros((8 - TOP_K, tm), F32)
    idx_ref[...] = jnp.concatenate(idx_rows + [zero], 0).astype(I32)
    wt_ref[...] = jnp.concatenate([w / wsum * ROUTED_SCALE for w in w_rows] + [zero], 0)
    pos_ref[...] = jnp.concatenate(pos_rows + [zero], 0).astype(I32)
    run_scr[...] = run_scr[...] + jnp.sum(chosen, 1, keepdims=True)
    cnt_ref[...] = run_scr[...]


def _post_mixer(x, o_dn, o_nsa, p3, w_out_bf16, ln_g, ln_b, w_router_t, b_router, *, tm, alpha):
    n, d = x.shape
    assert n % tm == 0
    bsz, seq, _ = p3.shape
    assert seq % tm == 0 or tm % seq == 0
    if seq % tm == 0:
        per_b = seq // tm
        gspec = lambda c: pl.BlockSpec((1, tm, d), lambda i: (i // per_b, i % per_b, c))
        p_in = p3
    else:
        p_in = p3.reshape(1, n, P_COLS)
        gspec = lambda c: pl.BlockSpec((1, tm, d), lambda i: (0, i, c))
    tok = pl.BlockSpec((tm, d), lambda i: (i, 0))
    full = lambda a: pl.BlockSpec(a.shape, lambda i: (0,) * a.ndim)
    rt = pl.BlockSpec((8, tm), lambda i: (0, i))
    kern = functools.partial(_post_mixer_kernel, tm=tm, alpha=alpha)
    g2 = ln_g.reshape(1, d)
    b2 = ln_b.reshape(1, d)
    br = b_router.reshape(N_EXPERTS, 1)
    return pl.pallas_call(
        kern,
        grid=(n // tm,),
        in_specs=[tok, tok, tok, gspec(C_MG // d), gspec(C_MG // d + 1), full(w_out_bf16), full(g2), full(b2),
                  full(w_router_t), full(br)],
        out_specs=[tok, pl.BlockSpec((tm, d // 2), lambda i: (i, 0)), rt, rt, rt,
                   pl.BlockSpec((N_EXPERTS, LANES), lambda i: (0, 0))],
        out_shape=[jax.ShapeDtypeStruct((n, d), F32), jax.ShapeDtypeStruct((n, d // 2), jnp.uint32),
                   jax.ShapeDtypeStruct((8, n), I32),
                   jax.ShapeDtypeStruct((8, n), F32), jax.ShapeDtypeStruct((8, n), I32),
                   jax.ShapeDtypeStruct((N_EXPERTS, LANES), F32)],
        scratch_shapes=[pltpu.VMEM((N_EXPERTS, LANES), F32)],
        compiler_params=_cparams(("arbitrary",)),
        name="merge_outproj_ln_router",
    )(x, o_dn, o_nsa, p_in, p_in, w_out_bf16, g2, b2, w_router_t, br)


def _slot_kernel(ps_ref, idx_ref, pos_ref, slot_ref):
    idx = idx_ref[...]
    acc = pos_ref[...]
    for e in range(N_EXPERTS):
        acc = acc + jnp.where(idx == e, ps_ref[e], 0)
    slot_ref[...] = jnp.where(_iota(idx.shape, 0) < TOP_K, acc, 0)


def _slots(pad_start, idx, pos):
    n = idx.shape[1]
    blk = pl.BlockSpec((8, n), lambda i, ps: (0, 0))
    return pl.pallas_call(
        _slot_kernel,
        grid_spec=pltpu.PrefetchScalarGridSpec(num_scalar_prefetch=1, grid=(1,), in_specs=[blk, blk], out_specs=blk),
        out_shape=jax.ShapeDtypeStruct((8, n), I32),
        compiler_params=_cparams(("arbitrary",)),
        name="moe_slots",
    )(pad_start, idx, pos)


def _dispatch_kernel(slot_ref, x_ref, xs_in_ref, xs_ref, sem, *, tm):
    del xs_in_ref

    def row_copy(r, s):
        return pltpu.make_async_copy(x_ref.at[pl.ds(r, 1)], xs_ref.at[pl.ds(s, 1)], sem)

    def issue(r, carry):
        for k in range(TOP_K):
            row_copy(r, slot_ref[k, r]).start()
        return carry

    lax.fori_loop(0, tm, issue, 0, unroll=8)

    def drain(r, carry):
        for k in range(TOP_K):
            row_copy(0, 0).wait()
        return carry

    lax.fori_loop(0, tm, drain, 0, unroll=8)


def _dispatch(x1, slot, n_slots, *, tm):
    n, d = x1.shape
    assert n % tm == 0
    kern = functools.partial(_dispatch_kernel, tm=tm)
    xs0 = jnp.zeros((n_slots, d), x1.dtype)
    return pl.pallas_call(
        kern,
        grid=(n // tm,),
        in_specs=[pl.BlockSpec((8, tm), lambda i: (0, i), memory_space=pltpu.SMEM),
                  pl.BlockSpec((tm, d), lambda i: (i, 0)),
                  pl.BlockSpec(memory_space=pl.ANY)],
        out_specs=pl.BlockSpec(memory_space=pl.ANY),
        out_shape=jax.ShapeDtypeStruct((n_slots, d), x1.dtype),
        scratch_shapes=[pltpu.SemaphoreType.DMA(())],
        input_output_aliases={2: 0},
        compiler_params=_cparams(("arbitrary",)),
        name="moe_dispatch",
    )(slot, x1, xs0)


def _expert_kernel(be_ref, nu_ref, x_ref, wg_ref, wu_ref, wd_ref, y_ref, wg_b, wu_b, wd_b):
    i = pl.program_id(0)

    @pl.when((i == 0) | (be_ref[i] != be_ref[jnp.maximum(i - 1, 0)]))
    def _():
        wg_b[...] = wg_ref[0].astype(BF16)
        wu_b[...] = wu_ref[0].astype(BF16)
        wd_b[...] = wd_ref[0].astype(BF16)

    @pl.when(i < nu_ref[0])
    def _():
        w = x_ref[...]
        x = jnp.concatenate([pltpu.bitcast(w << 16, F32), pltpu.bitcast(w & jnp.uint32(0xFFFF0000), F32)],
                            axis=1).astype(BF16)
        hg = jnp.dot(x, wg_b[...], preferred_element_type=F32)
        hu = jnp.dot(x, wu_b[...], preferred_element_type=F32)
        y_ref[...] = jnp.dot((_silu(hg) * hu).astype(BF16), wd_b[...], preferred_element_type=F32)

    @pl.when(i >= nu_ref[0])
    def _():
        y_ref[...] = jnp.zeros(y_ref.shape, F32)


def _experts(xs, blk_exp, n_used, w_gate, w_up, w_down, *, blk):
    n_slots, dpk = xs.shape
    d, de = w_gate.shape[1:]
    assert dpk * 2 == d
    n_blocks = n_slots // blk
    return pl.pallas_call(
        _expert_kernel,
        grid_spec=pltpu.PrefetchScalarGridSpec(
            num_scalar_prefetch=2,
            grid=(n_blocks,),
            in_specs=[pl.BlockSpec((blk, dpk), lambda i, be, nu: (jnp.maximum(jnp.minimum(i, nu[0] - 1), 0), 0)),
                      pl.BlockSpec((1, d, de), lambda i, be, nu: (be[i], 0, 0)),
                      pl.BlockSpec((1, d, de), lambda i, be, nu: (be[i], 0, 0)),
                      pl.BlockSpec((1, de, d), lambda i, be, nu: (be[i], 0, 0))],
            out_specs=pl.BlockSpec((blk, d), lambda i, be, nu: (i, 0)),
            scratch_shapes=[pltpu.VMEM((d, de), BF16), pltpu.VMEM((d, de), BF16), pltpu.VMEM((de, d), BF16)]),
        out_shape=jax.ShapeDtypeStruct((n_slots, d), F32),
        compiler_params=_cparams(("arbitrary",)),
        name="moe_experts",
    )(blk_exp, n_used, xs, w_gate, w_up, w_down)


def _combine_kernel(slot_ref, x_ref, w_ref, ys_ref, wsg_ref, wsu_ref, wsd_ref, g_ref, b_ref, o_ref, buf, sem,
                    *, tm, alpha):
    def row_copy(s, k, r):
        return pltpu.make_async_copy(ys_ref.at[pl.ds(s, 1)], buf.at[k, pl.ds(r, 1)], sem)

    def issue(r, carry):
        for k in range(TOP_K):
            row_copy(slot_ref[k, r], k, r).start()
        return carry

    lax.fori_loop(0, tm, issue, 0, unroll=8)
    x = x_ref[...]
    xb = x.astype(BF16)
    hs = _silu(jnp.dot(xb, wsg_ref[...], preferred_element_type=F32)) * jnp.dot(xb, wsu_ref[...],
                                                                               preferred_element_type=F32)
    acc = alpha * x + _bdot(hs, wsd_ref[...])

    def drain(r, carry):
        for k in range(TOP_K):
            row_copy(0, k, 0).wait()
        return carry

    lax.fori_loop(0, tm, drain, 0, unroll=8)
    w = w_ref[...]
    for k in range(TOP_K):
        acc = acc + w[:, k:k + 1] * buf[k]
    o_ref[...] = _layer_norm(acc, g_ref[...], b_ref[...])


def _combine(x1, slot, w_tok, ys, ws_gate, ws_up, ws_down, ln_g, ln_b, *, tm, alpha):
    n, d = x1.shape
    assert n % tm == 0
    kern = functools.partial(_combine_kernel, tm=tm, alpha=alpha)
    full = lambda a: pl.BlockSpec(a.shape, lambda i: (0,) * a.ndim)
    g2 = ln_g.reshape(1, d)
    b2 = ln_b.reshape(1, d)
    return pl.pallas_call(
        kern,
        grid=(n // tm,),
        in_specs=[pl.BlockSpec((8, tm), lambda i: (0, i), memory_space=pltpu.SMEM),
                  pl.BlockSpec((tm, d), lambda i: (i, 0)),
                  pl.BlockSpec((tm, 8), lambda i: (i, 0)),
                  pl.BlockSpec(memory_space=pl.ANY),
                  full(ws_gate), full(ws_up), full(ws_down), full(g2), full(b2)],
        out_specs=pl.BlockSpec((tm, d), lambda i: (i, 0)),
        out_shape=jax.ShapeDtypeStruct((n, d), F32),
        scratch_shapes=[pltpu.VMEM((TOP_K, tm, d), F32), pltpu.SemaphoreType.DMA(())],
        compiler_params=_cparams(("arbitrary",)),
        name="moe_combine_ln",
    )(slot, x1, w_tok, ys, ws_gate, ws_up, ws_down, g2, b2)


def _moe_layer(x1, xp, idx, wts, pos, counts, w_gate, w_up, w_down, ws_gate, ws_up, ws_down, ln_g, ln_b,
               *, blk, tm_d, tm_c, alpha):
    n = x1.shape[0]
    cnt = counts[:, 0].astype(I32)
    padded = (cnt + blk - 1) // blk * blk
    pad_end = jnp.cumsum(padded)
    slot = _slots((pad_end - padded).astype(I32), idx, pos)
    n_blocks = -(-(n * TOP_K) // blk) + N_EXPERTS
    blk_exp = jnp.minimum(jnp.sum(pad_end[None, :] <= (jnp.arange(n_blocks) * blk)[:, None], axis=1),
                          N_EXPERTS - 1).astype(I32)
    n_used = (pad_end[-1:] // blk).astype(I32)
    xs = _dispatch(xp, slot, n_blocks * blk, tm=tm_d)
    ys = _experts(xs, blk_exp, n_used, w_gate, w_up, w_down, blk=blk)
    return _combine(x1, slot, wts.T, ys, ws_gate.astype(BF16), ws_up.astype(BF16), ws_down.astype(BF16),
                    ln_g, ln_b, tm=tm_c, alpha=alpha)


def kernel(x_prompt, x_sample, cache_cmp_kv, cache_slc_kv, cache_win_kv, state_delta_S, state_delta_conv, page_table, w_in, dn_conv_w, dn_A_log, dn_dt_bias, dn_norm_w, nsa_cmp_w1, nsa_cmp_pos, nsa_cmp_w2, w_out, ln1_g, ln1_b, w_router, b_router, w_exp_gate, w_exp_up, w_exp_down, w_sh_gate, w_sh_up, w_sh_down, ln2_g, ln2_b):
    depth = w_in.shape[0]
    assert depth == 1
    alpha = (2.0 * depth) ** 0.25
    bsz, seq, d = x_prompt.shape
    sb, st, _ = x_sample.shape
    tl = _tiles(bsz * seq, seq, sb * st)
    w_r = _reorder_w_in(w_in[0])
    wf, w2bd = _cmp_weights(nsa_cmp_w1[0], nsa_cmp_w2[0])
    mix_w = (w_r, dn_conv_w[0], dn_A_log[0], dn_dt_bias[0], dn_norm_w[0], wf, w2bd, nsa_cmp_pos[0], nsa_cmp_w1[0])
    p3, o_dn, s_p, o_nsa = _prompt_mixers(x_prompt, *mix_w, tl)
    ps, o_dn_s, s_s, o_nsa_s = _sample_mixers(x_sample, cache_cmp_kv[0], cache_slc_kv[0], cache_win_kv[0],
                                              state_delta_S[0], state_delta_conv[0], page_table, *mix_w, tl)
    wo = w_out[0].astype(BF16)
    wrt = w_router[0].T

    def ffn(x2, o_dn2, o_nsa2, p_any, tm, blk, tm_d, tm_c):
        x1, xp, idx, wts, pos, counts = _post_mixer(x2, o_dn2, o_nsa2, p_any, wo, ln1_g[0], ln1_b[0], wrt,
                                                    b_router[0], tm=tm, alpha=alpha)
        return _moe_layer(x1, xp, idx, wts, pos, counts, w_exp_gate[0], w_exp_up[0], w_exp_down[0],
                          w_sh_gate[0], w_sh_up[0], w_sh_down[0], ln2_g[0], ln2_b[0],
                          blk=blk, tm_d=tm_d, tm_c=tm_c, alpha=alpha)

    y_p = ffn(x_prompt.reshape(-1, d), o_dn.reshape(-1, d), o_nsa.reshape(-1, d), p3,
              tl["post_tm"], tl["moe_blk"], tl["moe_tm_dispatch"], tl["moe_tm_combine"])
    y_s = ffn(x_sample.reshape(-1, d), o_dn_s[:, :st].reshape(-1, d), o_nsa_s[:, :st].reshape(-1, d), ps,
              tl["sample_tm"], tl["sample_moe_blk"], tl["sample_tm"], tl["sample_tm"])

    kv_shape = (2, NSA_KV_HEADS, NSA_HEAD_DIM)

    def kv_rows(pp, c0):
        return pp[:, :, c0:c0 + KV_COLS].reshape(pp.shape[:2] + kv_shape)

    nconv = CONV_W - 1
    conv_p = jnp.concatenate([jnp.zeros((bsz, nconv, 3 * DN_WIDTH), F32), p3[:, :, :3 * DN_WIDTH]], 1)[:, -nconv:]
    conv_s = jnp.concatenate([state_delta_conv[0], ps[:, :, :3 * DN_WIDTH]], 1)[:, -nconv:]
    past = page_table.shape[1] * PAGE_SIZE
    win_s = jnp.concatenate([cache_win_kv[0], kv_rows(ps, C_KVW)], 1)[:, -min(WINDOW, past + st):]
    return (y_p.reshape(x_prompt.shape), y_s.reshape(x_sample.shape),
            kv_rows(p3, C_KVC)[None], kv_rows(p3, C_KVS)[None], kv_rows(p3, C_KVW)[:, -min(WINDOW, seq):][None],
            s_p[None], conv_p[None],
            kv_rows(ps, C_KVC)[None], kv_rows(ps, C_KVS)[None], win_s[None], s_s[None], conv_s[None])
```

```python
import functools
import math

import jax
import jax.numpy as jnp
import numpy as np
from jax import lax
from jax.experimental import pallas as pl
from jax.experimental.pallas import tpu as pltpu

F32 = jnp.float32
BF16 = jnp.bfloat16
I32 = jnp.int32
HIGHEST = lax.Precision.HIGHEST

D_MODEL = 1024
PAGE_SIZE = 128
DN_HEADS = 8
DN_HEAD_DIM = 128
DN_WIDTH = DN_HEADS * DN_HEAD_DIM
CONV_W = 4
DN_CHUNK = 64
NSA_HEADS = 16
NSA_KV_HEADS = 2
NSA_GROUP = NSA_HEADS // NSA_KV_HEADS
NSA_HEAD_DIM = 64
NSA_WIDTH = NSA_HEADS * NSA_HEAD_DIM
NSA_KV_WIDTH = NSA_KV_HEADS * NSA_HEAD_DIM
KV_COLS = 2 * NSA_KV_WIDTH
CMP_BLOCK = 32
CMP_STRIDE = 16
SLC_BLOCK = 64
N_SELECT = 16
WINDOW = 512
N_EXPERTS = 64
TOP_K = 6
N_GROUPS = 8
TOPK_GROUPS = 4
ROUTED_SCALE = 2.5
LN_EPS = 1e-5
RMS_EPS = 1e-6
NEG = -1e30
LOG2E = math.log2(math.e)
FORCE_BONUS = 1e6

C_QKV = 0
C_Z = 3072
C_NQ = 4096
C_MG = 5120
C_KVC = 7168
C_KVS = 7424
C_KVW = 7680
C_SMALL = 7936
P_COLS = 8064
SM_A = 0
SM_B = DN_HEADS
SM_NG = 2 * DN_HEADS

LANES = 128
VMEM_LIMIT = 48 * 1024 * 1024
KEY_TILE = 256


def _tiles(n_prompt_tokens, seq, n_sample_tokens):
    return dict(
        proj_tm=min(512, n_prompt_tokens), proj_tn=P_COLS // 3,
        dn_tb=min(256, seq), dn_heads=8,
        nsa_tq=256,
        cmp_pages=32,
        post_tm=min(256, n_prompt_tokens),
        moe_blk=256, moe_tm_dispatch=min(256, n_prompt_tokens), moe_tm_combine=min(128, n_prompt_tokens),
        sample_moe_blk=64, sample_tm=n_sample_tokens,
    )


def _cparams(sem):
    return pltpu.CompilerParams(dimension_semantics=sem, vmem_limit_bytes=VMEM_LIMIT)


def _bdot(a, b):
    return jnp.dot(a.astype(BF16), b.astype(BF16), preferred_element_type=F32)


def _bdot_nt(a, b):
    return lax.dot_general(a.astype(BF16), b.astype(BF16), (((1,), (1,)), ((), ())),
                           preferred_element_type=F32)


def _bdot_tn(a, b):
    return lax.dot_general(a.astype(BF16), b.astype(BF16), (((0,), (0,)), ((), ())),
                           preferred_element_type=F32)


def _hdot(a, b):
    return jnp.dot(a, b, precision=HIGHEST, preferred_element_type=F32)


def _hdot_tn(a, b):
    return lax.dot_general(a, b, (((0,), (0,)), ((), ())), precision=HIGHEST,
                           preferred_element_type=F32)


def _hdot_nt(a, b):
    return lax.dot_general(a, b, (((1,), (1,)), ((), ())), precision=HIGHEST,
                           preferred_element_type=F32)


def _sigmoid(x):
    return 1.0 / (1.0 + jnp.exp(-x))


def _silu(x):
    return x * _sigmoid(x)


def _softplus(x):
    return jnp.maximum(x, 0.0) + jnp.log(1.0 + jnp.exp(-jnp.abs(x)))


def _iota(shape, dim):
    return lax.broadcasted_iota(I32, shape, dim)


def _log2(n):
    assert n & (n - 1) == 0
    return int(math.log2(n))


def _mm_kernel(x_ref, w_ref, o_ref):
    o_ref[...] = jnp.dot(x_ref[...].astype(BF16), w_ref[...], preferred_element_type=F32)


def _matmul(x, w_bf16, tm, tn):
    m, k = x.shape
    n = w_bf16.shape[1]
    assert m % tm == 0 and n % tn == 0
    return pl.pallas_call(
        _mm_kernel,
        grid=(n // tn, m // tm),
        in_specs=[pl.BlockSpec((tm, k), lambda j, i: (i, 0)),
                  pl.BlockSpec((k, tn), lambda j, i: (0, j))],
        out_specs=pl.BlockSpec((tm, tn), lambda j, i: (i, j)),
        out_shape=jax.ShapeDtypeStruct((m, n), F32),
        compiler_params=_cparams(("parallel", "parallel")),
        name="dense_matmul",
    )(x, w_bf16)


def _reorder_w_in(w_in):
    o = 0
    seg = {}
    for name, size in (("qkv", 3 * DN_WIDTH), ("z", DN_WIDTH), ("a", DN_HEADS), ("b", DN_HEADS),
                       ("nq", NSA_WIDTH), ("kvc", KV_COLS), ("kvs", KV_COLS),
                       ("kvw", KV_COLS), ("ng", 3 * NSA_HEADS), ("mg", 2 * D_MODEL)):
        seg[name] = w_in[:, o:o + size]
        o += size
    assert o == w_in.shape[1]
    pad = jnp.zeros((w_in.shape[0], P_COLS - C_SMALL - SM_NG - 3 * NSA_HEADS), w_in.dtype)
    w = jnp.concatenate([seg["qkv"], seg["z"], seg["nq"], seg["mg"], seg["kvc"], seg["kvs"], seg["kvw"],
                         seg["a"], seg["b"], seg["ng"], pad], axis=1)
    assert w.shape[1] == P_COLS
    return w.astype(BF16)


def _tri_inverse(lmats, c):
    r = _iota((c, c), 0)
    q = _iota((c, c), 1)
    eye = (r == q).astype(F32)
    blk = min(16, c)
    shift = _log2(blk)
    same = (r >> shift) == (q >> shift)
    dmats = [jnp.where(same, lm, 0.0) for lm in lmats]
    prods = [eye - dm for dm in dmats]
    dpows = dmats
    k = 2
    while k < blk:
        dpows = [_bdot(dp, dp) for dp in dpows]
        prods = [pr + _bdot(pr, dp) for pr, dp in zip(prods, dpows)]
        k *= 2
    if c == blk:
        return prods
    mmats = [_bdot(pr, lm - dm) for pr, lm, dm in zip(prods, lmats, dmats)]
    outers = [eye - mm for mm in mmats]
    mpows = mmats
    k = 2
    while k < c // blk:
        mpows = [_bdot(mp, mp) for mp in mpows]
        outers = [ou + _bdot(ou, mp) for ou, mp in zip(outers, mpows)]
        k *= 2
    return [_bdot(ou, pr) for ou, pr in zip(outers, prods)]


def _dn_kernel(q_ref, k_ref, v_ref, z_ref, sm_ref, hq_ref, hk_ref, hv_ref, cwq_ref, cwk_ref, cwv_ref,
               hp_ref, nw_ref, s0_ref, o_ref, sout_ref,
               s_scr, xp_scr, qn_scr, kn_scr, vn_scr, gb_scr, u_scr, w_scr, qe_scr, kd_scr, a_scr, eg_scr,
               *, tb, c, t_valid, hp):
    hb = pl.program_id(1)
    t = pl.program_id(2)
    nt = pl.num_programs(2)
    dk = DN_HEAD_DIM
    nc = tb // c

    @pl.when(t == 0)
    def _():
        s_scr[...] = s0_ref[0]
        xp_scr[0, 0:8, :] = hq_ref[0]
        xp_scr[1, 0:8, :] = hk_ref[0]
        xp_scr[2, 0:8, :] = hv_ref[0]

    rows = t * tb + _iota((tb, 1), 0)
    valid = rows < t_valid

    def conv(i, raw_ref, cw_ref):
        xp_scr[i, 8:8 + tb, :] = raw_ref[0]
        acc = xp_scr[i, 8:8 + tb, :] * cw_ref[CONV_W - 1:CONV_W, :]
        for j in range(CONV_W - 1):
            acc = acc + xp_scr[i, 8 - (CONV_W - 1) + j:8 - (CONV_W - 1) + j + tb, :] * cw_ref[j:j + 1, :]
        tail = xp_scr[i, tb:tb + 8, :]
        xp_scr[i, 0:8, :] = tail
        return _silu(acc)

    qc = conv(0, q_ref, cwq_ref)
    kc = conv(1, k_ref, cwk_ref)
    vc = conv(2, v_ref, cwv_ref)
    lane = _iota((1, LANES), 1)
    sm = sm_ref[0]
    for hh in range(hp):
        h = hb * hp + hh
        cs = slice(hh * dk, (hh + 1) * dk)
        qh = qc[:, cs]
        kh = kc[:, cs]
        qn = qh * lax.rsqrt(jnp.sum(qh * qh, -1, keepdims=True) + 1e-6) * (dk ** -0.5)
        kn = kh * lax.rsqrt(jnp.sum(kh * kh, -1, keepdims=True) + 1e-6)
        a_h = jnp.sum(jnp.where(lane == SM_A + h, sm, 0.0), -1, keepdims=True)
        b_h = jnp.sum(jnp.where(lane == SM_B + h, sm, 0.0), -1, keepdims=True)
        neg_a = -jnp.exp(jnp.sum(jnp.where(lane == h, hp_ref[0:1, :], 0.0), -1, keepdims=True))
        dtb = jnp.sum(jnp.where(lane == h, hp_ref[1:2, :], 0.0), -1, keepdims=True)
        g = neg_a * _softplus(a_h + dtb)
        beta = _sigmoid(b_h)
        qn_scr[hh] = jnp.where(valid, qn, 0.0)
        kn_scr[hh] = jnp.where(valid, kn, 0.0)
        vn_scr[hh] = jnp.where(valid, vc[:, cs], 0.0)
        gb_scr[hh] = jnp.where(lane == 0, jnp.where(valid, g, 0.0), jnp.where(valid, beta, 0.0))

    r = _iota((c, c), 0)
    q = _iota((c, c), 1)
    incl = r >= q
    strict = r > q

    where = [(hh, slice(ci * c, (ci + 1) * c)) for hh in range(hp) for ci in range(nc)]
    lmats, vbs, kbes = [], [], []
    for hh, rs in where:
        qi = qn_scr[hh, rs, :]
        ki = kn_scr[hh, rs, :]
        gb = gb_scr[hh, rs, :]
        gi = gb[:, 0:1]
        bi = gb[:, 1:2]
        g_row = jnp.sum(jnp.where(r == q, gi, 0.0), 0, keepdims=True)
        gcum_col = jnp.sum(jnp.where(incl, g_row, 0.0), 1, keepdims=True)
        gcum_row = jnp.sum(jnp.where(r <= q, gi, 0.0), 0, keepdims=True)
        decay = jnp.where(incl, jnp.exp(jnp.where(incl, gcum_col - gcum_row, 0.0)), 0.0)
        kb = ki * bi
        eg = jnp.exp(gcum_col)
        g_last = gcum_col[c - 1:c, :]
        lmats.append(jnp.where(strict, _bdot_nt(kb, ki) * decay, 0.0))
        vbs.append(vn_scr[hh, rs, :] * bi)
        kbes.append(kb * eg)
        a_scr[hh, rs, :] = jnp.where(incl, _bdot_nt(qi, ki) * decay, 0.0)
        qe_scr[hh, rs, :] = qi * eg
        kd_scr[hh, rs, :] = ki * jnp.exp(g_last - gcum_col)
        e0 = rs.start // c * 8
        eg_scr[hh, e0:e0 + 8, :] = jnp.broadcast_to(jnp.exp(g_last), (8, LANES))
    tms = _tri_inverse(lmats, c)
    for (hh, rs), tm, vb, kbe in zip(where, tms, vbs, kbes):
        u_scr[hh, rs, :] = _bdot(tm, vb)
        w_scr[hh, rs, :] = _bdot(tm, kbe)

    nw = nw_ref[...]

    def chunk(ci, carry):
        r0 = pl.multiple_of(ci * c, c)
        e0 = pl.multiple_of(ci * 8, 8)
        for hh in range(hp):
            s = s_scr[hh]
            wq = jnp.concatenate([w_scr[hh, pl.ds(r0, c), :], qe_scr[hh, pl.ds(r0, c), :]], axis=0)
            ws = _bdot(wq, s)
            v_new = u_scr[hh, pl.ds(r0, c), :] - ws[0:c]
            o = ws[c:2 * c] + _bdot(a_scr[hh, pl.ds(r0, c), :], v_new)
            s_scr[hh] = s * eg_scr[hh, pl.ds(e0, 8), :][0:1, :] + _bdot_tn(kd_scr[hh, pl.ds(r0, c), :], v_new)
            o = o * lax.rsqrt(jnp.mean(o * o, -1, keepdims=True) + RMS_EPS) * nw
            o_ref[0, pl.ds(r0, c), hh * dk:(hh + 1) * dk] = o * _silu(z_ref[0, pl.ds(r0, c), hh * dk:(hh + 1) * dk])
        return carry

    lax.fori_loop(0, nc, chunk, 0)

    @pl.when(t == nt - 1)
    def _():
        sout_ref[0] = s_scr[...]


def _deltanet(p3, hist, s0, conv_w, a_log, dt_bias, norm_w, *, t_valid, tb, c, hp):
    bsz, tpad, _ = p3.shape
    assert tpad % tb == 0 and tb % c == 0 and tb % 8 == 0 and DN_HEADS % hp == 0
    nt = tpad // tb
    dk = DN_HEAD_DIM
    wid = hp * dk
    cw = jnp.concatenate([conv_w, jnp.zeros((8 - CONV_W, conv_w.shape[1]), F32)], 0)
    hpar = jnp.zeros((8, LANES), F32).at[0, :DN_HEADS].set(a_log).at[1, :DN_HEADS].set(dt_bias)
    nw = norm_w.reshape(1, dk)
    nb = DN_WIDTH // wid

    tok = lambda off: pl.BlockSpec((1, tb, wid), lambda b, h, t: (b, t, off + h))
    his = lambda off: pl.BlockSpec((1, 8, wid), lambda b, h, t: (b, 0, off + h))
    cws = lambda off: pl.BlockSpec((8, wid), lambda b, h, t: (0, off + h))
    st = pl.BlockSpec((1, hp, dk, dk), lambda b, h, t: (b, h, 0, 0))
    kern = functools.partial(_dn_kernel, tb=tb, c=c, t_valid=t_valid, hp=hp)
    big = pltpu.VMEM((hp, tb, dk), F32)
    return pl.pallas_call(
        kern,
        grid=(bsz, DN_HEADS // hp, nt),
        in_specs=[tok(0), tok(nb), tok(2 * nb), tok(C_Z // wid),
                  pl.BlockSpec((1, tb, LANES), lambda b, h, t: (b, t, C_SMALL // LANES)),
                  his(0), his(nb), his(2 * nb), cws(0), cws(nb), cws(2 * nb),
                  pl.BlockSpec((8, LANES), lambda b, h, t: (0, 0)),
                  pl.BlockSpec((1, dk), lambda b, h, t: (0, 0)),
                  st],
        out_specs=[pl.BlockSpec((1, tb, wid), lambda b, h, t: (b, t, h)), st],
        out_shape=[jax.ShapeDtypeStruct((bsz, tpad, DN_WIDTH), F32),
                   jax.ShapeDtypeStruct((bsz, DN_HEADS, dk, dk), F32)],
        scratch_shapes=[pltpu.VMEM((hp, dk, dk), F32),
                        pltpu.VMEM((3, tb + 8, wid), F32),
                        big, big, big, big, big, big, big, big,
                        pltpu.VMEM((hp, tb, c), F32),
                        pltpu.VMEM((hp, (tb // c) * 8, LANES), F32)],
        compiler_params=_cparams(("parallel", "parallel", "arbitrary")),
        name="gated_deltanet",
    )(p3, p3, p3, p3, p3, hist, hist, hist, cw, cw, cw, hpar, nw, s0)


def _cmp_weights(w1, w2):
    w1r = w1.reshape(2, CMP_BLOCK // CMP_STRIDE, CMP_STRIDE, NSA_HEAD_DIM, NSA_HEAD_DIM)
    eye = jnp.eye(2, dtype=F32)
    wf = jnp.einsum("srpde,st,hg->pshdrtge", w1r, eye, eye)
    wf = wf.reshape(CMP_STRIDE * KV_COLS, 2 * KV_COLS)
    w2bd = jnp.einsum("sef,st,hg->shetgf", w2, eye, eye).reshape(KV_COLS, KV_COLS)
    return wf.astype(BF16), w2bd.astype(BF16)


def _cmp_epi_kernel(p_ref, pos_ref, w1_ref, w2_ref, o_ref):
    pm = p_ref[0]
    n = pm.shape[0]
    nxt = pltpu.roll(pm[:, KV_COLS:2 * KV_COLS], n - 1, 0)
    b_k = _hdot(pos_ref[0:1, :], w1_ref[0])
    b_v = _hdot(pos_ref[1:2, :], w1_ref[1])
    bias = jnp.concatenate([b_k, b_k, b_v, b_v], axis=-1)
    h = pm[:, 0:KV_COLS] + nxt + bias
    o_ref[0] = _bdot(jax.nn.gelu(h), w2_ref[...])


def _cmp_epilogue(pmat, pos, w1, w2bd):
    bsz, n_sub, wid = pmat.shape
    return pl.pallas_call(
        _cmp_epi_kernel,
        grid=(bsz,),
        in_specs=[pl.BlockSpec((1, n_sub, wid), lambda b: (b, 0, 0)),
                  pl.BlockSpec(pos.shape, lambda b: (0, 0)),
                  pl.BlockSpec(w1.shape, lambda b: (0, 0, 0)),
                  pl.BlockSpec(w2bd.shape, lambda b: (0, 0))],
        out_specs=pl.BlockSpec((1, n_sub, wid // 2), lambda b: (b, 0, 0)),
        out_shape=jax.ShapeDtypeStruct((bsz, n_sub, wid // 2), F32),
        compiler_params=_cparams(("parallel",)),
        name="nsa_compress_epilogue",
    )(pmat, pos, w1, w2bd)


def _cmp_paged_kernel(pt_ref, cache_ref, w_ref, o_ref, buf, rows_k, rows_v, sem, *, npg):
    i = pl.program_id(0)
    n = pl.num_programs(0)
    spp = PAGE_SIZE // CMP_STRIDE

    def page_copy(page, slot, j):
        return pltpu.make_async_copy(cache_ref.at[page], buf.at[slot, j], sem.at[slot])

    def issue(step, slot):
        for j in range(npg):
            page_copy(pt_ref[step * npg + j], slot, j).start()

    @pl.when(i == 0)
    def _():
        issue(0, 0)

    @pl.when(i + 1 < n)
    def _():
        issue(i + 1, (i + 1) % 2)

    slot = i % 2
    for j in range(npg):
        page_copy(0, slot, j).wait()
    rows = (rows_k, rows_v)
    for hf in range(2):
        for j in range(npg):
            rows[hf][j * PAGE_SIZE:(j + 1) * PAGE_SIZE, :] = buf[slot, j, hf * LANES:(hf + 1) * LANES, :].T
    for hf in range(2):
        acc = jnp.zeros((o_ref.shape[0], 2 * LANES), F32)
        for p in range(CMP_STRIDE):
            xs = rows[hf][pl.ds(p, npg * spp, stride=CMP_STRIDE), :]
            acc = acc + jnp.dot(xs.astype(BF16), w_ref[hf, p], preferred_element_type=F32)
        for rslot in range(CMP_BLOCK // CMP_STRIDE):
            c0 = rslot * KV_COLS + hf * LANES
            o_ref[:, c0:c0 + LANES] = acc[:, rslot * LANES:(rslot + 1) * LANES]


def _cmp_paged(cache_t, page_table, wf, *, npg):
    n_pool, cols, psz = cache_t.shape
    bsz, n_pages = page_table.shape
    total = bsz * n_pages
    spp = psz // CMP_STRIDE
    assert total % npg == 0 and cols == KV_COLS == 2 * LANES and psz == PAGE_SIZE
    kern = functools.partial(_cmp_paged_kernel, npg=npg)
    w5 = wf.reshape(CMP_STRIDE, 2, LANES, CMP_BLOCK // CMP_STRIDE, 2, LANES)
    wfs = jnp.stack([w5[:, s, :, :, s, :] for s in range(2)]).reshape(2, CMP_STRIDE, LANES, 2 * LANES)
    return pl.pallas_call(
        kern,
        grid_spec=pltpu.PrefetchScalarGridSpec(
            num_scalar_prefetch=1,
            grid=(total // npg,),
            in_specs=[pl.BlockSpec(memory_space=pl.ANY),
                      pl.BlockSpec(wfs.shape, lambda i, pt: (0, 0, 0, 0))],
            out_specs=pl.BlockSpec((npg * spp, wf.shape[1]), lambda i, pt: (i, 0)),
            scratch_shapes=[pltpu.VMEM((2, npg, cols, psz), F32), pltpu.VMEM((npg * psz, LANES), F32),
                            pltpu.VMEM((npg * psz, LANES), F32), pltpu.SemaphoreType.DMA((2,))]),
        out_shape=jax.ShapeDtypeStruct((total * spp, wf.shape[1]), F32),
        compiler_params=_cparams(("arbitrary",)),
        name="nsa_compress_paged",
    )(page_table.reshape(-1), cache_t, wfs)


def _slope(head):
    return 2.0 ** (-8.0 * (head + 1) / NSA_HEADS)


def _gather_heads(q_ref, hk):
    g = NSA_GROUP
    dh = NSA_HEAD_DIM
    qs = jnp.concatenate([q_ref[0, :, (hk * g + i) * dh:(hk * g + i + 1) * dh] for i in range(g)], axis=0)
    return qs * (dh ** -0.5)


def _cmp_branch(qs, kc, vc, hk, valid_c, dist_c, tq):
    s_all = _bdot_nt(qs, kc)
    ps = []
    psum = None
    for i in range(NSA_GROUP):
        s = s_all[i * tq:(i + 1) * tq] - _slope(hk * NSA_GROUP + i) * dist_c
        s = jnp.where(valid_c, s, NEG)
        m = jnp.max(s, -1, keepdims=True)
        p = jnp.where(valid_c, jnp.exp(s - m), 0.0)
        p = p / jnp.maximum(jnp.sum(p, -1, keepdims=True), 1e-30)
        ps.append(p)
        psum = p if psum is None else psum + p
    return _bdot(jnp.concatenate(ps, axis=0), vc), psum


def _slope_features(tq):
    out = np.zeros((NSA_KV_HEADS, NSA_GROUP * tq, NSA_HEAD_DIM), np.float32)
    for hk in range(NSA_KV_HEADS):
        for g in range(NSA_GROUP):
            rem = _slope(hk * NSA_GROUP + g) * LOG2E
            for i in range(3):
                piece = float(np.float32(rem).astype(jnp.bfloat16))
                out[hk, g * tq:(g + 1) * tq, 2 * i:2 * i + 2] = piece
                rem -= piece
    return jnp.asarray(out)


def _position_features(pos):
    lo = (pos % 256).astype(F32)
    hi = (pos - pos % 256).astype(F32)
    cols = jnp.stack([hi, lo, hi, lo, hi, lo], axis=1)
    return jnp.pad(cols, ((0, 0), (0, NSA_HEAD_DIM - 6)))


_NT = (((1,), (1,)), ((), ()))


def _cmp_branch_aug(q_aug, kc_aug, vc, valid_c, tq):
    raw = lax.dot_general(q_aug, kc_aug, _NT, preferred_element_type=F32)
    bias = jnp.where(valid_c, 0.0, NEG)
    ps = []
    psum = None
    for i in range(NSA_GROUP):
        s = raw[i * tq:(i + 1) * tq] + bias
        p = jnp.where(valid_c, jnp.exp2(s - jnp.max(s, -1, keepdims=True)), 0.0)
        p = p * (1.0 / jnp.maximum(jnp.sum(p, -1, keepdims=True), 1e-30))
        ps.append(p)
        psum = p if psum is None else psum + p
    return _bdot(jnp.concatenate(ps, axis=0), vc), psum


def _flash_branch(q_aug, kf, kv_ref, hk, t_lo, t_hi, bias_fn, m_scr, acc_scr, tq):
    g = NSA_GROUP
    dh = NSA_HEAD_DIM
    m_scr[...] = jnp.full(m_scr.shape, NEG, F32)
    acc_scr[...] = jnp.zeros(acc_scr.shape, F32)
    ones = jnp.ones((KEY_TILE, dh), F32)

    def body(i, carry):
        t = t_hi - 1 - i
        k0 = pl.multiple_of(t * KEY_TILE, KEY_TILE)
        k_aug = jnp.concatenate([kv_ref[0, pl.ds(k0, KEY_TILE), hk * dh:(hk + 1) * dh], kf], axis=1).astype(BF16)
        v = kv_ref[0, pl.ds(k0, KEY_TILE), NSA_KV_WIDTH + hk * dh:NSA_KV_WIDTH + (hk + 1) * dh]
        vaug = jnp.concatenate([v, ones], axis=1).astype(BF16)
        bias = bias_fn(k0)
        k0f = k0.astype(F32)
        half = g // 2
        raws = [lax.dot_general(q_aug[h * half * tq:(h + 1) * half * tq], k_aug, _NT, preferred_element_type=F32)
                for h in range(2)]
        for h in range(2):
            ps = []
            alphas = []
            for jj in range(half):
                j = h * half + jj
                rs = slice(j * tq, (j + 1) * tq)
                shift = k0f * (_slope(hk * g + j) * LOG2E)
                s = raws[h][jj * tq:(jj + 1) * tq] + bias
                m_old = m_scr[rs, :]
                m_new = jnp.maximum(m_old, jnp.max(s, -1, keepdims=True) + shift)
                alphas.append(jnp.exp2(m_old - m_new))
                ps.append(jnp.exp2(s - jnp.concatenate([m_new - shift] * (KEY_TILE // LANES), axis=1)).astype(BF16))
                m_scr[rs, :] = m_new
            hs = slice(h * half * tq, (h + 1) * half * tq)
            pv = jnp.dot(jnp.concatenate(ps, axis=0), vaug, preferred_element_type=F32)
            acc_scr[hs, :] = jnp.concatenate(alphas, axis=0) * acc_scr[hs, :] + pv
        return carry

    lax.fori_loop(0, t_hi - t_lo, body, 0)
    acc = acc_scr[...]
    return acc[:, 0:dh] / jnp.maximum(acc[:, dh:2 * dh], 1e-30)


def _gate_combine(sm, hk, o_c, o_s, o_w, o_ref, tq):
    g = NSA_GROUP
    dh = NSA_HEAD_DIM
    outs = []
    for i in range(g):
        c0 = SM_NG + (hk * g + i) * 3
        gt = _sigmoid(sm[:, c0:c0 + 3])
        rows = slice(i * tq, (i + 1) * tq)
        outs.append(gt[:, 0:1] * o_c[rows] + gt[:, 1:2] * o_s[rows] + gt[:, 2:3] * o_w[rows])
    for i in range(0, g, 2):
        c0 = (hk * g + i) * dh
        o_ref[0, :, c0:c0 + 2 * dh] = jnp.concatenate([outs[i], outs[i + 1]], axis=-1)


def _nsa_prompt_kernel(q_ref, sm_ref, kvs_ref, kvw_ref, kc_ref, qsl_ref, kf_ref, kfc_ref, o_ref, m_scr, acc_scr,
                       *, tq, seq, n_cmp):
    dh = NSA_HEAD_DIM
    q0 = pl.program_id(1) * tq
    n_sub = kc_ref.shape[1]
    n_slc = seq // SLC_BLOCK
    qpos_i = q0 + _iota((tq, 1), 0)
    sm = sm_ref[0]
    t_hi = (q0 + tq + KEY_TILE - 1) // KEY_TILE
    t_lo_w = jnp.maximum(q0 - (WINDOW - 1), 0) // KEY_TILE

    cidx = _iota((1, n_sub), 1)
    valid_c = (cidx * CMP_STRIDE + (CMP_BLOCK - 1) <= qpos_i) & (cidx < n_cmp)
    cr = _iota((n_slc, n_sub), 1) * CMP_STRIDE
    s_st = _iota((n_slc, n_sub), 0) * SLC_BLOCK
    cover_t = jnp.where((cr < s_st + SLC_BLOCK) & (cr + (CMP_BLOCK - 1) >= s_st), 1.0, 0.0)
    srow = _iota((n_slc, 1), 0)
    qrow = q0 + _iota((1, tq), 1)
    cur = qrow >> _log2(SLC_BLOCK)
    forced = (srow == 0) | (srow == cur) | (srow == cur - 1)
    bonus = jnp.where(forced, FORCE_BONUS, 0.0)
    past_ok = srow * SLC_BLOCK <= qrow
    kf = kf_ref[...]

    q_augs, o_cs, sels = [], [], []
    for hk in range(NSA_KV_HEADS):
        q_aug = jnp.concatenate([_gather_heads(q_ref, hk) * LOG2E, qsl_ref[hk]], axis=1).astype(BF16)
        kc_aug = jnp.concatenate([kc_ref[0, :, hk * dh:(hk + 1) * dh], kfc_ref[...]], axis=1).astype(BF16)
        vc = kc_ref[0, :, NSA_KV_WIDTH + hk * dh:NSA_KV_WIDTH + (hk + 1) * dh]
        o_c, psum = _cmp_branch_aug(q_aug, kc_aug, vc, valid_c, tq)
        q_augs.append(q_aug)
        o_cs.append(o_c)
        score_t = jnp.where(past_ok, _hdot_nt(cover_t, psum) + bonus, NEG)
        sels.append(jnp.where(_rank_rows(score_t, n_slc) < N_SELECT, 1.0, 0.0).astype(BF16))

    for hk in range(NSA_KV_HEADS):
        def slc_bias(k0, sel_t=sels[hk]):
            kblk = (k0 + _iota((n_slc, KEY_TILE), 1)) >> _log2(SLC_BLOCK)
            expand = jnp.where(_iota((n_slc, KEY_TILE), 0) == kblk, 1.0, 0.0).astype(BF16)
            picked = lax.dot_general(sel_t, expand, (((0,), (0,)), ((), ())), preferred_element_type=F32)
            dist = qpos_i - (k0 + _iota((1, KEY_TILE), 1))
            return jnp.where((picked > 0.5) & (dist >= 0), 0.0, NEG)

        def win_bias(k0):
            dist = qpos_i - (k0 + _iota((1, KEY_TILE), 1))
            return jnp.where((dist >= 0) & (dist < WINDOW), 0.0, NEG)

        o_s = _flash_branch(q_augs[hk], kf, kvs_ref, hk, 0, t_hi, slc_bias, m_scr, acc_scr, tq)
        o_w = _flash_branch(q_augs[hk], kf, kvw_ref, hk, t_lo_w, t_hi, win_bias, m_scr, acc_scr, tq)
        _gate_combine(sm, hk, o_cs[hk], o_s, o_w, o_ref, tq)


def _nsa_prompt(p3, kcvc, *, tq):
    bsz, seq, _ = p3.shape
    n_sub = kcvc.shape[1]
    assert seq % KEY_TILE == 0 and seq % tq == 0 and seq % SLC_BLOCK == 0 and KEY_TILE % tq == 0
    assert seq + CMP_BLOCK < 256 * 256, "positions are split into two bf16-exact parts"
    kern = functools.partial(_nsa_prompt_kernel, tq=tq, seq=seq, n_cmp=seq // CMP_STRIDE - 1)
    rows = NSA_GROUP * tq
    qsl = _slope_features(tq)
    kf = _position_features(jnp.arange(KEY_TILE))
    kfc = _position_features(jnp.arange(n_sub) * CMP_STRIDE + (CMP_BLOCK - 1))
    full = lambda a: pl.BlockSpec(a.shape, lambda b, j: (0,) * a.ndim)
    return pl.pallas_call(
        kern,
        grid=(bsz, seq // tq),
        in_specs=[pl.BlockSpec((1, tq, NSA_WIDTH), lambda b, j: (b, j, C_NQ // NSA_WIDTH)),
                  pl.BlockSpec((1, tq, LANES), lambda b, j: (b, j, C_SMALL // LANES)),
                  pl.BlockSpec((1, seq, KV_COLS), lambda b, j: (b, 0, C_KVS // KV_COLS)),
                  pl.BlockSpec((1, seq, KV_COLS), lambda b, j: (b, 0, C_KVW // KV_COLS)),
                  pl.BlockSpec((1, n_sub, KV_COLS), lambda b, j: (b, 0, 0)),
                  full(qsl), full(kf), full(kfc)],
        out_specs=pl.BlockSpec((1, tq, NSA_WIDTH), lambda b, j: (b, j, 0)),
        out_shape=jax.ShapeDtypeStruct((bsz, seq, NSA_WIDTH), F32),
        scratch_shapes=[pltpu.VMEM((rows, LANES), F32), pltpu.VMEM((rows, 2 * NSA_HEAD_DIM), F32)],
        compiler_params=_cparams(("parallel", "arbitrary")),
        name="nsa_prompt_attention",
    )(p3, p3, p3, p3, kcvc, qsl, kf, kfc)


def _nsa_select_kernel(q_ref, kc_ref, oc_ref, sel_ref, *, tq, past, n_cmp, n_slc, n_slc_pad):
    dh = NSA_HEAD_DIM
    n_sub = kc_ref.shape[1]
    qpos_i = past + _iota((tq, 1), 0)
    qpos = qpos_i.astype(F32)
    cidx = _iota((1, n_sub), 1)
    c_end = cidx * CMP_STRIDE + (CMP_BLOCK - 1)
    valid_c = (c_end <= qpos_i) & (cidx < n_cmp)
    dist_c = qpos - c_end.astype(F32)
    cr = _iota((n_sub, n_slc_pad), 0) * CMP_STRIDE
    s_st = _iota((n_sub, n_slc_pad), 1) * SLC_BLOCK
    cover = jnp.where((cr < s_st + SLC_BLOCK) & (cr + (CMP_BLOCK - 1) >= s_st), 1.0, 0.0)
    sidx = _iota((1, n_slc_pad), 1)
    sidx_f = sidx.astype(F32)
    cur = qpos_i >> _log2(SLC_BLOCK)
    forced = (sidx == 0) | (sidx == cur) | (sidx == cur - 1)
    bonus = jnp.where(forced, FORCE_BONUS, 0.0)
    past_ok = sidx * SLC_BLOCK <= qpos_i
    lane = _iota((1, LANES), 1)
    for hk in range(NSA_KV_HEADS):
        qs = _gather_heads(q_ref, hk)
        kc = kc_ref[0, :, hk * dh:(hk + 1) * dh]
        vc = kc_ref[0, :, NSA_KV_WIDTH + hk * dh:NSA_KV_WIDTH + (hk + 1) * dh]
        o_c, psum = _cmp_branch(qs, kc, vc, hk, valid_c, dist_c, tq)
        oc_ref[0, hk] = o_c
        imp = _hdot(psum, cover)
        score = jnp.where(past_ok, imp + bonus, NEG)
        score = jnp.where(sidx < n_slc, score, -jnp.inf)
        res = jnp.zeros((tq, LANES), F32)
        for it in range(min(N_SELECT, n_slc)):
            m = jnp.max(score, -1, keepdims=True)
            idx = jnp.min(jnp.where(score == m, sidx_f, 1e9), -1, keepdims=True)
            res = jnp.where(lane == it, idx, res)
            score = jnp.where(sidx_f == idx, -jnp.inf, score)
        sel_ref[0, hk] = res.astype(I32)


def _nsa_select(ps3, kcvc, *, past, n_cmp, n_slc):
    bsz, tq, _ = ps3.shape
    n_sub = kcvc.shape[1]
    n_slc_pad = -(-n_slc // LANES) * LANES
    kern = functools.partial(_nsa_select_kernel, tq=tq, past=past, n_cmp=n_cmp, n_slc=n_slc, n_slc_pad=n_slc_pad)
    rows = NSA_GROUP * tq
    return pl.pallas_call(
        kern,
        grid=(bsz,),
        in_specs=[pl.BlockSpec((1, tq, NSA_WIDTH), lambda b: (b, 0, C_NQ // NSA_WIDTH)),
                  pl.BlockSpec((1, n_sub, KV_COLS), lambda b: (b, 0, 0))],
        out_specs=[pl.BlockSpec((1, NSA_KV_HEADS, rows, NSA_HEAD_DIM), lambda b: (b, 0, 0, 0)),
                   pl.BlockSpec((1, NSA_KV_HEADS, tq, LANES), lambda b: (b, 0, 0, 0))],
        out_shape=[jax.ShapeDtypeStruct((bsz, NSA_KV_HEADS, rows, NSA_HEAD_DIM), F32),
                   jax.ShapeDtypeStruct((bsz, NSA_KV_HEADS, tq, LANES), I32)],
        compiler_params=_cparams(("parallel",)),
        name="nsa_sample_select",
    )(ps3, kcvc)


def _joint_softmax_pv(parts, hk, tq):
    g = NSA_GROUP
    outs = []
    for j in range(g):
        rs = slice(j * tq, (j + 1) * tq)
        slope = _slope(hk * g + j)
        ss = [jnp.where(valid, s_all[rs] - slope * dist, NEG) for s_all, valid, dist, _, _ in parts]
        m = None
        for s in ss:
            mi = jnp.max(s, -1, keepdims=True)
            m = mi if m is None else jnp.maximum(m, mi)
        num = None
        den = None
        for s, (_, valid, _, v, v_t) in zip(ss, parts):
            p = jnp.where(valid, jnp.exp(s - m), 0.0)
            d = jnp.sum(p, -1, keepdims=True)
            o = _bdot_nt(p, v) if v_t else _bdot(p, v)
            num = o if num is None else num + o
            den = d if den is None else den + d
        outs.append(num / jnp.maximum(den, 1e-30))
    return jnp.concatenate(outs, axis=0)


def _nsa_sample_kernel(phys_ref, q_ref, sm_ref, kpos_ref, tail_ref, wcache_ref, wnew_ref, oc_ref, cache_ref,
                       o_ref, kbuf, vbuf, sem, *, tq, t_valid, past, n_gather):
    dh = NSA_HEAD_DIM
    b = pl.program_id(0)
    per_b = NSA_KV_HEADS * n_gather

    def page_copies(page, hk, i):
        dst = pl.ds(i * PAGE_SIZE, PAGE_SIZE)
        return (pltpu.make_async_copy(cache_ref.at[page, pl.ds(hk * dh, dh), :], kbuf.at[hk, :, dst], sem),
                pltpu.make_async_copy(cache_ref.at[page, pl.ds(NSA_KV_WIDTH + hk * dh, dh), :], vbuf.at[hk, :, dst], sem))

    for hk in range(NSA_KV_HEADS):
        for i in range(n_gather):
            for cp in page_copies(phys_ref[b * per_b + hk * n_gather + i], hk, i):
                cp.start()

    qpos_i = past + _iota((tq, 1), 0)
    qpos = qpos_i.astype(F32)
    sm = sm_ref[0]
    n_keys = n_gather * PAGE_SIZE
    per_q = n_keys // t_valid
    new_ok = _iota((1, tq), 1) < t_valid
    dist_new = qpos - (past + _iota((1, tq), 1)).astype(F32)
    n_win = wcache_ref.shape[2]
    dist_wc = qpos - (past - n_win + _iota((1, n_win), 1)).astype(F32)
    ok_wc = (dist_wc >= 0.0) & (dist_wc < float(WINDOW))
    ok_wn = (dist_new >= 0.0) & (dist_new < float(WINDOW)) & new_ok

    qss = [_gather_heads(q_ref, hk) for hk in range(NSA_KV_HEADS)]
    win = []
    for hk in range(NSA_KV_HEADS):
        kw_t = wcache_ref[0, hk * dh:(hk + 1) * dh, :]
        vw_t = wcache_ref[0, NSA_KV_WIDTH + hk * dh:NSA_KV_WIDTH + (hk + 1) * dh, :]
        kn = wnew_ref[0, :, hk * dh:(hk + 1) * dh]
        vn = wnew_ref[0, :, NSA_KV_WIDTH + hk * dh:NSA_KV_WIDTH + (hk + 1) * dh]
        win.append(_joint_softmax_pv([(_bdot(qss[hk], kw_t), ok_wc, dist_wc, vw_t, True),
                                      (_bdot_nt(qss[hk], kn), ok_wn, dist_new, vn, False)], hk, tq))

    for hk in range(NSA_KV_HEADS):
        for i in range(n_gather):
            for cp in page_copies(0, hk, i):
                cp.wait()

    g = NSA_GROUP
    grow = _iota((g, g * tq), 0)
    gcol = _iota((g, g * tq), 1)
    gi = _iota((g, 1), 0)
    tcol = _iota((1, tq), 1)
    probs = [(hk, q) for hk in range(NSA_KV_HEADS) for q in range(t_valid)]
    slopes = []
    for hk in range(NSA_KV_HEADS):
        slope = jnp.zeros((g, 1), F32)
        for j in range(g):
            slope = jnp.where(gi == j, _slope(hk * g + j), slope)
        slopes.append(slope)
    kts = [tail_ref[0, :, hk * dh:(hk + 1) * dh] for hk in range(NSA_KV_HEADS)]
    vts = [tail_ref[0, :, NSA_KV_WIDTH + hk * dh:NSA_KV_WIDTH + (hk + 1) * dh] for hk in range(NSA_KV_HEADS)]
    picks = [jnp.where(gcol == grow * tq + q, 1.0, 0.0) for _, q in probs]
    q8s = [_hdot(pick, qss[hk]) for pick, (hk, _) in zip(picks, probs)]
    raw_p = [_bdot(q8, kbuf[hk, :, q * per_q:(q + 1) * per_q]) for q8, (hk, q) in zip(q8s, probs)]
    raw_t = [_bdot_nt(q8, kts[hk]) for q8, (hk, _) in zip(q8s, probs)]
    pps, pts, dens = [], [], []
    for (hk, q), rp, rt in zip(probs, raw_p, raw_t):
        dist_p = float(past + q) - kpos_ref[0, hk][:, q * per_q:(q + 1) * per_q]
        dist_t = (q - tcol).astype(F32)
        ok_t = (dist_t >= 0.0) & new_ok
        s_p = jnp.where(dist_p >= 0.0, rp - slopes[hk] * dist_p, NEG)
        s_t = jnp.where(ok_t, rt - slopes[hk] * dist_t, NEG)
        m = jnp.maximum(jnp.max(s_p, -1, keepdims=True), jnp.max(s_t, -1, keepdims=True))
        p_p = jnp.where(dist_p >= 0.0, jnp.exp(s_p - m), 0.0)
        p_t = jnp.where(ok_t, jnp.exp(s_t - m), 0.0)
        pps.append(p_p)
        pts.append(p_t)
        dens.append(jnp.sum(p_p, -1, keepdims=True) + jnp.sum(p_t, -1, keepdims=True))
    o_qs = [(_bdot_nt(p_p, vbuf[hk, :, q * per_q:(q + 1) * per_q]) + _bdot(p_t, vts[hk])) / jnp.maximum(den, 1e-30)
            for (hk, q), p_p, p_t, den in zip(probs, pps, pts, dens)]
    backs = [_hdot_tn(pick, o_q) for pick, o_q in zip(picks, o_qs)]
    for hk in range(NSA_KV_HEADS):
        o_s = backs[hk * t_valid]
        for q in range(1, t_valid):
            o_s = o_s + backs[hk * t_valid + q]
        _gate_combine(sm, hk, oc_ref[0, hk], o_s, win[hk], o_ref, tq)


def _nsa_sample(ps3, o_c, phys, kpos, cache_t, win_t, *, t_valid, past):
    bsz, tq, _ = ps3.shape
    n_gather = t_valid * N_SELECT
    rows = NSA_GROUP * tq
    n_keys = n_gather * PAGE_SIZE
    kern = functools.partial(_nsa_sample_kernel, tq=tq, t_valid=t_valid, past=past, n_gather=n_gather)
    return pl.pallas_call(
        kern,
        grid_spec=pltpu.PrefetchScalarGridSpec(
            num_scalar_prefetch=1,
            grid=(bsz,),
            in_specs=[pl.BlockSpec((1, tq, NSA_WIDTH), lambda b, ph: (b, 0, C_NQ // NSA_WIDTH)),
                      pl.BlockSpec((1, tq, LANES), lambda b, ph: (b, 0, C_SMALL // LANES)),
                      pl.BlockSpec((1, NSA_KV_HEADS, 1, n_keys), lambda b, ph: (b, 0, 0, 0)),
                      pl.BlockSpec((1, tq, KV_COLS), lambda b, ph: (b, 0, C_KVS // KV_COLS)),
                      pl.BlockSpec((1,) + win_t.shape[1:], lambda b, ph: (b, 0, 0)),
                      pl.BlockSpec((1, tq, KV_COLS), lambda b, ph: (b, 0, C_KVW // KV_COLS)),
                      pl.BlockSpec((1, NSA_KV_HEADS, rows, NSA_HEAD_DIM), lambda b, ph: (b, 0, 0, 0)),
                      pl.BlockSpec(memory_space=pl.ANY)],
            out_specs=pl.BlockSpec((1, tq, NSA_WIDTH), lambda b, ph: (b, 0, 0)),
            scratch_shapes=[pltpu.VMEM((NSA_KV_HEADS, NSA_HEAD_DIM, n_keys), F32),
                            pltpu.VMEM((NSA_KV_HEADS, NSA_HEAD_DIM, n_keys), F32),
                            pltpu.SemaphoreType.DMA(())]),
        out_shape=jax.ShapeDtypeStruct((bsz, tq, NSA_WIDTH), F32),
        compiler_params=_cparams(("arbitrary",)),
        name="nsa_sample_attention",
    )(phys, ps3, ps3, kpos, ps3, win_t, ps3, o_c, cache_t)


def _rows_transposed(cache):
    nd = cache.ndim
    perm = tuple(range(nd - 4)) + (nd - 3, nd - 2, nd - 1, nd - 4)
    t = jnp.transpose(cache, perm)
    return t.reshape(t.shape[:nd - 4] + (KV_COLS, cache.shape[nd - 4]))


def _prompt_mixers(x, w_r, conv_w, a_log, dt_bias, norm_w, cmp_wf, cmp_w2bd, cmp_pos, cmp_w1, tl):
    bsz, seq, _ = x.shape
    p = _matmul(x.reshape(bsz * seq, D_MODEL), w_r, tl["proj_tm"], tl["proj_tn"])
    p3 = p.reshape(bsz, seq, P_COLS)
    hist = jnp.zeros((bsz, 8, 3 * DN_WIDTH), F32)
    s0 = jnp.zeros((bsz, DN_HEADS, DN_HEAD_DIM, DN_HEAD_DIM), F32)
    o_dn, s_new = _deltanet(p3, hist, s0, conv_w, a_log, dt_bias, norm_w, t_valid=seq, tb=tl["dn_tb"], c=DN_CHUNK,
                            hp=tl["dn_heads"])
    kvc = p3[:, :, C_KVC:C_KVC + KV_COLS]
    n_sub = seq // CMP_STRIDE
    sub = kvc.reshape(bsz * n_sub, CMP_STRIDE * KV_COLS)
    pmat = _matmul(sub, cmp_wf, min(256, bsz * n_sub), cmp_wf.shape[1]).reshape(bsz, n_sub, -1)
    kcvc = _cmp_epilogue(pmat, cmp_pos.reshape(2, -1), cmp_w1, cmp_w2bd)
    o_nsa = _nsa_prompt(p3, kcvc, tq=tl["nsa_tq"])
    return p3, o_dn, s_new, o_nsa


def _sample_mixers(x, cache_cmp, cache_slc, win_buf, s0, conv_buf, page_table, w_r, conv_w, a_log, dt_bias,
                   norm_w, cmp_wf, cmp_w2bd, cmp_pos, cmp_w1, tl):
    bsz, t, _ = x.shape
    tq = 8
    n_pages = page_table.shape[1]
    past = n_pages * PAGE_SIZE
    assert t <= tq and t <= SLC_BLOCK and past % SLC_BLOCK == 0 and cache_cmp.shape[1] == PAGE_SIZE
    assert (past + t) // CMP_STRIDE * CMP_STRIDE == past, "new rows never complete a compression sub-block"
    ps = _matmul(x.reshape(bsz * t, D_MODEL), w_r, bsz * t, tl["proj_tn"]).reshape(bsz, t, P_COLS)
    ps3 = jnp.pad(ps, ((0, 0), (0, tq - t), (0, 0)))
    hist = jnp.pad(conv_buf, ((0, 0), (8 - (CONV_W - 1), 0), (0, 0)))
    o_dn, s_new = _deltanet(ps3, hist, s0, conv_w, a_log, dt_bias, norm_w, t_valid=t, tb=tq, c=tq, hp=DN_HEADS)
    n_sub = past // CMP_STRIDE
    pmat = _cmp_paged(_rows_transposed(cache_cmp), page_table, cmp_wf, npg=tl["cmp_pages"])
    kcvc = _cmp_epilogue(pmat.reshape(bsz, n_sub, -1), cmp_pos.reshape(2, -1), cmp_w1, cmp_w2bd)
    n_past_blocks = past // SLC_BLOCK
    o_c, sel = _nsa_select(ps3, kcvc, past=past, n_cmp=n_sub - 1, n_slc=n_past_blocks + 1)
    sel = sel[:, :, :t, :N_SELECT]
    bpp = PAGE_SIZE // SLC_BLOCK
    jp = jnp.minimum(sel, n_past_blocks - 1)
    page = jp // bpp
    phys = page_table[jnp.arange(bsz)[:, None, None, None], page]
    row = jnp.arange(PAGE_SIZE)
    in_blk = (row // SLC_BLOCK == (jp % bpp)[..., None]) & (sel < n_past_blocks)[..., None]
    kpos = jnp.where(in_blk, (page[..., None] * PAGE_SIZE + row).astype(F32), 1e9)
    kpos = kpos.reshape(bsz, NSA_KV_HEADS, 1, t * N_SELECT * PAGE_SIZE)
    o_nsa = _nsa_sample(ps3, o_c, phys.reshape(-1).astype(I32), kpos, _rows_transposed(cache_slc),
                        _rows_transposed(win_buf), t_valid=t, past=past)
    return ps, o_dn, s_new, o_nsa


def _layer_norm(x, g, b):
    xc = x - jnp.mean(x, -1, keepdims=True)
    var = jnp.mean(xc * xc, -1, keepdims=True)
    return xc * lax.rsqrt(var + LN_EPS) * g + b


def _rank_rows(v, n):
    ri = _iota(v.shape, 0)
    rank = jnp.zeros(v.shape, F32)
    for rp in range(n):
        row = v[rp:rp + 1, :]
        beats = (row > v) | ((row == v) & (rp < ri))
        rank = rank + jnp.where(beats, 1.0, 0.0)
    return rank


def _post_mixer_kernel(x_ref, odn_ref, onsa_ref, gdn_ref, gnsa_ref, wo_ref, g_ref, b_ref, wr_ref, br_ref,
                       x1_ref, xp_ref, idx_ref, wt_ref, pos_ref, cnt_ref, run_scr, *, tm, alpha):
    i = pl.program_id(0)

    @pl.when(i == 0)
    def _():
        run_scr[...] = jnp.zeros(run_scr.shape, F32)

    h = _sigmoid(gdn_ref[0]) * odn_ref[...] + _sigmoid(gnsa_ref[0]) * onsa_ref[...]
    x1 = _layer_norm(alpha * x_ref[...] + _bdot(h, wo_ref[...]), g_ref[...], b_ref[...])
    x1_ref[...] = x1
    bits = pltpu.bitcast(x1.astype(BF16).astype(F32), jnp.uint32)
    half = x1.shape[1] // 2
    xp_ref[...] = (bits[:, :half] >> 16) | (bits[:, half:] & jnp.uint32(0xFFFF0000))

    ne = N_EXPERTS
    per = ne // N_GROUPS
    scores = _sigmoid(_hdot_nt(wr_ref[...], x1))
    s3 = (scores + br_ref[...]).reshape(N_GROUPS, per, tm)
    e3 = _iota((N_GROUPS, per, tm), 1).astype(F32)
    g1 = jnp.max(s3, axis=1, keepdims=True)
    first = jnp.min(jnp.where(s3 == g1, e3, float(per)), axis=1, keepdims=True)
    g2 = jnp.max(jnp.where(e3 == first, -jnp.inf, s3), axis=1, keepdims=True)
    grank = _rank_rows((g1 + g2).reshape(N_GROUPS, tm), N_GROUPS)
    keep = (grank < TOPK_GROUPS).reshape(N_GROUPS, 1, tm)
    selm = jnp.where(keep, s3, NEG).reshape(ne, tm)
    erank = _rank_rows(selm, ne)
    ei = _iota((ne, tm), 0).astype(F32)
    chosen = jnp.where(erank < TOP_K, 1.0, 0.0)
    tr = _iota((tm, tm), 0)
    tc = _iota((tm, tm), 1)
    before = jnp.where(tr < tc, 1.0, 0.0)
    pos_full = _bdot(chosen, before) + run_scr[:, 0:1]
    idx_rows, w_rows, pos_rows = [], [], []
    for k in range(TOP_K):
        hit = erank == float(k)
        idx_rows.append(jnp.sum(jnp.where(hit, ei, 0.0), 0, keepdims=True))
        w_rows.append(jnp.sum(jnp.where(hit, scores, 0.0), 0, keepdims=True))
        pos_rows.append(jnp.sum(jnp.where(hit, pos_full, 0.0), 0, keepdims=True))
    wsum = w_rows[0]
    for k in range(1, TOP_K):
        wsum = wsum + w_rows[k]
    zero = jnp.zeros((8 - TOP_K, tm), F32)
    idx_ref[...] = jnp.concatenate(idx_rows + [zero], 0).astype(I32)
    wt_ref[...] = jnp.concatenate([w / wsum * ROUTED_SCALE for w in w_rows] + [zero], 0)
    pos_ref[...] = jnp.concatenate(pos_rows + [zero], 0).astype(I32)
    run_scr[...] = run_scr[...] + jnp.sum(chosen, 1, keepdims=True)
    cnt_ref[...] = run_scr[...]


def _post_mixer(x, o_dn, o_nsa, p3, w_out_bf16, ln_g, ln_b, w_router_t, b_router, *, tm, alpha):
    n, d = x.shape
    assert n % tm == 0
    bsz, seq, _ = p3.shape
    assert seq % tm == 0 or tm % seq == 0
    if seq % tm == 0:
        per_b = seq // tm
        gspec = lambda c: pl.BlockSpec((1, tm, d), lambda i: (i // per_b, i % per_b, c))
        p_in = p3
    else:
        p_in = p3.reshape(1, n, P_COLS)
        gspec = lambda c: pl.BlockSpec((1, tm, d), lambda i: (0, i, c))
    tok = pl.BlockSpec((tm, d), lambda i: (i, 0))
    full = lambda a: pl.BlockSpec(a.shape, lambda i: (0,) * a.ndim)
    rt = pl.BlockSpec((8, tm), lambda i: (0, i))
    kern = functools.partial(_post_mixer_kernel, tm=tm, alpha=alpha)
    g2 = ln_g.reshape(1, d)
    b2 = ln_b.reshape(1, d)
    br = b_router.reshape(N_EXPERTS, 1)
    return pl.pallas_call(
        kern,
        grid=(n // tm,),
        in_specs=[tok, tok, tok, gspec(C_MG // d), gspec(C_MG // d + 1), full(w_out_bf16), full(g2), full(b2),
                  full(w_router_t), full(br)],
        out_specs=[tok, pl.BlockSpec((tm, d // 2), lambda i: (i, 0)), rt, rt, rt,
                   pl.BlockSpec((N_EXPERTS, LANES), lambda i: (0, 0))],
        out_shape=[jax.ShapeDtypeStruct((n, d), F32), jax.ShapeDtypeStruct((n, d // 2), jnp.uint32),
                   jax.ShapeDtypeStruct((8, n), I32),
                   jax.ShapeDtypeStruct((8, n), F32), jax.ShapeDtypeStruct((8, n), I32),
                   jax.ShapeDtypeStruct((N_EXPERTS, LANES), F32)],
        scratch_shapes=[pltpu.VMEM((N_EXPERTS, LANES), F32)],
        compiler_params=_cparams(("arbitrary",)),
        name="merge_outproj_ln_router",
    )(x, o_dn, o_nsa, p_in, p_in, w_out_bf16, g2, b2, w_router_t, br)


def _slot_kernel(ps_ref, idx_ref, pos_ref, slot_ref):
    idx = idx_ref[...]
    acc = pos_ref[...]
    for e in range(N_EXPERTS):
        acc = acc + jnp.where(idx == e, ps_ref[e], 0)
    slot_ref[...] = jnp.where(_iota(idx.shape, 0) < TOP_K, acc, 0)


def _slots(pad_start, idx, pos):
    n = idx.shape[1]
    blk = pl.BlockSpec((8, n), lambda i, ps: (0, 0))
    return pl.pallas_call(
        _slot_kernel,
        grid_spec=pltpu.PrefetchScalarGridSpec(num_scalar_prefetch=1, grid=(1,), in_specs=[blk, blk], out_specs=blk),
        out_shape=jax.ShapeDtypeStruct((8, n), I32),
        compiler_params=_cparams(("arbitrary",)),
        name="moe_slots",
    )(pad_start, idx, pos)


def _dispatch_kernel(slot_ref, x_ref, xs_in_ref, xs_ref, sem, *, tm):
    del xs_in_ref

    def row_copy(r, s):
        return pltpu.make_async_copy(x_ref.at[pl.ds(r, 1)], xs_ref.at[pl.ds(s, 1)], sem)

    def issue(r, carry):
        for k in range(TOP_K):
            row_copy(r, slot_ref[k, r]).start()
        return carry

    lax.fori_loop(0, tm, issue, 0, unroll=8)

    def drain(r, carry):
        for k in range(TOP_K):
            row_copy(0, 0).wait()
        return carry

    lax.fori_loop(0, tm, drain, 0, unroll=8)


def _dispatch(x1, slot, n_slots, *, tm):
    n, d = x1.shape
    assert n % tm == 0
    kern = functools.partial(_dispatch_kernel, tm=tm)
    xs0 = jnp.zeros((n_slots, d), x1.dtype)
    return pl.pallas_call(
        kern,
        grid=(n // tm,),
        in_specs=[pl.BlockSpec((8, tm), lambda i: (0, i), memory_space=pltpu.SMEM),
                  pl.BlockSpec((tm, d), lambda i: (i, 0)),
                  pl.BlockSpec(memory_space=pl.ANY)],
        out_specs=pl.BlockSpec(memory_space=pl.ANY),
        out_shape=jax.ShapeDtypeStruct((n_slots, d), x1.dtype),
        scratch_shapes=[pltpu.SemaphoreType.DMA(())],
        input_output_aliases={2: 0},
        compiler_params=_cparams(("arbitrary",)),
        name="moe_dispatch",
    )(slot, x1, xs0)


def _expert_kernel(be_ref, nu_ref, x_ref, wg_ref, wu_ref, wd_ref, y_ref, wg_b, wu_b, wd_b):
    i = pl.program_id(0)

    @pl.when((i == 0) | (be_ref[i] != be_ref[jnp.maximum(i - 1, 0)]))
    def _():
        wg_b[...] = wg_ref[0].astype(BF16)
        wu_b[...] = wu_ref[0].astype(BF16)
        wd_b[...] = wd_ref[0].astype(BF16)

    @pl.when(i < nu_ref[0])
    def _():
        w = x_ref[...]
        x = jnp.concatenate([pltpu.bitcast(w << 16, F32), pltpu.bitcast(w & jnp.uint32(0xFFFF0000), F32)],
                            axis=1).astype(BF16)
        hg = jnp.dot(x, wg_b[...], preferred_element_type=F32)
        hu = jnp.dot(x, wu_b[...], preferred_element_type=F32)
        y_ref[...] = jnp.dot((_silu(hg) * hu).astype(BF16), wd_b[...], preferred_element_type=F32)

    @pl.when(i >= nu_ref[0])
    def _():
        y_ref[...] = jnp.zeros(y_ref.shape, F32)


def _experts(xs, blk_exp, n_used, w_gate, w_up, w_down, *, blk):
    n_slots, dpk = xs.shape
    d, de = w_gate.shape[1:]
    assert dpk * 2 == d
    n_blocks = n_slots // blk
    return pl.pallas_call(
        _expert_kernel,
        grid_spec=pltpu.PrefetchScalarGridSpec(
            num_scalar_prefetch=2,
            grid=(n_blocks,),
            in_specs=[pl.BlockSpec((blk, dpk), lambda i, be, nu: (jnp.maximum(jnp.minimum(i, nu[0] - 1), 0), 0)),
                      pl.BlockSpec((1, d, de), lambda i, be, nu: (be[i], 0, 0)),
                      pl.BlockSpec((1, d, de), lambda i, be, nu: (be[i], 0, 0)),
                      pl.BlockSpec((1, de, d), lambda i, be, nu: (be[i], 0, 0))],
            out_specs=pl.BlockSpec((blk, d), lambda i, be, nu: (i, 0)),
            scratch_shapes=[pltpu.VMEM((d, de), BF16), pltpu.VMEM((d, de), BF16), pltpu.VMEM((de, d), BF16)]),
        out_shape=jax.ShapeDtypeStruct((n_slots, d), F32),
        compiler_params=_cparams(("arbitrary",)),
        name="moe_experts",
    )(blk_exp, n_used, xs, w_gate, w_up, w_down)


def _combine_kernel(slot_ref, nslot_ref, x_ref, w_ref, ys_ref, wsg_ref, wsu_ref, wsd_ref, g_ref, b_ref, o_ref,
                    buf, sem, *, tm, alpha):
    i = pl.program_id(0)
    n = pl.num_programs(0)

    def row_copy(s, half, k, r):
        return pltpu.make_async_copy(ys_ref.at[pl.ds(s, 1)], buf.at[half, k, pl.ds(r, 1)], sem.at[half])

    def issue(slots, half):
        def body(r, carry):
            for k in range(TOP_K):
                row_copy(slots[k, r], half, k, r).start()
            return carry

        lax.fori_loop(0, tm, body, 0, unroll=8)

    @pl.when(i == 0)
    def _():
        issue(slot_ref, 0)

    @pl.when(i + 1 < n)
    def _():
        issue(nslot_ref, (i + 1) % 2)

    x = x_ref[...]
    xb = x.astype(BF16)
    hs = _silu(jnp.dot(xb, wsg_ref[...], preferred_element_type=F32)) * jnp.dot(xb, wsu_ref[...],
                                                                               preferred_element_type=F32)
    acc = alpha * x + _bdot(hs, wsd_ref[...])
    half = i % 2

    def drain(r, carry):
        for k in range(TOP_K):
            row_copy(0, half, k, 0).wait()
        return carry

    lax.fori_loop(0, tm, drain, 0, unroll=8)
    w = w_ref[...]
    for k in range(TOP_K):
        acc = acc + w[:, k:k + 1] * buf[half, k]
    o_ref[...] = _layer_norm(acc, g_ref[...], b_ref[...])


def _combine(x1, slot, w_tok, ys, ws_gate, ws_up, ws_down, ln_g, ln_b, *, tm, alpha):
    n, d = x1.shape
    assert n % tm == 0
    kern = functools.partial(_combine_kernel, tm=tm, alpha=alpha)
    full = lambda a: pl.BlockSpec(a.shape, lambda i: (0,) * a.ndim)
    g2 = ln_g.reshape(1, d)
    b2 = ln_b.reshape(1, d)
    last = n // tm - 1
    return pl.pallas_call(
        kern,
        grid=(n // tm,),
        in_specs=[pl.BlockSpec((8, tm), lambda i: (0, i), memory_space=pltpu.SMEM),
                  pl.BlockSpec((8, tm), lambda i: (0, jnp.minimum(i + 1, last)), memory_space=pltpu.SMEM),
                  pl.BlockSpec((tm, d), lambda i: (i, 0)),
                  pl.BlockSpec((tm, 8), lambda i: (i, 0)),
                  pl.BlockSpec(memory_space=pl.ANY),
                  full(ws_gate), full(ws_up), full(ws_down), full(g2), full(b2)],
        out_specs=pl.BlockSpec((tm, d), lambda i: (i, 0)),
        out_shape=jax.ShapeDtypeStruct((n, d), F32),
        scratch_shapes=[pltpu.VMEM((2, TOP_K, tm, d), F32), pltpu.SemaphoreType.DMA((2,))],
        compiler_params=_cparams(("arbitrary",)),
        name="moe_combine_ln",
    )(slot, slot, x1, w_tok, ys, ws_gate, ws_up, ws_down, g2, b2)


def _moe_layer(x1, xp, idx, wts, pos, counts, w_gate, w_up, w_down, ws_gate, ws_up, ws_down, ln_g, ln_b,
               *, blk, tm_d, tm_c, alpha):
    n = x1.shape[0]
    cnt = counts[:, 0].astype(I32)
    padded = (cnt + blk - 1) // blk * blk
    pad_end = jnp.cumsum(padded)
    slot = _slots((pad_end - padded).astype(I32), idx, pos)
    n_blocks = -(-(n * TOP_K) // blk) + N_EXPERTS
    blk_exp = jnp.minimum(jnp.sum(pad_end[None, :] <= (jnp.arange(n_blocks) * blk)[:, None], axis=1),
                          N_EXPERTS - 1).astype(I32)
    n_used = (pad_end[-1:] // blk).astype(I32)
    xs = _dispatch(xp, slot, n_blocks * blk, tm=tm_d)
    ys = _experts(xs, blk_exp, n_used, w_gate, w_up, w_down, blk=blk)
    return _combine(x1, slot, wts.T, ys, ws_gate.astype(BF16), ws_up.astype(BF16), ws_down.astype(BF16),
                    ln_g, ln_b, tm=tm_c, alpha=alpha)


def kernel(x_prompt, x_sample, cache_cmp_kv, cache_slc_kv, cache_win_kv, state_delta_S, state_delta_conv, page_table, w_in, dn_conv_w, dn_A_log, dn_dt_bias, dn_norm_w, nsa_cmp_w1, nsa_cmp_pos, nsa_cmp_w2, w_out, ln1_g, ln1_b, w_router, b_router, w_exp_gate, w_exp_up, w_exp_down, w_sh_gate, w_sh_up, w_sh_down, ln2_g, ln2_b):
    depth = w_in.shape[0]
    assert depth == 1
    alpha = (2.0 * depth) ** 0.25
    bsz, seq, d = x_prompt.shape
    sb, st, _ = x_sample.shape
    tl = _tiles(bsz * seq, seq, sb * st)
    w_r = _reorder_w_in(w_in[0])
    wf, w2bd = _cmp_weights(nsa_cmp_w1[0], nsa_cmp_w2[0])
    mix_w = (w_r, dn_conv_w[0], dn_A_log[0], dn_dt_bias[0], dn_norm_w[0], wf, w2bd, nsa_cmp_pos[0], nsa_cmp_w1[0])
    p3, o_dn, s_p, o_nsa = _prompt_mixers(x_prompt, *mix_w, tl)
    ps, o_dn_s, s_s, o_nsa_s = _sample_mixers(x_sample, cache_cmp_kv[0], cache_slc_kv[0], cache_win_kv[0],
                                              state_delta_S[0], state_delta_conv[0], page_table, *mix_w, tl)
    wo = w_out[0].astype(BF16)
    wrt = w_router[0].T

    def ffn(x2, o_dn2, o_nsa2, p_any, tm, blk, tm_d, tm_c):
        x1, xp, idx, wts, pos, counts = _post_mixer(x2, o_dn2, o_nsa2, p_any, wo, ln1_g[0], ln1_b[0], wrt,
                                                    b_router[0], tm=tm, alpha=alpha)
        return _moe_layer(x1, xp, idx, wts, pos, counts, w_exp_gate[0], w_exp_up[0], w_exp_down[0],
                          w_sh_gate[0], w_sh_up[0], w_sh_down[0], ln2_g[0], ln2_b[0],
                          blk=blk, tm_d=tm_d, tm_c=tm_c, alpha=alpha)

    y_p = ffn(x_prompt.reshape(-1, d), o_dn.reshape(-1, d), o_nsa.reshape(-1, d), p3,
              tl["post_tm"], tl["moe_blk"], tl["moe_tm_dispatch"], tl["moe_tm_combine"])
    y_s = ffn(x_sample.reshape(-1, d), o_dn_s[:, :st].reshape(-1, d), o_nsa_s[:, :st].reshape(-1, d), ps,
              tl["sample_tm"], tl["sample_moe_blk"], tl["sample_tm"], tl["sample_tm"])

    kv_shape = (2, NSA_KV_HEADS, NSA_HEAD_DIM)

    def kv_rows(pp, c0):
        return pp[:, :, c0:c0 + KV_COLS].reshape(pp.shape[:2] + kv_shape)

    nconv = CONV_W - 1
    conv_p = jnp.concatenate([jnp.zeros((bsz, nconv, 3 * DN_WIDTH), F32), p3[:, :, :3 * DN_WIDTH]], 1)[:, -nconv:]
    conv_s = jnp.concatenate([state_delta_conv[0], ps[:, :, :3 * DN_WIDTH]], 1)[:, -nconv:]
    past = page_table.shape[1] * PAGE_SIZE
    win_s = jnp.concatenate([cache_win_kv[0], kv_rows(ps, C_KVW)], 1)[:, -min(WINDOW, past + st):]
    return (y_p.reshape(x_prompt.shape), y_s.reshape(x_sample.shape),
            kv_rows(p3, C_KVC)[None], kv_rows(p3, C_KVS)[None], kv_rows(p3, C_KVW)[:, -min(WINDOW, seq):][None],
            s_p[None], conv_p[None],
            kv_rows(ps, C_KVC)[None], kv_rows(ps, C_KVS)[None], win_s[None], s_s[None], conv_s[None])
```

```python
import functools
import math

import jax
import jax.numpy as jnp
import numpy as np
from jax import lax
from jax.experimental import pallas as pl
from jax.experimental.pallas import tpu as pltpu

F32 = jnp.float32
BF16 = jnp.bfloat16
I32 = jnp.int32
HIGHEST = lax.Precision.HIGHEST

D_MODEL = 1024
PAGE_SIZE = 128
DN_HEADS = 8
DN_HEAD_DIM = 128
DN_WIDTH = DN_HEADS * DN_HEAD_DIM
CONV_W = 4
DN_CHUNK = 64
NSA_HEADS = 16
NSA_KV_HEADS = 2
NSA_GROUP = NSA_HEADS // NSA_KV_HEADS
NSA_HEAD_DIM = 64
NSA_WIDTH = NSA_HEADS * NSA_HEAD_DIM
NSA_KV_WIDTH = NSA_KV_HEADS * NSA_HEAD_DIM
KV_COLS = 2 * NSA_KV_WIDTH
CMP_BLOCK = 32
CMP_STRIDE = 16
SLC_BLOCK = 64
N_SELECT = 16
WINDOW = 512
N_EXPERTS = 64
TOP_K = 6
N_GROUPS = 8
TOPK_GROUPS = 4
ROUTED_SCALE = 2.5
LN_EPS = 1e-5
RMS_EPS = 1e-6
NEG = -1e30
LOG2E = math.log2(math.e)
FORCE_BONUS = 1e6

C_QKV = 0
C_Z = 3072
C_NQ = 4096
C_MG = 5120
C_KVC = 7168
C_KVS = 7424
C_KVW = 7680
C_SMALL = 7936
P_COLS = 8064
SM_A = 0
SM_B = DN_HEADS
SM_NG = 2 * DN_HEADS

LANES = 128
VMEM_LIMIT = 48 * 1024 * 1024
KEY_TILE = 256


def _tiles(n_prompt_tokens, seq, n_sample_tokens):
    return dict(
        proj_tm=min(512, n_prompt_tokens), proj_tn=P_COLS // 3,
        dn_tb=min(256, seq), dn_heads=8,
        nsa_tq=256,
        cmp_pages=32,
        post_tm=min(256, n_prompt_tokens),
        moe_blk=512, moe_tm_dispatch=min(256, n_prompt_tokens), moe_tm_combine=min(128, n_prompt_tokens),
        sample_moe_blk=64, sample_tm=n_sample_tokens,
    )


def _cparams(sem):
    return pltpu.CompilerParams(dimension_semantics=sem, vmem_limit_bytes=VMEM_LIMIT)


def _bdot(a, b):
    return jnp.dot(a.astype(BF16), b.astype(BF16), preferred_element_type=F32)


def _bdot_nt(a, b):
    return lax.dot_general(a.astype(BF16), b.astype(BF16), (((1,), (1,)), ((), ())),
                           preferred_element_type=F32)


def _bdot_tn(a, b):
    return lax.dot_general(a.astype(BF16), b.astype(BF16), (((0,), (0,)), ((), ())),
                           preferred_element_type=F32)


def _hdot(a, b):
    return jnp.dot(a, b, precision=HIGHEST, preferred_element_type=F32)


def _hdot_tn(a, b):
    return lax.dot_general(a, b, (((0,), (0,)), ((), ())), precision=HIGHEST,
                           preferred_element_type=F32)


def _hdot_nt(a, b):
    return lax.dot_general(a, b, (((1,), (1,)), ((), ())), precision=HIGHEST,
                           preferred_element_type=F32)


def _sigmoid(x):
    return 1.0 / (1.0 + jnp.exp(-x))


def _silu(x):
    return x * _sigmoid(x)


def _softplus(x):
    return jnp.maximum(x, 0.0) + jnp.log(1.0 + jnp.exp(-jnp.abs(x)))


def _iota(shape, dim):
    return lax.broadcasted_iota(I32, shape, dim)


def _log2(n):
    assert n & (n - 1) == 0
    return int(math.log2(n))


def _mm_kernel(x_ref, w_ref, o_ref):
    o_ref[...] = jnp.dot(x_ref[...].astype(BF16), w_ref[...], preferred_element_type=F32)


def _matmul(x, w_bf16, tm, tn):
    m, k = x.shape
    n = w_bf16.shape[1]
    assert m % tm == 0 and n % tn == 0
    return pl.pallas_call(
        _mm_kernel,
        grid=(n // tn, m // tm),
        in_specs=[pl.BlockSpec((tm, k), lambda j, i: (i, 0)),
                  pl.BlockSpec((k, tn), lambda j, i: (0, j))],
        out_specs=pl.BlockSpec((tm, tn), lambda j, i: (i, j)),
        out_shape=jax.ShapeDtypeStruct((m, n), F32),
        compiler_params=_cparams(("parallel", "parallel")),
        name="dense_matmul",
    )(x, w_bf16)


def _reorder_w_in(w_in):
    o = 0
    seg = {}
    for name, size in (("qkv", 3 * DN_WIDTH), ("z", DN_WIDTH), ("a", DN_HEADS), ("b", DN_HEADS),
                       ("nq", NSA_WIDTH), ("kvc", KV_COLS), ("kvs", KV_COLS),
                       ("kvw", KV_COLS), ("ng", 3 * NSA_HEADS), ("mg", 2 * D_MODEL)):
        seg[name] = w_in[:, o:o + size]
        o += size
    assert o == w_in.shape[1]
    pad = jnp.zeros((w_in.shape[0], P_COLS - C_SMALL - SM_NG - 3 * NSA_HEADS), w_in.dtype)
    w = jnp.concatenate([seg["qkv"], seg["z"], seg["nq"], seg["mg"], seg["kvc"], seg["kvs"], seg["kvw"],
                         seg["a"], seg["b"], seg["ng"], pad], axis=1)
    assert w.shape[1] == P_COLS
    return w.astype(BF16)


def _tri_inverse(lmats, c):
    r = _iota((c, c), 0)
    q = _iota((c, c), 1)
    eye = (r == q).astype(F32)
    blk = min(16, c)
    shift = _log2(blk)
    same = (r >> shift) == (q >> shift)
    dmats = [jnp.where(same, lm, 0.0) for lm in lmats]
    prods = [eye - dm for dm in dmats]
    dpows = dmats
    k = 2
    while k < blk:
        dpows = [_bdot(dp, dp) for dp in dpows]
        prods = [pr + _bdot(pr, dp) for pr, dp in zip(prods, dpows)]
        k *= 2
    if c == blk:
        return prods
    mmats = [_bdot(pr, lm - dm) for pr, lm, dm in zip(prods, lmats, dmats)]
    outers = [eye - mm for mm in mmats]
    mpows = mmats
    k = 2
    while k < c // blk:
        mpows = [_bdot(mp, mp) for mp in mpows]
        outers = [ou + _bdot(ou, mp) for ou, mp in zip(outers, mpows)]
        k *= 2
    return [_bdot(ou, pr) for ou, pr in zip(outers, prods)]


def _dn_kernel(q_ref, k_ref, v_ref, z_ref, sm_ref, hq_ref, hk_ref, hv_ref, cwq_ref, cwk_ref, cwv_ref,
               hp_ref, nw_ref, s0_ref, o_ref, sout_ref,
               s_scr, xp_scr, qn_scr, kn_scr, vn_scr, gb_scr, u_scr, w_scr, qe_scr, kd_scr, a_scr, eg_scr,
               *, tb, c, t_valid, hp):
    hb = pl.program_id(1)
    t = pl.program_id(2)
    nt = pl.num_programs(2)
    dk = DN_HEAD_DIM
    nc = tb // c

    @pl.when(t == 0)
    def _():
        s_scr[...] = s0_ref[0]
        xp_scr[0, 0:8, :] = hq_ref[0]
        xp_scr[1, 0:8, :] = hk_ref[0]
        xp_scr[2, 0:8, :] = hv_ref[0]

    rows = t * tb + _iota((tb, 1), 0)
    valid = rows < t_valid

    def conv(i, raw_ref, cw_ref):
        xp_scr[i, 8:8 + tb, :] = raw_ref[0]
        acc = xp_scr[i, 8:8 + tb, :] * cw_ref[CONV_W - 1:CONV_W, :]
        for j in range(CONV_W - 1):
            acc = acc + xp_scr[i, 8 - (CONV_W - 1) + j:8 - (CONV_W - 1) + j + tb, :] * cw_ref[j:j + 1, :]
        tail = xp_scr[i, tb:tb + 8, :]
        xp_scr[i, 0:8, :] = tail
        return _silu(acc)

    qc = conv(0, q_ref, cwq_ref)
    kc = conv(1, k_ref, cwk_ref)
    vc = conv(2, v_ref, cwv_ref)
    lane = _iota((1, LANES), 1)
    sm = sm_ref[0]
    for hh in range(hp):
        h = hb * hp + hh
        cs = slice(hh * dk, (hh + 1) * dk)
        qh = qc[:, cs]
        kh = kc[:, cs]
        qn = qh * lax.rsqrt(jnp.sum(qh * qh, -1, keepdims=True) + 1e-6) * (dk ** -0.5)
        kn = kh * lax.rsqrt(jnp.sum(kh * kh, -1, keepdims=True) + 1e-6)
        a_h = jnp.sum(jnp.where(lane == SM_A + h, sm, 0.0), -1, keepdims=True)
        b_h = jnp.sum(jnp.where(lane == SM_B + h, sm, 0.0), -1, keepdims=True)
        neg_a = -jnp.exp(jnp.sum(jnp.where(lane == h, hp_ref[0:1, :], 0.0), -1, keepdims=True))
        dtb = jnp.sum(jnp.where(lane == h, hp_ref[1:2, :], 0.0), -1, keepdims=True)
        g = neg_a * _softplus(a_h + dtb)
        beta = _sigmoid(b_h)
        qn_scr[hh] = jnp.where(valid, qn, 0.0)
        kn_scr[hh] = jnp.where(valid, kn, 0.0)
        vn_scr[hh] = jnp.where(valid, vc[:, cs], 0.0)
        gb_scr[hh] = jnp.where(lane == 0, jnp.where(valid, g, 0.0), jnp.where(valid, beta, 0.0))

    r = _iota((c, c), 0)
    q = _iota((c, c), 1)
    incl = r >= q
    strict = r > q

    where = [(hh, slice(ci * c, (ci + 1) * c)) for hh in range(hp) for ci in range(nc)]
    lmats, vbs, kbes = [], [], []
    for hh, rs in where:
        qi = qn_scr[hh, rs, :]
        ki = kn_scr[hh, rs, :]
        gb = gb_scr[hh, rs, :]
        gi = gb[:, 0:1]
        bi = gb[:, 1:2]
        g_row = jnp.sum(jnp.where(r == q, gi, 0.0), 0, keepdims=True)
        gcum_col = jnp.sum(jnp.where(incl, g_row, 0.0), 1, keepdims=True)
        gcum_row = jnp.sum(jnp.where(r <= q, gi, 0.0), 0, keepdims=True)
        decay = jnp.where(incl, jnp.exp(jnp.where(incl, gcum_col - gcum_row, 0.0)), 0.0)
        kb = ki * bi
        eg = jnp.exp(gcum_col)
        g_last = gcum_col[c - 1:c, :]
        lmats.append(jnp.where(strict, _bdot_nt(kb, ki) * decay, 0.0))
        vbs.append(vn_scr[hh, rs, :] * bi)
        kbes.append(kb * eg)
        a_scr[hh, rs, :] = jnp.where(incl, _bdot_nt(qi, ki) * decay, 0.0)
        qe_scr[hh, rs, :] = qi * eg
        kd_scr[hh, rs, :] = ki * jnp.exp(g_last - gcum_col)
        e0 = rs.start // c * 8
        eg_scr[hh, e0:e0 + 8, :] = jnp.broadcast_to(jnp.exp(g_last), (8, LANES))
    tms = _tri_inverse(lmats, c)
    for (hh, rs), tm, vb, kbe in zip(where, tms, vbs, kbes):
        u_scr[hh, rs, :] = _bdot(tm, vb)
        w_scr[hh, rs, :] = _bdot(tm, kbe)

    nw = nw_ref[...]

    def chunk(ci, carry):
        r0 = pl.multiple_of(ci * c, c)
        e0 = pl.multiple_of(ci * 8, 8)
        for hh in range(hp):
            s = s_scr[hh]
            wq = jnp.concatenate([w_scr[hh, pl.ds(r0, c), :], qe_scr[hh, pl.ds(r0, c), :]], axis=0)
            ws = _bdot(wq, s)
            v_new = u_scr[hh, pl.ds(r0, c), :] - ws[0:c]
            o = ws[c:2 * c] + _bdot(a_scr[hh, pl.ds(r0, c), :], v_new)
            s_scr[hh] = s * eg_scr[hh, pl.ds(e0, 8), :][0:1, :] + _bdot_tn(kd_scr[hh, pl.ds(r0, c), :], v_new)
            o = o * lax.rsqrt(jnp.mean(o * o, -1, keepdims=True) + RMS_EPS) * nw
            o_ref[0, pl.ds(r0, c), hh * dk:(hh + 1) * dk] = o * _silu(z_ref[0, pl.ds(r0, c), hh * dk:(hh + 1) * dk])
        return carry

    lax.fori_loop(0, nc, chunk, 0)

    @pl.when(t == nt - 1)
    def _():
        sout_ref[0] = s_scr[...]


def _deltanet(p3, hist, s0, conv_w, a_log, dt_bias, norm_w, *, t_valid, tb, c, hp):
    bsz, tpad, _ = p3.shape
    assert tpad % tb == 0 and tb % c == 0 and tb % 8 == 0 and DN_HEADS % hp == 0
    nt = tpad // tb
    dk = DN_HEAD_DIM
    wid = hp * dk
    cw = jnp.concatenate([conv_w, jnp.zeros((8 - CONV_W, conv_w.shape[1]), F32)], 0)
    hpar = jnp.zeros((8, LANES), F32).at[0, :DN_HEADS].set(a_log).at[1, :DN_HEADS].set(dt_bias)
    nw = norm_w.reshape(1, dk)
    nb = DN_WIDTH // wid

    tok = lambda off: pl.BlockSpec((1, tb, wid), lambda b, h, t: (b, t, off + h))
    his = lambda off: pl.BlockSpec((1, 8, wid), lambda b, h, t: (b, 0, off + h))
    cws = lambda off: pl.BlockSpec((8, wid), lambda b, h, t: (0, off + h))
    st = pl.BlockSpec((1, hp, dk, dk), lambda b, h, t: (b, h, 0, 0))
    kern = functools.partial(_dn_kernel, tb=tb, c=c, t_valid=t_valid, hp=hp)
    big = pltpu.VMEM((hp, tb, dk), F32)
    return pl.pallas_call(
        kern,
        grid=(bsz, DN_HEADS // hp, nt),
        in_specs=[tok(0), tok(nb), tok(2 * nb), tok(C_Z // wid),
                  pl.BlockSpec((1, tb, LANES), lambda b, h, t: (b, t, C_SMALL // LANES)),
                  his(0), his(nb), his(2 * nb), cws(0), cws(nb), cws(2 * nb),
                  pl.BlockSpec((8, LANES), lambda b, h, t: (0, 0)),
                  pl.BlockSpec((1, dk), lambda b, h, t: (0, 0)),
                  st],
        out_specs=[pl.BlockSpec((1, tb, wid), lambda b, h, t: (b, t, h)), st],
        out_shape=[jax.ShapeDtypeStruct((bsz, tpad, DN_WIDTH), F32),
                   jax.ShapeDtypeStruct((bsz, DN_HEADS, dk, dk), F32)],
        scratch_shapes=[pltpu.VMEM((hp, dk, dk), F32),
                        pltpu.VMEM((3, tb + 8, wid), F32),
                        big, big, big, big, big, big, big, big,
                        pltpu.VMEM((hp, tb, c), F32),
                        pltpu.VMEM((hp, (tb // c) * 8, LANES), F32)],
        compiler_params=_cparams(("parallel", "parallel", "arbitrary")),
        name="gated_deltanet",
    )(p3, p3, p3, p3, p3, hist, hist, hist, cw, cw, cw, hpar, nw, s0)


def _cmp_weights(w1, w2):
    assert NSA_KV_HEADS == 2 and CMP_BLOCK // CMP_STRIDE == 2
    w1r = w1.reshape(2, 2, CMP_STRIDE, NSA_HEAD_DIM, NSA_HEAD_DIM).astype(BF16)
    a0, a1 = w1r[:, 0], w1r[:, 1]
    z = jnp.zeros_like(a0)
    wfs = jnp.concatenate([jnp.concatenate([a0, z, a1, z], axis=-1),
                           jnp.concatenate([z, a0, z, a1], axis=-1)], axis=2)
    z2 = jnp.zeros((NSA_HEAD_DIM, NSA_HEAD_DIM), BF16)
    w2b = w2.astype(BF16)
    w2bd = jnp.concatenate([jnp.concatenate([w2b[0], z2, z2, z2], axis=1),
                            jnp.concatenate([z2, w2b[0], z2, z2], axis=1),
                            jnp.concatenate([z2, z2, w2b[1], z2], axis=1),
                            jnp.concatenate([z2, z2, z2, w2b[1]], axis=1)], axis=0)
    return wfs, w2bd


def _cmp_epi_kernel(p_ref, pos_ref, w1_ref, w2_ref, o_ref):
    pm = p_ref[0]
    n = pm.shape[0]
    nxt = pltpu.roll(pm[:, KV_COLS:2 * KV_COLS], n - 1, 0)
    b_k = _hdot(pos_ref[0:1, :], w1_ref[0])
    b_v = _hdot(pos_ref[1:2, :], w1_ref[1])
    bias = jnp.concatenate([b_k, b_k, b_v, b_v], axis=-1)
    h = pm[:, 0:KV_COLS] + nxt + bias
    o_ref[0] = _bdot(jax.nn.gelu(h), w2_ref[...])


def _cmp_epilogue(pmat, pos, w1, w2bd):
    bsz, n_sub, wid = pmat.shape
    return pl.pallas_call(
        _cmp_epi_kernel,
        grid=(bsz,),
        in_specs=[pl.BlockSpec((1, n_sub, wid), lambda b: (b, 0, 0)),
                  pl.BlockSpec(pos.shape, lambda b: (0, 0)),
                  pl.BlockSpec(w1.shape, lambda b: (0, 0, 0)),
                  pl.BlockSpec(w2bd.shape, lambda b: (0, 0))],
        out_specs=pl.BlockSpec((1, n_sub, wid // 2), lambda b: (b, 0, 0)),
        out_shape=jax.ShapeDtypeStruct((bsz, n_sub, wid // 2), F32),
        compiler_params=_cparams(("parallel",)),
        name="nsa_compress_epilogue",
    )(pmat, pos, w1, w2bd)


def _cmp_paged_kernel(pt_ref, cache_ref, w_ref, o_ref, buf, rows_k, rows_v, sem, *, npg):
    i = pl.program_id(0)
    n = pl.num_programs(0)
    spp = PAGE_SIZE // CMP_STRIDE

    def page_copy(page, slot, j):
        return pltpu.make_async_copy(cache_ref.at[page], buf.at[slot, j], sem.at[slot])

    def issue(step, slot):
        for j in range(npg):
            page_copy(pt_ref[step * npg + j], slot, j).start()

    @pl.when(i == 0)
    def _():
        issue(0, 0)

    @pl.when(i + 1 < n)
    def _():
        issue(i + 1, (i + 1) % 2)

    slot = i % 2
    for j in range(npg):
        page_copy(0, slot, j).wait()
    rows = (rows_k, rows_v)
    for hf in range(2):
        for j in range(npg):
            rows[hf][j * PAGE_SIZE:(j + 1) * PAGE_SIZE, :] = buf[slot, j, hf * LANES:(hf + 1) * LANES, :].T
    for hf in range(2):
        _project_half(lambda p, r=rows[hf]: r[pl.ds(p, npg * spp, stride=CMP_STRIDE), :], w_ref, hf, o_ref)


def _project_half(rows_at, w_ref, hf, o_ref):
    acc = jnp.zeros((o_ref.shape[-2], 2 * LANES), F32)
    for p in range(CMP_STRIDE):
        acc = acc + jnp.dot(rows_at(p).astype(BF16), w_ref[hf, p], preferred_element_type=F32)
    for rslot in range(CMP_BLOCK // CMP_STRIDE):
        c0 = rslot * KV_COLS + hf * LANES
        o_ref[..., c0:c0 + LANES] = acc[:, rslot * LANES:(rslot + 1) * LANES].reshape(o_ref.shape[:-1] + (LANES,))


def _cmp_paged(cache_t, page_table, wfs, *, npg):
    n_pool, cols, psz = cache_t.shape
    bsz, n_pages = page_table.shape
    total = bsz * n_pages
    spp = psz // CMP_STRIDE
    assert total % npg == 0 and cols == KV_COLS == 2 * LANES and psz == PAGE_SIZE
    kern = functools.partial(_cmp_paged_kernel, npg=npg)
    return pl.pallas_call(
        kern,
        grid_spec=pltpu.PrefetchScalarGridSpec(
            num_scalar_prefetch=1,
            grid=(total // npg,),
            in_specs=[pl.BlockSpec(memory_space=pl.ANY),
                      pl.BlockSpec(wfs.shape, lambda i, pt: (0, 0, 0, 0))],
            out_specs=pl.BlockSpec((npg * spp, 2 * KV_COLS), lambda i, pt: (i, 0)),
            scratch_shapes=[pltpu.VMEM((2, npg, cols, psz), F32), pltpu.VMEM((npg * psz, LANES), F32),
                            pltpu.VMEM((npg * psz, LANES), F32), pltpu.SemaphoreType.DMA((2,))]),
        out_shape=jax.ShapeDtypeStruct((total * spp, 2 * KV_COLS), F32),
        compiler_params=_cparams(("arbitrary",)),
        name="nsa_compress_paged",
    )(page_table.reshape(-1), cache_t, wfs)


def _cmp_rows_kernel(k_ref, v_ref, w_ref, o_ref):
    n_sub = o_ref.shape[1]
    for hf, ref in enumerate((k_ref, v_ref)):
        _project_half(lambda p, r=ref: r[0, pl.ds(p, n_sub, stride=CMP_STRIDE), :], w_ref, hf, o_ref)


def _cmp_rows(p3, wfs):
    bsz, seq, _ = p3.shape
    n_sub = seq // CMP_STRIDE
    half = lambda hf: pl.BlockSpec((1, seq, LANES), lambda b: (b, 0, C_KVC // LANES + hf))
    return pl.pallas_call(
        _cmp_rows_kernel,
        grid=(bsz,),
        in_specs=[half(0), half(1), pl.BlockSpec(wfs.shape, lambda b: (0, 0, 0, 0))],
        out_specs=pl.BlockSpec((1, n_sub, 2 * KV_COLS), lambda b: (b, 0, 0)),
        out_shape=jax.ShapeDtypeStruct((bsz, n_sub, 2 * KV_COLS), F32),
        compiler_params=_cparams(("parallel",)),
        name="nsa_compress_rows",
    )(p3, p3, wfs)


def _slope(head):
    return 2.0 ** (-8.0 * (head + 1) / NSA_HEADS)


def _gather_heads(q_ref, hk):
    g = NSA_GROUP
    dh = NSA_HEAD_DIM
    qs = jnp.concatenate([q_ref[0, :, (hk * g + i) * dh:(hk * g + i + 1) * dh] for i in range(g)], axis=0)
    return qs * (dh ** -0.5)


def _cmp_branch(qs, kc, vc, hk, valid_c, dist_c, tq):
    s_all = _bdot_nt(qs, kc)
    ps = []
    psum = None
    for i in range(NSA_GROUP):
        s = s_all[i * tq:(i + 1) * tq] - _slope(hk * NSA_GROUP + i) * dist_c
        s = jnp.where(valid_c, s, NEG)
        m = jnp.max(s, -1, keepdims=True)
        p = jnp.where(valid_c, jnp.exp(s - m), 0.0)
        p = p / jnp.maximum(jnp.sum(p, -1, keepdims=True), 1e-30)
        ps.append(p)
        psum = p if psum is None else psum + p
    return _bdot(jnp.concatenate(ps, axis=0), vc), psum


def _slope_features(tq):
    out = np.zeros((NSA_KV_HEADS, NSA_GROUP * tq, NSA_HEAD_DIM), np.float32)
    for hk in range(NSA_KV_HEADS):
        for g in range(NSA_GROUP):
            rem = _slope(hk * NSA_GROUP + g) * LOG2E
            for i in range(3):
                piece = float(np.float32(rem).astype(jnp.bfloat16))
                out[hk, g * tq:(g + 1) * tq, 2 * i:2 * i + 2] = piece
                rem -= piece
    return jnp.asarray(out)


def _position_features(pos):
    lo = (pos % 256).astype(F32)
    hi = (pos - pos % 256).astype(F32)
    cols = jnp.stack([hi, lo, hi, lo, hi, lo], axis=1)
    return jnp.pad(cols, ((0, 0), (0, NSA_HEAD_DIM - 6)))


_NT = (((1,), (1,)), ((), ()))


def _cmp_branch_aug(q_aug, kc_aug, vc, valid_c, tq):
    raw = lax.dot_general(q_aug, kc_aug, _NT, preferred_element_type=F32)
    bias = jnp.where(valid_c, 0.0, NEG)
    ps = []
    psum = None
    for i in range(NSA_GROUP):
        s = raw[i * tq:(i + 1) * tq] + bias
        p = jnp.where(valid_c, jnp.exp2(s - jnp.max(s, -1, keepdims=True)), 0.0)
        p = p * (1.0 / jnp.maximum(jnp.sum(p, -1, keepdims=True), 1e-30))
        ps.append(p)
        psum = p if psum is None else psum + p
    return _bdot(jnp.concatenate(ps, axis=0), vc), psum


def _flash_branch(q_aug, kf, kv_ref, hk, t_lo, t_hi, bias_fn, m_scr, acc_scr, tq):
    g = NSA_GROUP
    dh = NSA_HEAD_DIM
    m_scr[...] = jnp.full(m_scr.shape, NEG, F32)
    acc_scr[...] = jnp.zeros(acc_scr.shape, F32)
    ones = jnp.ones((KEY_TILE, dh), F32)

    def body(i, carry):
        t = t_hi - 1 - i
        k0 = pl.multiple_of(t * KEY_TILE, KEY_TILE)
        k_aug = jnp.concatenate([kv_ref[0, pl.ds(k0, KEY_TILE), hk * dh:(hk + 1) * dh], kf], axis=1).astype(BF16)
        v = kv_ref[0, pl.ds(k0, KEY_TILE), NSA_KV_WIDTH + hk * dh:NSA_KV_WIDTH + (hk + 1) * dh]
        vaug = jnp.concatenate([v, ones], axis=1).astype(BF16)
        bias = bias_fn(k0)
        k0f = k0.astype(F32)
        half = g // 2
        raws = [lax.dot_general(q_aug[h * half * tq:(h + 1) * half * tq], k_aug, _NT, preferred_element_type=F32)
                for h in range(2)]
        for h in range(2):
            ps = []
            alphas = []
            for jj in range(half):
                j = h * half + jj
                rs = slice(j * tq, (j + 1) * tq)
                shift = k0f * (_slope(hk * g + j) * LOG2E)
                s = raws[h][jj * tq:(jj + 1) * tq] + bias
                m_old = m_scr[rs, :]
                m_new = jnp.maximum(m_old, jnp.max(s, -1, keepdims=True) + shift)
                alphas.append(jnp.exp2(m_old - m_new))
                ps.append(jnp.exp2(s - jnp.concatenate([m_new - shift] * (KEY_TILE // LANES), axis=1)).astype(BF16))
                m_scr[rs, :] = m_new
            hs = slice(h * half * tq, (h + 1) * half * tq)
            pv = jnp.dot(jnp.concatenate(ps, axis=0), vaug, preferred_element_type=F32)
            acc_scr[hs, :] = jnp.concatenate(alphas, axis=0) * acc_scr[hs, :] + pv
        return carry

    lax.fori_loop(0, t_hi - t_lo, body, 0)
    acc = acc_scr[...]
    return acc[:, 0:dh] / jnp.maximum(acc[:, dh:2 * dh], 1e-30)


def _gate_combine(sm, hk, o_c, o_s, o_w, o_ref, tq):
    g = NSA_GROUP
    dh = NSA_HEAD_DIM
    outs = []
    for i in range(g):
        c0 = SM_NG + (hk * g + i) * 3
        gt = _sigmoid(sm[:, c0:c0 + 3])
        rows = slice(i * tq, (i + 1) * tq)
        outs.append(gt[:, 0:1] * o_c[rows] + gt[:, 1:2] * o_s[rows] + gt[:, 2:3] * o_w[rows])
    for i in range(0, g, 2):
        c0 = (hk * g + i) * dh
        o_ref[0, :, c0:c0 + 2 * dh] = jnp.concatenate([outs[i], outs[i + 1]], axis=-1)


def _nsa_prompt_kernel(q_ref, sm_ref, kvs_ref, kvw_ref, kc_ref, qsl_ref, kf_ref, kfc_ref, o_ref, m_scr, acc_scr,
                       *, tq, seq, n_cmp):
    dh = NSA_HEAD_DIM
    q0 = pl.program_id(1) * tq
    n_sub = kc_ref.shape[1]
    n_slc = seq // SLC_BLOCK
    qpos_i = q0 + _iota((tq, 1), 0)
    sm = sm_ref[0]
    t_hi = (q0 + tq + KEY_TILE - 1) // KEY_TILE
    t_lo_w = jnp.maximum(q0 - (WINDOW - 1), 0) // KEY_TILE

    cidx = _iota((1, n_sub), 1)
    valid_c = (cidx * CMP_STRIDE + (CMP_BLOCK - 1) <= qpos_i) & (cidx < n_cmp)
    cr = _iota((n_slc, n_sub), 1) * CMP_STRIDE
    s_st = _iota((n_slc, n_sub), 0) * SLC_BLOCK
    cover_t = jnp.where((cr < s_st + SLC_BLOCK) & (cr + (CMP_BLOCK - 1) >= s_st), 1.0, 0.0)
    srow = _iota((n_slc, 1), 0)
    qrow = q0 + _iota((1, tq), 1)
    cur = qrow >> _log2(SLC_BLOCK)
    forced = (srow == 0) | (srow == cur) | (srow == cur - 1)
    bonus = jnp.where(forced, FORCE_BONUS, 0.0)
    past_ok = srow * SLC_BLOCK <= qrow
    kf = kf_ref[...]

    q_augs, o_cs, sels = [], [], []
    for hk in range(NSA_KV_HEADS):
        q_aug = jnp.concatenate([_gather_heads(q_ref, hk) * LOG2E, qsl_ref[hk]], axis=1).astype(BF16)
        kc_aug = jnp.concatenate([kc_ref[0, :, hk * dh:(hk + 1) * dh], kfc_ref[...]], axis=1).astype(BF16)
        vc = kc_ref[0, :, NSA_KV_WIDTH + hk * dh:NSA_KV_WIDTH + (hk + 1) * dh]
        o_c, psum = _cmp_branch_aug(q_aug, kc_aug, vc, valid_c, tq)
        q_augs.append(q_aug)
        o_cs.append(o_c)
        score_t = jnp.where(past_ok, _hdot_nt(cover_t, psum) + bonus, NEG)
        sels.append(jnp.where(_rank_rows(score_t, n_slc) < N_SELECT, 1.0, 0.0).astype(BF16))

    for hk in range(NSA_KV_HEADS):
        def slc_bias(k0, sel_t=sels[hk]):
            kblk = (k0 + _iota((n_slc, KEY_TILE), 1)) >> _log2(SLC_BLOCK)
            expand = jnp.where(_iota((n_slc, KEY_TILE), 0) == kblk, 1.0, 0.0).astype(BF16)
            picked = lax.dot_general(sel_t, expand, (((0,), (0,)), ((), ())), preferred_element_type=F32)
            dist = qpos_i - (k0 + _iota((1, KEY_TILE), 1))
            return jnp.where((picked > 0.5) & (dist >= 0), 0.0, NEG)

        def win_bias(k0):
            dist = qpos_i - (k0 + _iota((1, KEY_TILE), 1))
            return jnp.where((dist >= 0) & (dist < WINDOW), 0.0, NEG)

        o_s = _flash_branch(q_augs[hk], kf, kvs_ref, hk, 0, t_hi, slc_bias, m_scr, acc_scr, tq)
        o_w = _flash_branch(q_augs[hk], kf, kvw_ref, hk, t_lo_w, t_hi, win_bias, m_scr, acc_scr, tq)
        _gate_combine(sm, hk, o_cs[hk], o_s, o_w, o_ref, tq)


def _nsa_prompt(p3, kcvc, *, tq):
    bsz, seq, _ = p3.shape
    n_sub = kcvc.shape[1]
    assert seq % KEY_TILE == 0 and seq % tq == 0 and seq % SLC_BLOCK == 0 and KEY_TILE % tq == 0
    assert seq + CMP_BLOCK < 256 * 256, "positions are split into two bf16-exact parts"
    kern = functools.partial(_nsa_prompt_kernel, tq=tq, seq=seq, n_cmp=seq // CMP_STRIDE - 1)
    rows = NSA_GROUP * tq
    qsl = _slope_features(tq)
    kf = _position_features(jnp.arange(KEY_TILE))
    kfc = _position_features(jnp.arange(n_sub) * CMP_STRIDE + (CMP_BLOCK - 1))
    full = lambda a: pl.BlockSpec(a.shape, lambda b, j: (0,) * a.ndim)
    return pl.pallas_call(
        kern,
        grid=(bsz, seq // tq),
        in_specs=[pl.BlockSpec((1, tq, NSA_WIDTH), lambda b, j: (b, j, C_NQ // NSA_WIDTH)),
                  pl.BlockSpec((1, tq, LANES), lambda b, j: (b, j, C_SMALL // LANES)),
                  pl.BlockSpec((1, seq, KV_COLS), lambda b, j: (b, 0, C_KVS // KV_COLS)),
                  pl.BlockSpec((1, seq, KV_COLS), lambda b, j: (b, 0, C_KVW // KV_COLS)),
                  pl.BlockSpec((1, n_sub, KV_COLS), lambda b, j: (b, 0, 0)),
                  full(qsl), full(kf), full(kfc)],
        out_specs=pl.BlockSpec((1, tq, NSA_WIDTH), lambda b, j: (b, j, 0)),
        out_shape=jax.ShapeDtypeStruct((bsz, seq, NSA_WIDTH), F32),
        scratch_shapes=[pltpu.VMEM((rows, LANES), F32), pltpu.VMEM((rows, 2 * NSA_HEAD_DIM), F32)],
        compiler_params=_cparams(("parallel", "arbitrary")),
        name="nsa_prompt_attention",
    )(p3, p3, p3, p3, kcvc, qsl, kf, kfc)


def _nsa_select_kernel(q_ref, kc_ref, oc_ref, sel_ref, *, tq, past, n_cmp, n_slc, n_slc_pad):
    dh = NSA_HEAD_DIM
    n_sub = kc_ref.shape[1]
    qpos_i = past + _iota((tq, 1), 0)
    qpos = qpos_i.astype(F32)
    cidx = _iota((1, n_sub), 1)
    c_end = cidx * CMP_STRIDE + (CMP_BLOCK - 1)
    valid_c = (c_end <= qpos_i) & (cidx < n_cmp)
    dist_c = qpos - c_end.astype(F32)
    cr = _iota((n_sub, n_slc_pad), 0) * CMP_STRIDE
    s_st = _iota((n_sub, n_slc_pad), 1) * SLC_BLOCK
    cover = jnp.where((cr < s_st + SLC_BLOCK) & (cr + (CMP_BLOCK - 1) >= s_st), 1.0, 0.0)
    sidx = _iota((1, n_slc_pad), 1)
    sidx_f = sidx.astype(F32)
    cur = qpos_i >> _log2(SLC_BLOCK)
    forced = (sidx == 0) | (sidx == cur) | (sidx == cur - 1)
    bonus = jnp.where(forced, FORCE_BONUS, 0.0)
    past_ok = sidx * SLC_BLOCK <= qpos_i
    lane = _iota((1, LANES), 1)
    for hk in range(NSA_KV_HEADS):
        qs = _gather_heads(q_ref, hk)
        kc = kc_ref[0, :, hk * dh:(hk + 1) * dh]
        vc = kc_ref[0, :, NSA_KV_WIDTH + hk * dh:NSA_KV_WIDTH + (hk + 1) * dh]
        o_c, psum = _cmp_branch(qs, kc, vc, hk, valid_c, dist_c, tq)
        oc_ref[0, hk] = o_c
        imp = _hdot(psum, cover)
        score = jnp.where(past_ok, imp + bonus, NEG)
        score = jnp.where(sidx < n_slc, score, -jnp.inf)
        res = jnp.zeros((tq, LANES), F32)
        for it in range(min(N_SELECT, n_slc)):
            m = jnp.max(score, -1, keepdims=True)
            idx = jnp.min(jnp.where(score == m, sidx_f, 1e9), -1, keepdims=True)
            res = jnp.where(lane == it, idx, res)
            score = jnp.where(sidx_f == idx, -jnp.inf, score)
        sel_ref[0, hk] = res.astype(I32)


def _nsa_select(ps3, kcvc, *, past, n_cmp, n_slc):
    bsz, tq, _ = ps3.shape
    n_sub = kcvc.shape[1]
    n_slc_pad = -(-n_slc // LANES) * LANES
    kern = functools.partial(_nsa_select_kernel, tq=tq, past=past, n_cmp=n_cmp, n_slc=n_slc, n_slc_pad=n_slc_pad)
    rows = NSA_GROUP * tq
    return pl.pallas_call(
        kern,
        grid=(bsz,),
        in_specs=[pl.BlockSpec((1, tq, NSA_WIDTH), lambda b: (b, 0, C_NQ // NSA_WIDTH)),
                  pl.BlockSpec((1, n_sub, KV_COLS), lambda b: (b, 0, 0))],
        out_specs=[pl.BlockSpec((1, NSA_KV_HEADS, rows, NSA_HEAD_DIM), lambda b: (b, 0, 0, 0)),
                   pl.BlockSpec((1, NSA_KV_HEADS, tq, LANES), lambda b: (b, 0, 0, 0))],
        out_shape=[jax.ShapeDtypeStruct((bsz, NSA_KV_HEADS, rows, NSA_HEAD_DIM), F32),
                   jax.ShapeDtypeStruct((bsz, NSA_KV_HEADS, tq, LANES), I32)],
        compiler_params=_cparams(("parallel",)),
        name="nsa_sample_select",
    )(ps3, kcvc)


def _joint_softmax_pv(parts, hk, tq):
    g = NSA_GROUP
    outs = []
    for j in range(g):
        rs = slice(j * tq, (j + 1) * tq)
        slope = _slope(hk * g + j)
        ss = [jnp.where(valid, s_all[rs] - slope * dist, NEG) for s_all, valid, dist, _, _ in parts]
        m = None
        for s in ss:
            mi = jnp.max(s, -1, keepdims=True)
            m = mi if m is None else jnp.maximum(m, mi)
        num = None
        den = None
        for s, (_, valid, _, v, v_t) in zip(ss, parts):
            p = jnp.where(valid, jnp.exp(s - m), 0.0)
            d = jnp.sum(p, -1, keepdims=True)
            o = _bdot_nt(p, v) if v_t else _bdot(p, v)
            num = o if num is None else num + o
            den = d if den is None else den + d
        outs.append(num / jnp.maximum(den, 1e-30))
    return jnp.concatenate(outs, axis=0)


def _nsa_sample_kernel(phys_ref, q_ref, sm_ref, kpos_ref, tail_ref, wcache_ref, wnew_ref, oc_ref, cache_ref,
                       o_ref, kbuf, vbuf, sem, *, tq, t_valid, past, n_gather):
    dh = NSA_HEAD_DIM
    b = pl.program_id(0)
    per_b = NSA_KV_HEADS * n_gather

    def page_copies(page, hk, i):
        dst = pl.ds(i * PAGE_SIZE, PAGE_SIZE)
        return (pltpu.make_async_copy(cache_ref.at[page, pl.ds(hk * dh, dh), :], kbuf.at[hk, :, dst], sem),
                pltpu.make_async_copy(cache_ref.at[page, pl.ds(NSA_KV_WIDTH + hk * dh, dh), :], vbuf.at[hk, :, dst], sem))

    for hk in range(NSA_KV_HEADS):
        for i in range(n_gather):
            for cp in page_copies(phys_ref[b * per_b + hk * n_gather + i], hk, i):
                cp.start()

    qpos_i = past + _iota((tq, 1), 0)
    qpos = qpos_i.astype(F32)
    sm = sm_ref[0]
    n_keys = n_gather * PAGE_SIZE
    per_q = n_keys // t_valid
    new_ok = _iota((1, tq), 1) < t_valid
    dist_new = qpos - (past + _iota((1, tq), 1)).astype(F32)
    n_win = wcache_ref.shape[2]
    dist_wc = qpos - (past - n_win + _iota((1, n_win), 1)).astype(F32)
    ok_wc = (dist_wc >= 0.0) & (dist_wc < float(WINDOW))
    ok_wn = (dist_new >= 0.0) & (dist_new < float(WINDOW)) & new_ok

    qss = [_gather_heads(q_ref, hk) for hk in range(NSA_KV_HEADS)]
    win = []
    for hk in range(NSA_KV_HEADS):
        kw_t = wcache_ref[0, hk * dh:(hk + 1) * dh, :]
        vw_t = wcache_ref[0, NSA_KV_WIDTH + hk * dh:NSA_KV_WIDTH + (hk + 1) * dh, :]
        kn = wnew_ref[0, :, hk * dh:(hk + 1) * dh]
        vn = wnew_ref[0, :, NSA_KV_WIDTH + hk * dh:NSA_KV_WIDTH + (hk + 1) * dh]
        win.append(_joint_softmax_pv([(_bdot(qss[hk], kw_t), ok_wc, dist_wc, vw_t, True),
                                      (_bdot_nt(qss[hk], kn), ok_wn, dist_new, vn, False)], hk, tq))

    for hk in range(NSA_KV_HEADS):
        for i in range(n_gather):
            for cp in page_copies(0, hk, i):
                cp.wait()

    g = NSA_GROUP
    grow = _iota((g, g * tq), 0)
    gcol = _iota((g, g * tq), 1)
    gi = _iota((g, 1), 0)
    tcol = _iota((1, tq), 1)
    probs = [(hk, q) for hk in range(NSA_KV_HEADS) for q in range(t_valid)]
    slopes = []
    for hk in range(NSA_KV_HEADS):
        slope = jnp.zeros((g, 1), F32)
        for j in range(g):
            slope = jnp.where(gi == j, _slope(hk * g + j), slope)
        slopes.append(slope)
    kts = [tail_ref[0, :, hk * dh:(hk + 1) * dh] for hk in range(NSA_KV_HEADS)]
    vts = [tail_ref[0, :, NSA_KV_WIDTH + hk * dh:NSA_KV_WIDTH + (hk + 1) * dh] for hk in range(NSA_KV_HEADS)]
    picks = [jnp.where(gcol == grow * tq + q, 1.0, 0.0) for _, q in probs]
    q8s = [_hdot(pick, qss[hk]) for pick, (hk, _) in zip(picks, probs)]
    raw_p = [_bdot(q8, kbuf[hk, :, q * per_q:(q + 1) * per_q]) for q8, (hk, q) in zip(q8s, probs)]
    raw_t = [_bdot_nt(q8, kts[hk]) for q8, (hk, _) in zip(q8s, probs)]
    pps, pts, dens = [], [], []
    for (hk, q), rp, rt in zip(probs, raw_p, raw_t):
        dist_p = float(past + q) - kpos_ref[0, hk][:, q * per_q:(q + 1) * per_q]
        dist_t = (q - tcol).astype(F32)
        ok_t = (dist_t >= 0.0) & new_ok
        s_p = jnp.where(dist_p >= 0.0, rp - slopes[hk] * dist_p, NEG)
        s_t = jnp.where(ok_t, rt - slopes[hk] * dist_t, NEG)
        m = jnp.maximum(jnp.max(s_p, -1, keepdims=True), jnp.max(s_t, -1, keepdims=True))
        p_p = jnp.where(dist_p >= 0.0, jnp.exp(s_p - m), 0.0)
        p_t = jnp.where(ok_t, jnp.exp(s_t - m), 0.0)
        pps.append(p_p)
        pts.append(p_t)
        dens.append(jnp.sum(p_p, -1, keepdims=True) + jnp.sum(p_t, -1, keepdims=True))
    o_qs = [(_bdot_nt(p_p, vbuf[hk, :, q * per_q:(q + 1) * per_q]) + _bdot(p_t, vts[hk])) / jnp.maximum(den, 1e-30)
            for (hk, q), p_p, p_t, den in zip(probs, pps, pts, dens)]
    backs = [_hdot_tn(pick, o_q) for pick, o_q in zip(picks, o_qs)]
    for hk in range(NSA_KV_HEADS):
        o_s = backs[hk * t_valid]
        for q in range(1, t_valid):
            o_s = o_s + backs[hk * t_valid + q]
        _gate_combine(sm, hk, oc_ref[0, hk], o_s, win[hk], o_ref, tq)


def _nsa_sample(ps3, o_c, phys, kpos, cache_t, win_t, *, t_valid, past):
    bsz, tq, _ = ps3.shape
    n_gather = t_valid * N_SELECT
    rows = NSA_GROUP * tq
    n_keys = n_gather * PAGE_SIZE
    kern = functools.partial(_nsa_sample_kernel, tq=tq, t_valid=t_valid, past=past, n_gather=n_gather)
    return pl.pallas_call(
        kern,
        grid_spec=pltpu.PrefetchScalarGridSpec(
            num_scalar_prefetch=1,
            grid=(bsz,),
            in_specs=[pl.BlockSpec((1, tq, NSA_WIDTH), lambda b, ph: (b, 0, C_NQ // NSA_WIDTH)),
                      pl.BlockSpec((1, tq, LANES), lambda b, ph: (b, 0, C_SMALL // LANES)),
                      pl.BlockSpec((1, NSA_KV_HEADS, 1, n_keys), lambda b, ph: (b, 0, 0, 0)),
                      pl.BlockSpec((1, tq, KV_COLS), lambda b, ph: (b, 0, C_KVS // KV_COLS)),
                      pl.BlockSpec((1,) + win_t.shape[1:], lambda b, ph: (b, 0, 0)),
                      pl.BlockSpec((1, tq, KV_COLS), lambda b, ph: (b, 0, C_KVW // KV_COLS)),
                      pl.BlockSpec((1, NSA_KV_HEADS, rows, NSA_HEAD_DIM), lambda b, ph: (b, 0, 0, 0)),
                      pl.BlockSpec(memory_space=pl.ANY)],
            out_specs=pl.BlockSpec((1, tq, NSA_WIDTH), lambda b, ph: (b, 0, 0)),
            scratch_shapes=[pltpu.VMEM((NSA_KV_HEADS, NSA_HEAD_DIM, n_keys), F32),
                            pltpu.VMEM((NSA_KV_HEADS, NSA_HEAD_DIM, n_keys), F32),
                            pltpu.SemaphoreType.DMA(())]),
        out_shape=jax.ShapeDtypeStruct((bsz, tq, NSA_WIDTH), F32),
        compiler_params=_cparams(("arbitrary",)),
        name="nsa_sample_attention",
    )(phys, ps3, ps3, kpos, ps3, win_t, ps3, o_c, cache_t)


def _rows_transposed(cache):
    nd = cache.ndim
    perm = tuple(range(nd - 4)) + (nd - 3, nd - 2, nd - 1, nd - 4)
    t = jnp.transpose(cache, perm)
    return t.reshape(t.shape[:nd - 4] + (KV_COLS, cache.shape[nd - 4]))


def _prompt_mixers(x, w_r, conv_w, a_log, dt_bias, norm_w, cmp_wf, cmp_w2bd, cmp_pos, cmp_w1, tl):
    bsz, seq, _ = x.shape
    p = _matmul(x.reshape(bsz * seq, D_MODEL), w_r, tl["proj_tm"], tl["proj_tn"])
    p3 = p.reshape(bsz, seq, P_COLS)
    hist = jnp.zeros((bsz, 8, 3 * DN_WIDTH), F32)
    s0 = jnp.zeros((bsz, DN_HEADS, DN_HEAD_DIM, DN_HEAD_DIM), F32)
    o_dn, s_new = _deltanet(p3, hist, s0, conv_w, a_log, dt_bias, norm_w, t_valid=seq, tb=tl["dn_tb"], c=DN_CHUNK,
                            hp=tl["dn_heads"])
    assert seq % CMP_STRIDE == 0
    kcvc = _cmp_epilogue(_cmp_rows(p3, cmp_wf), cmp_pos.reshape(2, -1), cmp_w1, cmp_w2bd)
    o_nsa = _nsa_prompt(p3, kcvc, tq=tl["nsa_tq"])
    return p3, o_dn, s_new, o_nsa


def _sample_mixers(x, cache_cmp, cache_slc, win_buf, s0, conv_buf, page_table, w_r, conv_w, a_log, dt_bias,
                   norm_w, cmp_wf, cmp_w2bd, cmp_pos, cmp_w1, tl):
    bsz, t, _ = x.shape
    tq = 8
    n_pages = page_table.shape[1]
    past = n_pages * PAGE_SIZE
    assert t <= tq and t <= SLC_BLOCK and past % SLC_BLOCK == 0 and cache_cmp.shape[1] == PAGE_SIZE
    assert (past + t) // CMP_STRIDE * CMP_STRIDE == past, "new rows never complete a compression sub-block"
    ps = _matmul(x.reshape(bsz * t, D_MODEL), w_r, bsz * t, tl["proj_tn"]).reshape(bsz, t, P_COLS)
    ps3 = jnp.pad(ps, ((0, 0), (0, tq - t), (0, 0)))
    hist = jnp.pad(conv_buf, ((0, 0), (8 - (CONV_W - 1), 0), (0, 0)))
    o_dn, s_new = _deltanet(ps3, hist, s0, conv_w, a_log, dt_bias, norm_w, t_valid=t, tb=tq, c=tq, hp=DN_HEADS)
    n_sub = past // CMP_STRIDE
    pmat = _cmp_paged(_rows_transposed(cache_cmp), page_table, cmp_wf, npg=tl["cmp_pages"])
    kcvc = _cmp_epilogue(pmat.reshape(bsz, n_sub, -1), cmp_pos.reshape(2, -1), cmp_w1, cmp_w2bd)
    n_past_blocks = past // SLC_BLOCK
    o_c, sel = _nsa_select(ps3, kcvc, past=past, n_cmp=n_sub - 1, n_slc=n_past_blocks + 1)
    sel = sel[:, :, :t, :N_SELECT]
    bpp = PAGE_SIZE // SLC_BLOCK
    jp = jnp.minimum(sel, n_past_blocks - 1)
    page = jp // bpp
    phys = page_table[jnp.arange(bsz)[:, None, None, None], page]
    row = jnp.arange(PAGE_SIZE)
    in_blk = (row // SLC_BLOCK == (jp % bpp)[..., None]) & (sel < n_past_blocks)[..., None]
    kpos = jnp.where(in_blk, (page[..., None] * PAGE_SIZE + row).astype(F32), 1e9)
    kpos = kpos.reshape(bsz, NSA_KV_HEADS, 1, t * N_SELECT * PAGE_SIZE)
    o_nsa = _nsa_sample(ps3, o_c, phys.reshape(-1).astype(I32), kpos, _rows_transposed(cache_slc),
                        _rows_transposed(win_buf), t_valid=t, past=past)
    return ps, o_dn, s_new, o_nsa


def _layer_norm(x, g, b):
    xc = x - jnp.mean(x, -1, keepdims=True)
    var = jnp.mean(xc * xc, -1, keepdims=True)
    return xc * lax.rsqrt(var + LN_EPS) * g + b


def _rank_rows(v, n):
    ri = _iota(v.shape, 0)
    rank = jnp.zeros(v.shape, F32)
    for rp in range(n):
        row = v[rp:rp + 1, :]
        beats = (row > v) | ((row == v) & (rp < ri))
        rank = rank + jnp.where(beats, 1.0, 0.0)
    return rank


def _post_mixer_kernel(x_ref, odn_ref, onsa_ref, gdn_ref, gnsa_ref, wo_ref, g_ref, b_ref, wr_ref, br_ref,
                       x1_ref, xp_ref, idx_ref, wt_ref, pos_ref, cnt_ref, run_scr, *, tm, alpha):
    i = pl.program_id(0)

    @pl.when(i == 0)
    def _():
        run_scr[...] = jnp.zeros(run_scr.shape, F32)

    h = _sigmoid(gdn_ref[0]) * odn_ref[...] + _sigmoid(gnsa_ref[0]) * onsa_ref[...]
    x1 = _layer_norm(alpha * x_ref[...] + _bdot(h, wo_ref[...]), g_ref[...], b_ref[...])
    x1_ref[...] = x1
    bits = pltpu.bitcast(x1.astype(BF16).astype(F32), jnp.uint32)
    half = x1.shape[1] // 2
    xp_ref[...] = (bits[:, :half] >> 16) | (bits[:, half:] & jnp.uint32(0xFFFF0000))

    ne = N_EXPERTS
    per = ne // N_GROUPS
    scores = _sigmoid(_hdot_nt(wr_ref[...], x1))
    s3 = (scores + br_ref[...]).reshape(N_GROUPS, per, tm)
    e3 = _iota((N_GROUPS, per, tm), 1).astype(F32)
    g1 = jnp.max(s3, axis=1, keepdims=True)
    first = jnp.min(jnp.where(s3 == g1, e3, float(per)), axis=1, keepdims=True)
    g2 = jnp.max(jnp.where(e3 == first, -jnp.inf, s3), axis=1, keepdims=True)
    grank = _rank_rows((g1 + g2).reshape(N_GROUPS, tm), N_GROUPS)
    keep = (grank < TOPK_GROUPS).reshape(N_GROUPS, 1, tm)
    selm = jnp.where(keep, s3, NEG).reshape(ne, tm)
    erank = _rank_rows(selm, ne)
    ei = _iota((ne, tm), 0).astype(F32)
    chosen = jnp.where(erank < TOP_K, 1.0, 0.0)
    tr = _iota((tm, tm), 0)
    tc = _iota((tm, tm), 1)
    before = jnp.where(tr < tc, 1.0, 0.0)
    pos_full = _bdot(chosen, before) + run_scr[:, 0:1]
    idx_rows, w_rows, pos_rows = [], [], []
    for k in range(TOP_K):
        hit = erank == float(k)
        idx_rows.append(jnp.sum(jnp.where(hit, ei, 0.0), 0, keepdims=True))
        w_rows.append(jnp.sum(jnp.where(hit, scores, 0.0), 0, keepdims=True))
        pos_rows.append(jnp.sum(jnp.where(hit, pos_full, 0.0), 0, keepdims=True))
    wsum = w_rows[0]
    for k in range(1, TOP_K):
        wsum = wsum + w_rows[k]
    zero = jnp.zeros((8 - TOP_K, tm), F32)
    idx_ref[...] = jnp.concatenate(idx_rows + [zero], 0).astype(I32)
    wt_ref[...] = jnp.concatenate([w / wsum * ROUTED_SCALE for w in w_rows] + [zero], 0)
    pos_ref[...] = jnp.concatenate(pos_rows + [zero], 0).astype(I32)
    run_scr[...] = run_scr[...] + jnp.sum(chosen, 1, keepdims=True)
    cnt_ref[...] = run_scr[...]


def _post_mixer(x, o_dn, o_nsa, p3, w_out_bf16, ln_g, ln_b, w_router_t, b_router, *, tm, alpha):
    n, d = x.shape
    assert n % tm == 0
    bsz, seq, _ = p3.shape
    assert seq % tm == 0 or tm % seq == 0
    if seq % tm == 0:
        per_b = seq // tm
        gspec = lambda c: pl.BlockSpec((1, tm, d), lambda i: (i // per_b, i % per_b, c))
        p_in = p3
    else:
        p_in = p3.reshape(1, n, P_COLS)
        gspec = lambda c: pl.BlockSpec((1, tm, d), lambda i: (0, i, c))
    tok = pl.BlockSpec((tm, d), lambda i: (i, 0))
    full = lambda a: pl.BlockSpec(a.shape, lambda i: (0,) * a.ndim)
    rt = pl.BlockSpec((8, tm), lambda i: (0, i))
    kern = functools.partial(_post_mixer_kernel, tm=tm, alpha=alpha)
    g2 = ln_g.reshape(1, d)
    b2 = ln_b.reshape(1, d)
    br = b_router.reshape(N_EXPERTS, 1)
    return pl.pallas_call(
        kern,
        grid=(n // tm,),
        in_specs=[tok, tok, tok, gspec(C_MG // d), gspec(C_MG // d + 1), full(w_out_bf16), full(g2), full(b2),
                  full(w_router_t), full(br)],
        out_specs=[tok, pl.BlockSpec((tm, d // 2), lambda i: (i, 0)), rt, rt, rt,
                   pl.BlockSpec((N_EXPERTS, LANES), lambda i: (0, 0))],
        out_shape=[jax.ShapeDtypeStruct((n, d), F32), jax.ShapeDtypeStruct((n, d // 2), jnp.uint32),
                   jax.ShapeDtypeStruct((8, n), I32),
                   jax.ShapeDtypeStruct((8, n), F32), jax.ShapeDtypeStruct((8, n), I32),
                   jax.ShapeDtypeStruct((N_EXPERTS, LANES), F32)],
        scratch_shapes=[pltpu.VMEM((N_EXPERTS, LANES), F32)],
        compiler_params=_cparams(("arbitrary",)),
        name="merge_outproj_ln_router",
    )(x, o_dn, o_nsa, p_in, p_in, w_out_bf16, g2, b2, w_router_t, br)


def _slot_kernel(ps_ref, idx_ref, pos_ref, slot_ref):
    idx = idx_ref[...]
    acc = pos_ref[...]
    for e in range(N_EXPERTS):
        acc = acc + jnp.where(idx == e, ps_ref[e], 0)
    slot_ref[...] = jnp.where(_iota(idx.shape, 0) < TOP_K, acc, 0)


def _slots(pad_start, idx, pos):
    n = idx.shape[1]
    blk = pl.BlockSpec((8, n), lambda i, ps: (0, 0))
    return pl.pallas_call(
        _slot_kernel,
        grid_spec=pltpu.PrefetchScalarGridSpec(num_scalar_prefetch=1, grid=(1,), in_specs=[blk, blk], out_specs=blk),
        out_shape=jax.ShapeDtypeStruct((8, n), I32),
        compiler_params=_cparams(("arbitrary",)),
        name="moe_slots",
    )(pad_start, idx, pos)


def _dispatch_kernel(zb_ref, nu_ref, slot_ref, x_ref, xs_ref, zero_scr, sem, zsem, *, tm, blk):
    @pl.when(pl.program_id(0) == 0)
    def _():
        zero_scr[...] = jnp.zeros(zero_scr.shape, zero_scr.dtype)
        n_blocks = xs_ref.shape[0] // blk

        def zero_copy(b):
            return pltpu.make_async_copy(zero_scr, xs_ref.at[pl.ds(pl.multiple_of(b * blk, blk), blk)], zsem)

        for e in range(N_EXPERTS):
            zero_copy(zb_ref[e]).start()
        for e in range(N_EXPERTS):
            zero_copy(0).wait()
        for t in range(n_blocks - N_EXPERTS, n_blocks):
            @pl.when(t >= nu_ref[0])
            def _(t=t):
                zero_copy(t).start()
        for t in range(n_blocks - N_EXPERTS, n_blocks):
            @pl.when(t >= nu_ref[0])
            def _():
                zero_copy(0).wait()

    def row_copy(r, s):
        return pltpu.make_async_copy(x_ref.at[pl.ds(r, 1)], xs_ref.at[pl.ds(s, 1)], sem)

    def issue(r, carry):
        for k in range(TOP_K):
            row_copy(r, slot_ref[k, r]).start()
        return carry

    lax.fori_loop(0, tm, issue, 0, unroll=8)

    def drain(r, carry):
        for k in range(TOP_K):
            row_copy(0, 0).wait()
        return carry

    lax.fori_loop(0, tm, drain, 0, unroll=8)


def _dispatch(x1, slot, zero_blocks, n_used, n_slots, *, tm, blk):
    n, d = x1.shape
    assert n % tm == 0 and n_slots % blk == 0 and n_slots // blk >= N_EXPERTS
    kern = functools.partial(_dispatch_kernel, tm=tm, blk=blk)
    return pl.pallas_call(
        kern,
        grid_spec=pltpu.PrefetchScalarGridSpec(
            num_scalar_prefetch=2,
            grid=(n // tm,),
            in_specs=[pl.BlockSpec((8, tm), lambda i, zb, nu: (0, i), memory_space=pltpu.SMEM),
                      pl.BlockSpec((tm, d), lambda i, zb, nu: (i, 0))],
            out_specs=pl.BlockSpec(memory_space=pl.ANY),
            scratch_shapes=[pltpu.VMEM((blk, d), x1.dtype), pltpu.SemaphoreType.DMA(()),
                            pltpu.SemaphoreType.DMA(())]),
        out_shape=jax.ShapeDtypeStruct((n_slots, d), x1.dtype),
        compiler_params=_cparams(("arbitrary",)),
        name="moe_dispatch",
    )(zero_blocks, n_used, slot, x1)


def _expert_kernel(be_ref, nu_ref, x_ref, wg_ref, wu_ref, wd_ref, y_ref, wg_b, wu_b, wd_b):
    i = pl.program_id(0)

    @pl.when((i == 0) | (be_ref[i] != be_ref[jnp.maximum(i - 1, 0)]))
    def _():
        wg_b[...] = wg_ref[0].astype(BF16)
        wu_b[...] = wu_ref[0].astype(BF16)
        wd_b[...] = wd_ref[0].astype(BF16)

    @pl.when(i < nu_ref[0])
    def _():
        w = x_ref[...]
        x = jnp.concatenate([pltpu.bitcast(w << 16, F32), pltpu.bitcast(w & jnp.uint32(0xFFFF0000), F32)],
                            axis=1).astype(BF16)
        hg = jnp.dot(x, wg_b[...], preferred_element_type=F32)
        hu = jnp.dot(x, wu_b[...], preferred_element_type=F32)
        y_ref[...] = jnp.dot((_silu(hg) * hu).astype(BF16), wd_b[...], preferred_element_type=F32)

    @pl.when(i >= nu_ref[0])
    def _():
        y_ref[...] = jnp.zeros(y_ref.shape, F32)


def _experts(xs, blk_exp, n_used, w_gate, w_up, w_down, *, blk):
    n_slots, dpk = xs.shape
    d, de = w_gate.shape[1:]
    assert dpk * 2 == d
    n_blocks = n_slots // blk
    return pl.pallas_call(
        _expert_kernel,
        grid_spec=pltpu.PrefetchScalarGridSpec(
            num_scalar_prefetch=2,
            grid=(n_blocks,),
            in_specs=[pl.BlockSpec((blk, dpk), lambda i, be, nu: (jnp.maximum(jnp.minimum(i, nu[0] - 1), 0), 0)),
                      pl.BlockSpec((1, d, de), lambda i, be, nu: (be[i], 0, 0)),
                      pl.BlockSpec((1, d, de), lambda i, be, nu: (be[i], 0, 0)),
                      pl.BlockSpec((1, de, d), lambda i, be, nu: (be[i], 0, 0))],
            out_specs=pl.BlockSpec((blk, d), lambda i, be, nu: (i, 0)),
            scratch_shapes=[pltpu.VMEM((d, de), BF16), pltpu.VMEM((d, de), BF16), pltpu.VMEM((de, d), BF16)]),
        out_shape=jax.ShapeDtypeStruct((n_slots, d), F32),
        compiler_params=_cparams(("arbitrary",)),
        name="moe_experts",
    )(blk_exp, n_used, xs, w_gate, w_up, w_down)


def _combine_kernel(slot_ref, x_ref, w_ref, ys_ref, wsg_ref, wsu_ref, wsd_ref, g_ref, b_ref, o_ref, buf, sem,
                    *, tm, alpha):
    def row_copy(s, k, r):
        return pltpu.make_async_copy(ys_ref.at[pl.ds(s, 1)], buf.at[k, pl.ds(r, 1)], sem)

    def issue(r, carry):
        for k in range(TOP_K):
            row_copy(slot_ref[k, r], k, r).start()
        return carry

    lax.fori_loop(0, tm, issue, 0, unroll=8)
    x = x_ref[...]
    xb = x.astype(BF16)
    hs = _silu(jnp.dot(xb, wsg_ref[...], preferred_element_type=F32)) * jnp.dot(xb, wsu_ref[...],
                                                                               preferred_element_type=F32)
    acc = alpha * x + _bdot(hs, wsd_ref[...])

    def drain(r, carry):
        for k in range(TOP_K):
            row_copy(0, k, 0).wait()
        return carry

    lax.fori_loop(0, tm, drain, 0, unroll=8)
    w = w_ref[...]
    for k in range(TOP_K):
        acc = acc + w[:, k:k + 1] * buf[k]
    o_ref[...] = _layer_norm(acc, g_ref[...], b_ref[...])


def _combine(x1, slot, w_tok, ys, ws_gate, ws_up, ws_down, ln_g, ln_b, *, tm, alpha):
    n, d = x1.shape
    assert n % tm == 0
    kern = functools.partial(_combine_kernel, tm=tm, alpha=alpha)
    full = lambda a: pl.BlockSpec(a.shape, lambda i: (0,) * a.ndim)
    g2 = ln_g.reshape(1, d)
    b2 = ln_b.reshape(1, d)
    return pl.pallas_call(
        kern,
        grid=(n // tm,),
        in_specs=[pl.BlockSpec((8, tm), lambda i: (0, i), memory_space=pltpu.SMEM),
                  pl.BlockSpec((tm, d), lambda i: (i, 0)),
                  pl.BlockSpec((tm, 8), lambda i: (i, 0)),
                  pl.BlockSpec(memory_space=pl.ANY),
                  full(ws_gate), full(ws_up), full(ws_down), full(g2), full(b2)],
        out_specs=pl.BlockSpec((tm, d), lambda i: (i, 0)),
        out_shape=jax.ShapeDtypeStruct((n, d), F32),
        scratch_shapes=[pltpu.VMEM((TOP_K, tm, d), F32), pltpu.SemaphoreType.DMA(())],
        compiler_params=_cparams(("arbitrary",)),
        name="moe_combine_ln",
    )(slot, x1, w_tok, ys, ws_gate, ws_up, ws_down, g2, b2)


def _moe_layer(x1, xp, idx, wts, pos, counts, w_gate, w_up, w_down, ws_gate, ws_up, ws_down, ln_g, ln_b,
               *, blk, tm_d, tm_c, alpha):
    n = x1.shape[0]
    cnt = counts[:, 0].astype(I32)
    padded = (cnt + blk - 1) // blk * blk
    pad_end = jnp.cumsum(padded)
    slot = _slots((pad_end - padded).astype(I32), idx, pos)
    n_blocks = -(-(n * TOP_K) // blk) + N_EXPERTS
    blk_exp = jnp.minimum(jnp.sum(pad_end[None, :] <= (jnp.arange(n_blocks) * blk)[:, None], axis=1),
                          N_EXPERTS - 1).astype(I32)
    n_used = (pad_end[-1:] // blk).astype(I32)
    empty = padded == 0
    zero_blocks = jnp.where(empty, n_blocks - jnp.cumsum(empty), pad_end // blk - 1).astype(I32)
    xs = _dispatch(xp, slot, zero_blocks, n_used, n_blocks * blk, tm=tm_d, blk=blk)
    ys = _experts(xs, blk_exp, n_used, w_gate, w_up, w_down, blk=blk)
    return _combine(x1, slot, wts.T, ys, ws_gate.astype(BF16), ws_up.astype(BF16), ws_down.astype(BF16),
                    ln_g, ln_b, tm=tm_c, alpha=alpha)


def kernel(x_prompt, x_sample, cache_cmp_kv, cache_slc_kv, cache_win_kv, state_delta_S, state_delta_conv, page_table, w_in, dn_conv_w, dn_A_log, dn_dt_bias, dn_norm_w, nsa_cmp_w1, nsa_cmp_pos, nsa_cmp_w2, w_out, ln1_g, ln1_b, w_router, b_router, w_exp_gate, w_exp_up, w_exp_down, w_sh_gate, w_sh_up, w_sh_down, ln2_g, ln2_b):
    depth = w_in.shape[0]
    assert depth == 1
    alpha = (2.0 * depth) ** 0.25
    bsz, seq, d = x_prompt.shape
    sb, st, _ = x_sample.shape
    tl = _tiles(bsz * seq, seq, sb * st)
    w_r = _reorder_w_in(w_in[0])
    wf, w2bd = _cmp_weights(nsa_cmp_w1[0], nsa_cmp_w2[0])
    mix_w = (w_r, dn_conv_w[0], dn_A_log[0], dn_dt_bias[0], dn_norm_w[0], wf, w2bd, nsa_cmp_pos[0], nsa_cmp_w1[0])
    p3, o_dn, s_p, o_nsa = _prompt_mixers(x_prompt, *mix_w, tl)
    ps, o_dn_s, s_s, o_nsa_s = _sample_mixers(x_sample, cache_cmp_kv[0], cache_slc_kv[0], cache_win_kv[0],
                                              state_delta_S[0], state_delta_conv[0], page_table, *mix_w, tl)
    wo = w_out[0].astype(BF16)
    wrt = w_router[0].T

    def ffn(x2, o_dn2, o_nsa2, p_any, tm, blk, tm_d, tm_c):
        x1, xp, idx, wts, pos, counts = _post_mixer(x2, o_dn2, o_nsa2, p_any, wo, ln1_g[0], ln1_b[0], wrt,
                                                    b_router[0], tm=tm, alpha=alpha)
        return _moe_layer(x1, xp, idx, wts, pos, counts, w_exp_gate[0], w_exp_up[0], w_exp_down[0],
                          w_sh_gate[0], w_sh_up[0], w_sh_down[0], ln2_g[0], ln2_b[0],
                          blk=blk, tm_d=tm_d, tm_c=tm_c, alpha=alpha)

    y_p = ffn(x_prompt.reshape(-1, d), o_dn.reshape(-1, d), o_nsa.reshape(-1, d), p3,
              tl["post_tm"], tl["moe_blk"], tl["moe_tm_dispatch"], tl["moe_tm_combine"])
    y_s = ffn(x_sample.reshape(-1, d), o_dn_s[:, :st].reshape(-1, d), o_nsa_s[:, :st].reshape(-1, d), ps,
              tl["sample_tm"], tl["sample_moe_blk"], tl["sample_tm"], tl["sample_tm"])

    kv_shape = (2, NSA_KV_HEADS, NSA_HEAD_DIM)

    def kv_rows(pp, c0):
        return pp[:, :, c0:c0 + KV_COLS].reshape(pp.shape[:2] + kv_shape)

    nconv = CONV_W - 1
    conv_p = jnp.concatenate([jnp.zeros((bsz, nconv, 3 * DN_WIDTH), F32), p3[:, :, :3 * DN_WIDTH]], 1)[:, -nconv:]
    conv_s = jnp.concatenate([state_delta_conv[0], ps[:, :, :3 * DN_WIDTH]], 1)[:, -nconv:]
    past = page_table.shape[1] * PAGE_SIZE
    win_s = jnp.concatenate([cache_win_kv[0], kv_rows(ps, C_KVW)], 1)[:, -min(WINDOW, past + st):]
    return (y_p.reshape(x_prompt.shape), y_s.reshape(x_sample.shape),
            kv_rows(p3, C_KVC)[None], kv_rows(p3, C_KVS)[None], kv_rows(p3, C_KVW)[:, -min(WINDOW, seq):][None],
            s_p[None], conv_p[None],
            kv_rows(ps, C_KVC)[None], kv_rows(ps, C_KVS)[None], win_s[None], s_s[None], conv_s[None])
```

```python
import functools
import math

import jax
import jax.numpy as jnp
import numpy as np
from jax import lax
from jax.experimental import pallas as pl
from jax.experimental.pallas import tpu as pltpu

F32 = jnp.float32
BF16 = jnp.bfloat16
I32 = jnp.int32
HIGHEST = lax.Precision.HIGHEST

D_MODEL = 1024
PAGE_SIZE = 128
DN_HEADS = 8
DN_HEAD_DIM = 128
DN_WIDTH = DN_HEADS * DN_HEAD_DIM
CONV_W = 4
DN_CHUNK = 64
NSA_HEADS = 16
NSA_KV_HEADS = 2
NSA_GROUP = NSA_HEADS // NSA_KV_HEADS
NSA_HEAD_DIM = 64
NSA_WIDTH = NSA_HEADS * NSA_HEAD_DIM
NSA_KV_WIDTH = NSA_KV_HEADS * NSA_HEAD_DIM
KV_COLS = 2 * NSA_KV_WIDTH
CMP_BLOCK = 32
CMP_STRIDE = 16
SLC_BLOCK = 64
N_SELECT = 16
WINDOW = 512
N_EXPERTS = 64
TOP_K = 6
N_GROUPS = 8
TOPK_GROUPS = 4
ROUTED_SCALE = 2.5
LN_EPS = 1e-5
RMS_EPS = 1e-6
NEG = -1e30
LOG2E = math.log2(math.e)
FORCE_BONUS = 1e6

C_QKV = 0
C_Z = 3072
C_NQ = 4096
C_MG = 5120
C_KVC = 7168
C_KVS = 7424
C_KVW = 7680
C_SMALL = 7936
P_COLS = 8064
SM_A = 0
SM_B = DN_HEADS
SM_NG = 2 * DN_HEADS

LANES = 128
VMEM_LIMIT = 48 * 1024 * 1024
KEY_TILE = 256


def _tiles(n_prompt_tokens, seq, n_sample_tokens):
    return dict(
        proj_tm=min(512, n_prompt_tokens), proj_tn=P_COLS // 3,
        dn_tb=min(256, seq), dn_heads=8,
        nsa_tq=256,
        cmp_pages=32,
        post_tm=min(256, n_prompt_tokens),
        moe_blk=512, moe_tm_dispatch=min(256, n_prompt_tokens), moe_tm_combine=min(128, n_prompt_tokens),
        sample_moe_blk=64, sample_tm=n_sample_tokens,
    )


def _cparams(sem):
    return pltpu.CompilerParams(dimension_semantics=sem, vmem_limit_bytes=VMEM_LIMIT)


def _bdot(a, b):
    return jnp.dot(a.astype(BF16), b.astype(BF16), preferred_element_type=F32)


def _bdot_nt(a, b):
    return lax.dot_general(a.astype(BF16), b.astype(BF16), (((1,), (1,)), ((), ())),
                           preferred_element_type=F32)


def _bdot_tn(a, b):
    return lax.dot_general(a.astype(BF16), b.astype(BF16), (((0,), (0,)), ((), ())),
                           preferred_element_type=F32)


def _hdot(a, b):
    return jnp.dot(a, b, precision=HIGHEST, preferred_element_type=F32)


def _hdot_tn(a, b):
    return lax.dot_general(a, b, (((0,), (0,)), ((), ())), precision=HIGHEST,
                           preferred_element_type=F32)


def _hdot_nt(a, b):
    return lax.dot_general(a, b, (((1,), (1,)), ((), ())), precision=HIGHEST,
                           preferred_element_type=F32)


def _sigmoid(x):
    return 1.0 / (1.0 + jnp.exp(-x))


def _silu(x):
    return x * _sigmoid(x)


def _softplus(x):
    return jnp.maximum(x, 0.0) + jnp.log(1.0 + jnp.exp(-jnp.abs(x)))


def _iota(shape, dim):
    return lax.broadcasted_iota(I32, shape, dim)


def _log2(n):
    assert n & (n - 1) == 0
    return int(math.log2(n))


def _mm_kernel(x_ref, w_ref, o_ref):
    o_ref[...] = jnp.dot(x_ref[...].astype(BF16), w_ref[...], preferred_element_type=F32)


def _matmul(x, w_bf16, tm, tn):
    m, k = x.shape
    n = w_bf16.shape[1]
    assert m % tm == 0 and n % tn == 0
    return pl.pallas_call(
        _mm_kernel,
        grid=(n // tn, m // tm),
        in_specs=[pl.BlockSpec((tm, k), lambda j, i: (i, 0)),
                  pl.BlockSpec((k, tn), lambda j, i: (0, j))],
        out_specs=pl.BlockSpec((tm, tn), lambda j, i: (i, j)),
        out_shape=jax.ShapeDtypeStruct((m, n), F32),
        compiler_params=_cparams(("parallel", "parallel")),
        name="dense_matmul",
    )(x, w_bf16)


def _reorder_w_in(w_in):
    o = 0
    seg = {}
    for name, size in (("qkv", 3 * DN_WIDTH), ("z", DN_WIDTH), ("a", DN_HEADS), ("b", DN_HEADS),
                       ("nq", NSA_WIDTH), ("kvc", KV_COLS), ("kvs", KV_COLS),
                       ("kvw", KV_COLS), ("ng", 3 * NSA_HEADS), ("mg", 2 * D_MODEL)):
        seg[name] = w_in[:, o:o + size]
        o += size
    assert o == w_in.shape[1]
    pad = jnp.zeros((w_in.shape[0], P_COLS - C_SMALL - SM_NG - 3 * NSA_HEADS), w_in.dtype)
    w = jnp.concatenate([seg["qkv"], seg["z"], seg["nq"], seg["mg"], seg["kvc"], seg["kvs"], seg["kvw"],
                         seg["a"], seg["b"], seg["ng"], pad], axis=1)
    assert w.shape[1] == P_COLS
    return w.astype(BF16)


def _tri_inverse(lmats, c):
    r = _iota((c, c), 0)
    q = _iota((c, c), 1)
    eye = (r == q).astype(F32)
    blk = min(16, c)
    shift = _log2(blk)
    same = (r >> shift) == (q >> shift)
    dmats = [jnp.where(same, lm, 0.0) for lm in lmats]
    prods = [eye - dm for dm in dmats]
    dpows = dmats
    k = 2
    while k < blk:
        dpows = [_bdot(dp, dp) for dp in dpows]
        prods = [pr + _bdot(pr, dp) for pr, dp in zip(prods, dpows)]
        k *= 2
    if c == blk:
        return prods
    mmats = [_bdot(pr, lm - dm) for pr, lm, dm in zip(prods, lmats, dmats)]
    outers = [eye - mm for mm in mmats]
    mpows = mmats
    k = 2
    while k < c // blk:
        mpows = [_bdot(mp, mp) for mp in mpows]
        outers = [ou + _bdot(ou, mp) for ou, mp in zip(outers, mpows)]
        k *= 2
    return [_bdot(ou, pr) for ou, pr in zip(outers, prods)]


def _dn_kernel(q_ref, k_ref, v_ref, z_ref, sm_ref, hq_ref, hk_ref, hv_ref, cwq_ref, cwk_ref, cwv_ref,
               hp_ref, nw_ref, s0_ref, o_ref, sout_ref,
               s_scr, xp_scr, qn_scr, kn_scr, vn_scr, gb_scr, u_scr, w_scr, qe_scr, kd_scr, a_scr, eg_scr,
               *, tb, c, t_valid, hp):
    hb = pl.program_id(1)
    t = pl.program_id(2)
    nt = pl.num_programs(2)
    dk = DN_HEAD_DIM
    nc = tb // c

    @pl.when(t == 0)
    def _():
        s_scr[...] = s0_ref[0]
        xp_scr[0, 0:8, :] = hq_ref[0]
        xp_scr[1, 0:8, :] = hk_ref[0]
        xp_scr[2, 0:8, :] = hv_ref[0]

    rows = t * tb + _iota((tb, 1), 0)
    valid = rows < t_valid

    def conv(i, raw_ref, cw_ref):
        xp_scr[i, 8:8 + tb, :] = raw_ref[0]
        acc = xp_scr[i, 8:8 + tb, :] * cw_ref[CONV_W - 1:CONV_W, :]
        for j in range(CONV_W - 1):
            acc = acc + xp_scr[i, 8 - (CONV_W - 1) + j:8 - (CONV_W - 1) + j + tb, :] * cw_ref[j:j + 1, :]
        tail = xp_scr[i, tb:tb + 8, :]
        xp_scr[i, 0:8, :] = tail
        return _silu(acc)

    qc = conv(0, q_ref, cwq_ref)
    kc = conv(1, k_ref, cwk_ref)
    vc = conv(2, v_ref, cwv_ref)
    lane = _iota((1, LANES), 1)
    sm = sm_ref[0]
    for hh in range(hp):
        h = hb * hp + hh
        cs = slice(hh * dk, (hh + 1) * dk)
        qh = qc[:, cs]
        kh = kc[:, cs]
        qn = qh * lax.rsqrt(jnp.sum(qh * qh, -1, keepdims=True) + 1e-6) * (dk ** -0.5)
        kn = kh * lax.rsqrt(jnp.sum(kh * kh, -1, keepdims=True) + 1e-6)
        a_h = jnp.sum(jnp.where(lane == SM_A + h, sm, 0.0), -1, keepdims=True)
        b_h = jnp.sum(jnp.where(lane == SM_B + h, sm, 0.0), -1, keepdims=True)
        neg_a = -jnp.exp(jnp.sum(jnp.where(lane == h, hp_ref[0:1, :], 0.0), -1, keepdims=True))
        dtb = jnp.sum(jnp.where(lane == h, hp_ref[1:2, :], 0.0), -1, keepdims=True)
        g = neg_a * _softplus(a_h + dtb)
        beta = _sigmoid(b_h)
        qn_scr[hh] = jnp.where(valid, qn, 0.0)
        kn_scr[hh] = jnp.where(valid, kn, 0.0)
        vn_scr[hh] = jnp.where(valid, vc[:, cs], 0.0)
        gb_scr[hh] = jnp.where(lane == 0, jnp.where(valid, g, 0.0), jnp.where(valid, beta, 0.0))

    r = _iota((c, c), 0)
    q = _iota((c, c), 1)
    incl = r >= q
    strict = r > q

    where = [(hh, slice(ci * c, (ci + 1) * c)) for hh in range(hp) for ci in range(nc)]
    lmats, vbs, kbes = [], [], []
    for hh, rs in where:
        qi = qn_scr[hh, rs, :]
        ki = kn_scr[hh, rs, :]
        gb = gb_scr[hh, rs, :]
        gi = gb[:, 0:1]
        bi = gb[:, 1:2]
        g_row = jnp.sum(jnp.where(r == q, gi, 0.0), 0, keepdims=True)
        gcum_col = jnp.sum(jnp.where(incl, g_row, 0.0), 1, keepdims=True)
        gcum_row = jnp.sum(jnp.where(r <= q, gi, 0.0), 0, keepdims=True)
        decay = jnp.where(incl, jnp.exp(jnp.where(incl, gcum_col - gcum_row, 0.0)), 0.0)
        kb = ki * bi
        eg = jnp.exp(gcum_col)
        g_last = gcum_col[c - 1:c, :]
        lmats.append(jnp.where(strict, _bdot_nt(kb, ki) * decay, 0.0))
        vbs.append(vn_scr[hh, rs, :] * bi)
        kbes.append(kb * eg)
        a_scr[hh, rs, :] = jnp.where(incl, _bdot_nt(qi, ki) * decay, 0.0)
        qe_scr[hh, rs, :] = qi * eg
        kd_scr[hh, rs, :] = ki * jnp.exp(g_last - gcum_col)
        e0 = rs.start // c * 8
        eg_scr[hh, e0:e0 + 8, :] = jnp.broadcast_to(jnp.exp(g_last), (8, LANES))
    tms = _tri_inverse(lmats, c)
    for (hh, rs), tm, vb, kbe in zip(where, tms, vbs, kbes):
        u_scr[hh, rs, :] = _bdot(tm, vb)
        w_scr[hh, rs, :] = _bdot(tm, kbe)

    nw = nw_ref[...]

    def chunk(ci, carry):
        r0 = pl.multiple_of(ci * c, c)
        e0 = pl.multiple_of(ci * 8, 8)
        heads = range(hp)
        rows = pl.ds(r0, c)
        ss = [s_scr[hh] for hh in heads]
        wss = [_bdot(jnp.concatenate([w_scr[hh, rows, :], qe_scr[hh, rows, :]], axis=0), ss[hh]) for hh in heads]
        v_news = [u_scr[hh, rows, :] - wss[hh][0:c] for hh in heads]
        intra = [_bdot(a_scr[hh, rows, :], v_news[hh]) for hh in heads]
        s_adds = [_bdot_tn(kd_scr[hh, rows, :], v_news[hh]) for hh in heads]
        for hh in heads:
            s_scr[hh] = ss[hh] * eg_scr[hh, pl.ds(e0, 8), :][0:1, :] + s_adds[hh]
            o = wss[hh][c:2 * c] + intra[hh]
            o = o * lax.rsqrt(jnp.mean(o * o, -1, keepdims=True) + RMS_EPS) * nw
            o_ref[0, rows, hh * dk:(hh + 1) * dk] = o * _silu(z_ref[0, rows, hh * dk:(hh + 1) * dk])
        return carry

    lax.fori_loop(0, nc, chunk, 0)

    @pl.when(t == nt - 1)
    def _():
        sout_ref[0] = s_scr[...]


def _deltanet(p3, hist, s0, conv_w, a_log, dt_bias, norm_w, *, t_valid, tb, c, hp):
    bsz, tpad, _ = p3.shape
    assert tpad % tb == 0 and tb % c == 0 and tb % 8 == 0 and DN_HEADS % hp == 0
    nt = tpad // tb
    dk = DN_HEAD_DIM
    wid = hp * dk
    cw = jnp.concatenate([conv_w, jnp.zeros((8 - CONV_W, conv_w.shape[1]), F32)], 0)
    hpar = jnp.zeros((8, LANES), F32).at[0, :DN_HEADS].set(a_log).at[1, :DN_HEADS].set(dt_bias)
    nw = norm_w.reshape(1, dk)
    nb = DN_WIDTH // wid

    tok = lambda off: pl.BlockSpec((1, tb, wid), lambda b, h, t: (b, t, off + h))
    his = lambda off: pl.BlockSpec((1, 8, wid), lambda b, h, t: (b, 0, off + h))
    cws = lambda off: pl.BlockSpec((8, wid), lambda b, h, t: (0, off + h))
    st = pl.BlockSpec((1, hp, dk, dk), lambda b, h, t: (b, h, 0, 0))
    kern = functools.partial(_dn_kernel, tb=tb, c=c, t_valid=t_valid, hp=hp)
    big = pltpu.VMEM((hp, tb, dk), F32)
    return pl.pallas_call(
        kern,
        grid=(bsz, DN_HEADS // hp, nt),
        in_specs=[tok(0), tok(nb), tok(2 * nb), tok(C_Z // wid),
                  pl.BlockSpec((1, tb, LANES), lambda b, h, t: (b, t, C_SMALL // LANES)),
                  his(0), his(nb), his(2 * nb), cws(0), cws(nb), cws(2 * nb),
                  pl.BlockSpec((8, LANES), lambda b, h, t: (0, 0)),
                  pl.BlockSpec((1, dk), lambda b, h, t: (0, 0)),
                  st],
        out_specs=[pl.BlockSpec((1, tb, wid), lambda b, h, t: (b, t, h)), st],
        out_shape=[jax.ShapeDtypeStruct((bsz, tpad, DN_WIDTH), F32),
                   jax.ShapeDtypeStruct((bsz, DN_HEADS, dk, dk), F32)],
        scratch_shapes=[pltpu.VMEM((hp, dk, dk), F32),
                        pltpu.VMEM((3, tb + 8, wid), F32),
                        big, big, big, big, big, big, big, big,
                        pltpu.VMEM((hp, tb, c), F32),
                        pltpu.VMEM((hp, (tb // c) * 8, LANES), F32)],
        compiler_params=_cparams(("parallel", "parallel", "arbitrary")),
        name="gated_deltanet",
    )(p3, p3, p3, p3, p3, hist, hist, hist, cw, cw, cw, hpar, nw, s0)


def _cmp_weights(w1, w2):
    assert NSA_KV_HEADS == 2 and CMP_BLOCK // CMP_STRIDE == 2
    w1r = w1.reshape(2, 2, CMP_STRIDE, NSA_HEAD_DIM, NSA_HEAD_DIM).astype(BF16)
    a0, a1 = w1r[:, 0], w1r[:, 1]
    z = jnp.zeros_like(a0)
    wfs = jnp.concatenate([jnp.concatenate([a0, z, a1, z], axis=-1),
                           jnp.concatenate([z, a0, z, a1], axis=-1)], axis=2)
    z2 = jnp.zeros((NSA_HEAD_DIM, NSA_HEAD_DIM), BF16)
    w2b = w2.astype(BF16)
    w2bd = jnp.concatenate([jnp.concatenate([w2b[0], z2, z2, z2], axis=1),
                            jnp.concatenate([z2, w2b[0], z2, z2], axis=1),
                            jnp.concatenate([z2, z2, w2b[1], z2], axis=1),
                            jnp.concatenate([z2, z2, z2, w2b[1]], axis=1)], axis=0)
    return wfs, w2bd


def _cmp_epi_kernel(p_ref, pos_ref, w1_ref, w2_ref, o_ref):
    pm = p_ref[0]
    n = pm.shape[0]
    nxt = pltpu.roll(pm[:, KV_COLS:2 * KV_COLS], n - 1, 0)
    b_k = _hdot(pos_ref[0:1, :], w1_ref[0])
    b_v = _hdot(pos_ref[1:2, :], w1_ref[1])
    bias = jnp.concatenate([b_k, b_k, b_v, b_v], axis=-1)
    h = pm[:, 0:KV_COLS] + nxt + bias
    o_ref[0] = _bdot(jax.nn.gelu(h), w2_ref[...])


def _cmp_epilogue(pmat, pos, w1, w2bd):
    bsz, n_sub, wid = pmat.shape
    return pl.pallas_call(
        _cmp_epi_kernel,
        grid=(bsz,),
        in_specs=[pl.BlockSpec((1, n_sub, wid), lambda b: (b, 0, 0)),
                  pl.BlockSpec(pos.shape, lambda b: (0, 0)),
                  pl.BlockSpec(w1.shape, lambda b: (0, 0, 0)),
                  pl.BlockSpec(w2bd.shape, lambda b: (0, 0))],
        out_specs=pl.BlockSpec((1, n_sub, wid // 2), lambda b: (b, 0, 0)),
        out_shape=jax.ShapeDtypeStruct((bsz, n_sub, wid // 2), F32),
        compiler_params=_cparams(("parallel",)),
        name="nsa_compress_epilogue",
    )(pmat, pos, w1, w2bd)


def _cmp_paged_kernel(pt_ref, cache_ref, w_ref, o_ref, buf, rows_k, rows_v, sem, *, npg):
    i = pl.program_id(0)
    n = pl.num_programs(0)
    spp = PAGE_SIZE // CMP_STRIDE

    def page_copy(page, slot, j):
        return pltpu.make_async_copy(cache_ref.at[page], buf.at[slot, j], sem.at[slot])

    def issue(step, slot):
        for j in range(npg):
            page_copy(pt_ref[step * npg + j], slot, j).start()

    @pl.when(i == 0)
    def _():
        issue(0, 0)

    @pl.when(i + 1 < n)
    def _():
        issue(i + 1, (i + 1) % 2)

    slot = i % 2
    for j in range(npg):
        page_copy(0, slot, j).wait()
    rows = (rows_k, rows_v)
    for hf in range(2):
        for j in range(npg):
            rows[hf][j * PAGE_SIZE:(j + 1) * PAGE_SIZE, :] = buf[slot, j, hf * LANES:(hf + 1) * LANES, :].T
    for hf in range(2):
        _project_half(lambda p, r=rows[hf]: r[pl.ds(p, npg * spp, stride=CMP_STRIDE), :], w_ref, hf, o_ref)


def _project_half(rows_at, w_ref, hf, o_ref):
    acc = jnp.zeros((o_ref.shape[-2], 2 * LANES), F32)
    for p in range(CMP_STRIDE):
        acc = acc + jnp.dot(rows_at(p).astype(BF16), w_ref[hf, p], preferred_element_type=F32)
    for rslot in range(CMP_BLOCK // CMP_STRIDE):
        c0 = rslot * KV_COLS + hf * LANES
        o_ref[..., c0:c0 + LANES] = acc[:, rslot * LANES:(rslot + 1) * LANES].reshape(o_ref.shape[:-1] + (LANES,))


def _cmp_paged(cache_t, page_table, wfs, *, npg):
    n_pool, cols, psz = cache_t.shape
    bsz, n_pages = page_table.shape
    total = bsz * n_pages
    spp = psz // CMP_STRIDE
    assert total % npg == 0 and cols == KV_COLS == 2 * LANES and psz == PAGE_SIZE
    kern = functools.partial(_cmp_paged_kernel, npg=npg)
    return pl.pallas_call(
        kern,
        grid_spec=pltpu.PrefetchScalarGridSpec(
            num_scalar_prefetch=1,
            grid=(total // npg,),
            in_specs=[pl.BlockSpec(memory_space=pl.ANY),
                      pl.BlockSpec(wfs.shape, lambda i, pt: (0, 0, 0, 0))],
            out_specs=pl.BlockSpec((npg * spp, 2 * KV_COLS), lambda i, pt: (i, 0)),
            scratch_shapes=[pltpu.VMEM((2, npg, cols, psz), F32), pltpu.VMEM((npg * psz, LANES), F32),
                            pltpu.VMEM((npg * psz, LANES), F32), pltpu.SemaphoreType.DMA((2,))]),
        out_shape=jax.ShapeDtypeStruct((total * spp, 2 * KV_COLS), F32),
        compiler_params=_cparams(("arbitrary",)),
        name="nsa_compress_paged",
    )(page_table.reshape(-1), cache_t, wfs)


def _cmp_rows_kernel(k_ref, v_ref, w_ref, o_ref):
    n_sub = o_ref.shape[1]
    for hf, ref in enumerate((k_ref, v_ref)):
        _project_half(lambda p, r=ref: r[0, pl.ds(p, n_sub, stride=CMP_STRIDE), :], w_ref, hf, o_ref)


def _cmp_rows(p3, wfs):
    bsz, seq, _ = p3.shape
    n_sub = seq // CMP_STRIDE
    half = lambda hf: pl.BlockSpec((1, seq, LANES), lambda b: (b, 0, C_KVC // LANES + hf))
    return pl.pallas_call(
        _cmp_rows_kernel,
        grid=(bsz,),
        in_specs=[half(0), half(1), pl.BlockSpec(wfs.shape, lambda b: (0, 0, 0, 0))],
        out_specs=pl.BlockSpec((1, n_sub, 2 * KV_COLS), lambda b: (b, 0, 0)),
        out_shape=jax.ShapeDtypeStruct((bsz, n_sub, 2 * KV_COLS), F32),
        compiler_params=_cparams(("parallel",)),
        name="nsa_compress_rows",
    )(p3, p3, wfs)


def _slope(head):
    return 2.0 ** (-8.0 * (head + 1) / NSA_HEADS)


def _gather_heads(q_ref, hk):
    g = NSA_GROUP
    dh = NSA_HEAD_DIM
    qs = jnp.concatenate([q_ref[0, :, (hk * g + i) * dh:(hk * g + i + 1) * dh] for i in range(g)], axis=0)
    return qs * (dh ** -0.5)


def _cmp_branch(qs, kc, vc, hk, valid_c, dist_c, tq):
    s_all = _bdot_nt(qs, kc)
    ps = []
    psum = None
    for i in range(NSA_GROUP):
        s = s_all[i * tq:(i + 1) * tq] - _slope(hk * NSA_GROUP + i) * dist_c
        s = jnp.where(valid_c, s, NEG)
        m = jnp.max(s, -1, keepdims=True)
        p = jnp.where(valid_c, jnp.exp(s - m), 0.0)
        p = p / jnp.maximum(jnp.sum(p, -1, keepdims=True), 1e-30)
        ps.append(p)
        psum = p if psum is None else psum + p
    return _bdot(jnp.concatenate(ps, axis=0), vc), psum


def _slope_features(tq):
    out = np.zeros((NSA_KV_HEADS, NSA_GROUP * tq, NSA_HEAD_DIM), np.float32)
    for hk in range(NSA_KV_HEADS):
        for g in range(NSA_GROUP):
            rem = _slope(hk * NSA_GROUP + g) * LOG2E
            for i in range(3):
                piece = float(np.float32(rem).astype(jnp.bfloat16))
                out[hk, g * tq:(g + 1) * tq, 2 * i:2 * i + 2] = piece
                rem -= piece
    return jnp.asarray(out)


def _position_features(pos):
    lo = (pos % 256).astype(F32)
    hi = (pos - pos % 256).astype(F32)
    cols = jnp.stack([hi, lo, hi, lo, hi, lo], axis=1)
    return jnp.pad(cols, ((0, 0), (0, NSA_HEAD_DIM - 6)))


_NT = (((1,), (1,)), ((), ()))


def _cmp_branch_aug(q_aug, kc_aug, vc, valid_c, tq):
    raw = lax.dot_general(q_aug, kc_aug, _NT, preferred_element_type=F32)
    bias = jnp.where(valid_c, 0.0, NEG)
    ps = []
    psum = None
    for i in range(NSA_GROUP):
        s = raw[i * tq:(i + 1) * tq] + bias
        p = jnp.where(valid_c, jnp.exp2(s - jnp.max(s, -1, keepdims=True)), 0.0)
        p = p * (1.0 / jnp.maximum(jnp.sum(p, -1, keepdims=True), 1e-30))
        ps.append(p)
        psum = p if psum is None else psum + p
    return _bdot(jnp.concatenate(ps, axis=0), vc), psum


def _flash_branch(q_aug, kf, kv_ref, hk, t_lo, t_hi, bias_fn, m_scr, acc_scr, tq):
    g = NSA_GROUP
    dh = NSA_HEAD_DIM
    m_scr[...] = jnp.full(m_scr.shape, NEG, F32)
    acc_scr[...] = jnp.zeros(acc_scr.shape, F32)
    ones = jnp.ones((KEY_TILE, dh), F32)

    def body(i, carry):
        t = t_hi - 1 - i
        k0 = pl.multiple_of(t * KEY_TILE, KEY_TILE)
        k_aug = jnp.concatenate([kv_ref[0, pl.ds(k0, KEY_TILE), hk * dh:(hk + 1) * dh], kf], axis=1).astype(BF16)
        v = kv_ref[0, pl.ds(k0, KEY_TILE), NSA_KV_WIDTH + hk * dh:NSA_KV_WIDTH + (hk + 1) * dh]
        vaug = jnp.concatenate([v, ones], axis=1).astype(BF16)
        bias = bias_fn(k0)
        k0f = k0.astype(F32)
        half = g // 2
        raws = [lax.dot_general(q_aug[h * half * tq:(h + 1) * half * tq], k_aug, _NT, preferred_element_type=F32)
                for h in range(2)]
        for h in range(2):
            ps = []
            alphas = []
            for jj in range(half):
                j = h * half + jj
                rs = slice(j * tq, (j + 1) * tq)
                shift = k0f * (_slope(hk * g + j) * LOG2E)
                s = raws[h][jj * tq:(jj + 1) * tq] + bias
                m_old = m_scr[rs, :]
                m_new = jnp.maximum(m_old, jnp.max(s, -1, keepdims=True) + shift)
                alphas.append(jnp.exp2(m_old - m_new))
                ps.append(jnp.exp2(s - jnp.concatenate([m_new - shift] * (KEY_TILE // LANES), axis=1)).astype(BF16))
                m_scr[rs, :] = m_new
            hs = slice(h * half * tq, (h + 1) * half * tq)
            pv = jnp.dot(jnp.concatenate(ps, axis=0), vaug, preferred_element_type=F32)
            acc_scr[hs, :] = jnp.concatenate(alphas, axis=0) * acc_scr[hs, :] + pv
        return carry

    lax.fori_loop(0, t_hi - t_lo, body, 0)
    acc = acc_scr[...]
    return acc[:, 0:dh] / jnp.maximum(acc[:, dh:2 * dh], 1e-30)


def _gate_combine(sm, hk, o_c, o_s, o_w, o_ref, tq):
    g = NSA_GROUP
    dh = NSA_HEAD_DIM
    outs = []
    for i in range(g):
        c0 = SM_NG + (hk * g + i) * 3
        gt = _sigmoid(sm[:, c0:c0 + 3])
        rows = slice(i * tq, (i + 1) * tq)
        outs.append(gt[:, 0:1] * o_c[rows] + gt[:, 1:2] * o_s[rows] + gt[:, 2:3] * o_w[rows])
    for i in range(0, g, 2):
        c0 = (hk * g + i) * dh
        o_ref[0, :, c0:c0 + 2 * dh] = jnp.concatenate([outs[i], outs[i + 1]], axis=-1)


def _nsa_prompt_kernel(q_ref, sm_ref, kvs_ref, kvw_ref, kc_ref, qsl_ref, kf_ref, kfc_ref, o_ref, m_scr, acc_scr,
                       *, tq, seq, n_cmp):
    dh = NSA_HEAD_DIM
    q0 = pl.program_id(1) * tq
    n_sub = kc_ref.shape[1]
    n_slc = seq // SLC_BLOCK
    qpos_i = q0 + _iota((tq, 1), 0)
    sm = sm_ref[0]
    t_hi = (q0 + tq + KEY_TILE - 1) // KEY_TILE
    t_lo_w = jnp.maximum(q0 - (WINDOW - 1), 0) // KEY_TILE

    cidx = _iota((1, n_sub), 1)
    valid_c = (cidx * CMP_STRIDE + (CMP_BLOCK - 1) <= qpos_i) & (cidx < n_cmp)
    cr = _iota((n_slc, n_sub), 1) * CMP_STRIDE
    s_st = _iota((n_slc, n_sub), 0) * SLC_BLOCK
    cover_t = jnp.where((cr < s_st + SLC_BLOCK) & (cr + (CMP_BLOCK - 1) >= s_st), 1.0, 0.0)
    srow = _iota((n_slc, 1), 0)
    qrow = q0 + _iota((1, tq), 1)
    cur = qrow >> _log2(SLC_BLOCK)
    forced = (srow == 0) | (srow == cur) | (srow == cur - 1)
    bonus = jnp.where(forced, FORCE_BONUS, 0.0)
    past_ok = srow * SLC_BLOCK <= qrow
    kf = kf_ref[...]

    q_augs, o_cs, sels = [], [], []
    for hk in range(NSA_KV_HEADS):
        q_aug = jnp.concatenate([_gather_heads(q_ref, hk) * LOG2E, qsl_ref[hk]], axis=1).astype(BF16)
        kc_aug = jnp.concatenate([kc_ref[0, :, hk * dh:(hk + 1) * dh], kfc_ref[...]], axis=1).astype(BF16)
        vc = kc_ref[0, :, NSA_KV_WIDTH + hk * dh:NSA_KV_WIDTH + (hk + 1) * dh]
        o_c, psum = _cmp_branch_aug(q_aug, kc_aug, vc, valid_c, tq)
        q_augs.append(q_aug)
        o_cs.append(o_c)
        score_t = jnp.where(past_ok, _hdot_nt(cover_t, psum) + bonus, NEG)
        sels.append(jnp.where(_rank_rows(score_t, n_slc) < N_SELECT, 1.0, 0.0).astype(BF16))

    for hk in range(NSA_KV_HEADS):
        def slc_bias(k0, sel_t=sels[hk]):
            kblk = (k0 + _iota((n_slc, KEY_TILE), 1)) >> _log2(SLC_BLOCK)
            expand = jnp.where(_iota((n_slc, KEY_TILE), 0) == kblk, 1.0, 0.0).astype(BF16)
            picked = lax.dot_general(sel_t, expand, (((0,), (0,)), ((), ())), preferred_element_type=F32)
            dist = qpos_i - (k0 + _iota((1, KEY_TILE), 1))
            return jnp.where((picked > 0.5) & (dist >= 0), 0.0, NEG)

        def win_bias(k0):
            dist = qpos_i - (k0 + _iota((1, KEY_TILE), 1))
            return jnp.where((dist >= 0) & (dist < WINDOW), 0.0, NEG)

        o_s = _flash_branch(q_augs[hk], kf, kvs_ref, hk, 0, t_hi, slc_bias, m_scr, acc_scr, tq)
        o_w = _flash_branch(q_augs[hk], kf, kvw_ref, hk, t_lo_w, t_hi, win_bias, m_scr, acc_scr, tq)
        _gate_combine(sm, hk, o_cs[hk], o_s, o_w, o_ref, tq)


def _nsa_prompt(p3, kcvc, *, tq):
    bsz, seq, _ = p3.shape
    n_sub = kcvc.shape[1]
    assert seq % KEY_TILE == 0 and seq % tq == 0 and seq % SLC_BLOCK == 0 and KEY_TILE % tq == 0
    assert seq + CMP_BLOCK < 256 * 256, "positions are split into two bf16-exact parts"
    kern = functools.partial(_nsa_prompt_kernel, tq=tq, seq=seq, n_cmp=seq // CMP_STRIDE - 1)
    rows = NSA_GROUP * tq
    qsl = _slope_features(tq)
    kf = _position_features(jnp.arange(KEY_TILE))
    kfc = _position_features(jnp.arange(n_sub) * CMP_STRIDE + (CMP_BLOCK - 1))
    full = lambda a: pl.BlockSpec(a.shape, lambda b, j: (0,) * a.ndim)
    return pl.pallas_call(
        kern,
        grid=(bsz, seq // tq),
        in_specs=[pl.BlockSpec((1, tq, NSA_WIDTH), lambda b, j: (b, j, C_NQ // NSA_WIDTH)),
                  pl.BlockSpec((1, tq, LANES), lambda b, j: (b, j, C_SMALL // LANES)),
                  pl.BlockSpec((1, seq, KV_COLS), lambda b, j: (b, 0, C_KVS // KV_COLS)),
                  pl.BlockSpec((1, seq, KV_COLS), lambda b, j: (b, 0, C_KVW // KV_COLS)),
                  pl.BlockSpec((1, n_sub, KV_COLS), lambda b, j: (b, 0, 0)),
                  full(qsl), full(kf), full(kfc)],
        out_specs=pl.BlockSpec((1, tq, NSA_WIDTH), lambda b, j: (b, j, 0)),
        out_shape=jax.ShapeDtypeStruct((bsz, seq, NSA_WIDTH), F32),
        scratch_shapes=[pltpu.VMEM((rows, LANES), F32), pltpu.VMEM((rows, 2 * NSA_HEAD_DIM), F32)],
        compiler_params=_cparams(("parallel", "arbitrary")),
        name="nsa_prompt_attention",
    )(p3, p3, p3, p3, kcvc, qsl, kf, kfc)


def _nsa_select_kernel(q_ref, kc_ref, oc_ref, sel_ref, *, tq, past, n_cmp, n_slc, n_slc_pad):
    dh = NSA_HEAD_DIM
    n_sub = kc_ref.shape[1]
    qpos_i = past + _iota((tq, 1), 0)
    qpos = qpos_i.astype(F32)
    cidx = _iota((1, n_sub), 1)
    c_end = cidx * CMP_STRIDE + (CMP_BLOCK - 1)
    valid_c = (c_end <= qpos_i) & (cidx < n_cmp)
    dist_c = qpos - c_end.astype(F32)
    cr = _iota((n_sub, n_slc_pad), 0) * CMP_STRIDE
    s_st = _iota((n_sub, n_slc_pad), 1) * SLC_BLOCK
    cover = jnp.where((cr < s_st + SLC_BLOCK) & (cr + (CMP_BLOCK - 1) >= s_st), 1.0, 0.0)
    sidx = _iota((1, n_slc_pad), 1)
    sidx_f = sidx.astype(F32)
    cur = qpos_i >> _log2(SLC_BLOCK)
    forced = (sidx == 0) | (sidx == cur) | (sidx == cur - 1)
    bonus = jnp.where(forced, FORCE_BONUS, 0.0)
    past_ok = sidx * SLC_BLOCK <= qpos_i
    lane = _iota((1, LANES), 1)
    for hk in range(NSA_KV_HEADS):
        qs = _gather_heads(q_ref, hk)
        kc = kc_ref[0, :, hk * dh:(hk + 1) * dh]
        vc = kc_ref[0, :, NSA_KV_WIDTH + hk * dh:NSA_KV_WIDTH + (hk + 1) * dh]
        o_c, psum = _cmp_branch(qs, kc, vc, hk, valid_c, dist_c, tq)
        oc_ref[0, hk] = o_c
        imp = _hdot(psum, cover)
        score = jnp.where(past_ok, imp + bonus, NEG)
        score = jnp.where(sidx < n_slc, score, -jnp.inf)
        res = jnp.zeros((tq, LANES), F32)
        for it in range(min(N_SELECT, n_slc)):
            m = jnp.max(score, -1, keepdims=True)
            idx = jnp.min(jnp.where(score == m, sidx_f, 1e9), -1, keepdims=True)
            res = jnp.where(lane == it, idx, res)
            score = jnp.where(sidx_f == idx, -jnp.inf, score)
        sel_ref[0, hk] = res.astype(I32)


def _nsa_select(ps3, kcvc, *, past, n_cmp, n_slc):
    bsz, tq, _ = ps3.shape
    n_sub = kcvc.shape[1]
    n_slc_pad = -(-n_slc // LANES) * LANES
    kern = functools.partial(_nsa_select_kernel, tq=tq, past=past, n_cmp=n_cmp, n_slc=n_slc, n_slc_pad=n_slc_pad)
    rows = NSA_GROUP * tq
    return pl.pallas_call(
        kern,
        grid=(bsz,),
        in_specs=[pl.BlockSpec((1, tq, NSA_WIDTH), lambda b: (b, 0, C_NQ // NSA_WIDTH)),
                  pl.BlockSpec((1, n_sub, KV_COLS), lambda b: (b, 0, 0))],
        out_specs=[pl.BlockSpec((1, NSA_KV_HEADS, rows, NSA_HEAD_DIM), lambda b: (b, 0, 0, 0)),
                   pl.BlockSpec((1, NSA_KV_HEADS, tq, LANES), lambda b: (b, 0, 0, 0))],
        out_shape=[jax.ShapeDtypeStruct((bsz, NSA_KV_HEADS, rows, NSA_HEAD_DIM), F32),
                   jax.ShapeDtypeStruct((bsz, NSA_KV_HEADS, tq, LANES), I32)],
        compiler_params=_cparams(("parallel",)),
        name="nsa_sample_select",
    )(ps3, kcvc)


def _joint_softmax_pv(parts, hk, tq):
    g = NSA_GROUP
    outs = []
    for j in range(g):
        rs = slice(j * tq, (j + 1) * tq)
        slope = _slope(hk * g + j)
        ss = [jnp.where(valid, s_all[rs] - slope * dist, NEG) for s_all, valid, dist, _, _ in parts]
        m = None
        for s in ss:
            mi = jnp.max(s, -1, keepdims=True)
            m = mi if m is None else jnp.maximum(m, mi)
        num = None
        den = None
        for s, (_, valid, _, v, v_t) in zip(ss, parts):
            p = jnp.where(valid, jnp.exp(s - m), 0.0)
            d = jnp.sum(p, -1, keepdims=True)
            o = _bdot_nt(p, v) if v_t else _bdot(p, v)
            num = o if num is None else num + o
            den = d if den is None else den + d
        outs.append(num / jnp.maximum(den, 1e-30))
    return jnp.concatenate(outs, axis=0)


def _nsa_sample_kernel(phys_ref, q_ref, sm_ref, kpos_ref, tail_ref, wcache_ref, wnew_ref, oc_ref, cache_ref,
                       o_ref, kbuf, vbuf, sem, *, tq, t_valid, past, n_gather):
    dh = NSA_HEAD_DIM
    b = pl.program_id(0)
    per_b = NSA_KV_HEADS * n_gather

    def page_copies(page, hk, i):
        dst = pl.ds(i * PAGE_SIZE, PAGE_SIZE)
        return (pltpu.make_async_copy(cache_ref.at[page, pl.ds(hk * dh, dh), :], kbuf.at[hk, :, dst], sem),
                pltpu.make_async_copy(cache_ref.at[page, pl.ds(NSA_KV_WIDTH + hk * dh, dh), :], vbuf.at[hk, :, dst], sem))

    for hk in range(NSA_KV_HEADS):
        for i in range(n_gather):
            for cp in page_copies(phys_ref[b * per_b + hk * n_gather + i], hk, i):
                cp.start()

    qpos_i = past + _iota((tq, 1), 0)
    qpos = qpos_i.astype(F32)
    sm = sm_ref[0]
    n_keys = n_gather * PAGE_SIZE
    per_q = n_keys // t_valid
    new_ok = _iota((1, tq), 1) < t_valid
    dist_new = qpos - (past + _iota((1, tq), 1)).astype(F32)
    n_win = wcache_ref.shape[2]
    dist_wc = qpos - (past - n_win + _iota((1, n_win), 1)).astype(F32)
    ok_wc = (dist_wc >= 0.0) & (dist_wc < float(WINDOW))
    ok_wn = (dist_new >= 0.0) & (dist_new < float(WINDOW)) & new_ok

    qss = [_gather_heads(q_ref, hk) for hk in range(NSA_KV_HEADS)]
    win = []
    for hk in range(NSA_KV_HEADS):
        kw_t = wcache_ref[0, hk * dh:(hk + 1) * dh, :]
        vw_t = wcache_ref[0, NSA_KV_WIDTH + hk * dh:NSA_KV_WIDTH + (hk + 1) * dh, :]
        kn = wnew_ref[0, :, hk * dh:(hk + 1) * dh]
        vn = wnew_ref[0, :, NSA_KV_WIDTH + hk * dh:NSA_KV_WIDTH + (hk + 1) * dh]
        win.append(_joint_softmax_pv([(_bdot(qss[hk], kw_t), ok_wc, dist_wc, vw_t, True),
                                      (_bdot_nt(qss[hk], kn), ok_wn, dist_new, vn, False)], hk, tq))

    for hk in range(NSA_KV_HEADS):
        for i in range(n_gather):
            for cp in page_copies(0, hk, i):
                cp.wait()

    g = NSA_GROUP
    grow = _iota((g, g * tq), 0)
    gcol = _iota((g, g * tq), 1)
    gi = _iota((g, 1), 0)
    tcol = _iota((1, tq), 1)
    probs = [(hk, q) for hk in range(NSA_KV_HEADS) for q in range(t_valid)]
    slopes = []
    for hk in range(NSA_KV_HEADS):
        slope = jnp.zeros((g, 1), F32)
        for j in range(g):
            slope = jnp.where(gi == j, _slope(hk * g + j), slope)
        slopes.append(slope)
    kts = [tail_ref[0, :, hk * dh:(hk + 1) * dh] for hk in range(NSA_KV_HEADS)]
    vts = [tail_ref[0, :, NSA_KV_WIDTH + hk * dh:NSA_KV_WIDTH + (hk + 1) * dh] for hk in range(NSA_KV_HEADS)]
    picks = [jnp.where(gcol == grow * tq + q, 1.0, 0.0) for _, q in probs]
    q8s = [_hdot(pick, qss[hk]) for pick, (hk, _) in zip(picks, probs)]
    raw_p = [_bdot(q8, kbuf[hk, :, q * per_q:(q + 1) * per_q]) for q8, (hk, q) in zip(q8s, probs)]
    raw_t = [_bdot_nt(q8, kts[hk]) for q8, (hk, _) in zip(q8s, probs)]
    pps, pts, dens = [], [], []
    for (hk, q), rp, rt in zip(probs, raw_p, raw_t):
        dist_p = float(past + q) - kpos_ref[0, hk][:, q * per_q:(q + 1) * per_q]
        dist_t = (q - tcol).astype(F32)
        ok_t = (dist_t >= 0.0) & new_ok
        s_p = jnp.where(dist_p >= 0.0, rp - slopes[hk] * dist_p, NEG)
        s_t = jnp.where(ok_t, rt - slopes[hk] * dist_t, NEG)
        m = jnp.maximum(jnp.max(s_p, -1, keepdims=True), jnp.max(s_t, -1, keepdims=True))
        p_p = jnp.where(dist_p >= 0.0, jnp.exp(s_p - m), 0.0)
        p_t = jnp.where(ok_t, jnp.exp(s_t - m), 0.0)
        pps.append(p_p)
        pts.append(p_t)
        dens.append(jnp.sum(p_p, -1, keepdims=True) + jnp.sum(p_t, -1, keepdims=True))
    o_qs = [(_bdot_nt(p_p, vbuf[hk, :, q * per_q:(q + 1) * per_q]) + _bdot(p_t, vts[hk])) / jnp.maximum(den, 1e-30)
            for (hk, q), p_p, p_t, den in zip(probs, pps, pts, dens)]
    backs = [_hdot_tn(pick, o_q) for pick, o_q in zip(picks, o_qs)]
    for hk in range(NSA_KV_HEADS):
        o_s = backs[hk * t_valid]
        for q in range(1, t_valid):
            o_s = o_s + backs[hk * t_valid + q]
        _gate_combine(sm, hk, oc_ref[0, hk], o_s, win[hk], o_ref, tq)


def _nsa_sample(ps3, o_c, phys, kpos, cache_t, win_t, *, t_valid, past):
    bsz, tq, _ = ps3.shape
    n_gather = t_valid * N_SELECT
    rows = NSA_GROUP * tq
    n_keys = n_gather * PAGE_SIZE
    kern = functools.partial(_nsa_sample_kernel, tq=tq, t_valid=t_valid, past=past, n_gather=n_gather)
    return pl.pallas_call(
        kern,
        grid_spec=pltpu.PrefetchScalarGridSpec(
            num_scalar_prefetch=1,
            grid=(bsz,),
            in_specs=[pl.BlockSpec((1, tq, NSA_WIDTH), lambda b, ph: (b, 0, C_NQ // NSA_WIDTH)),
                      pl.BlockSpec((1, tq, LANES), lambda b, ph: (b, 0, C_SMALL // LANES)),
                      pl.BlockSpec((1, NSA_KV_HEADS, 1, n_keys), lambda b, ph: (b, 0, 0, 0)),
                      pl.BlockSpec((1, tq, KV_COLS), lambda b, ph: (b, 0, C_KVS // KV_COLS)),
                      pl.BlockSpec((1,) + win_t.shape[1:], lambda b, ph: (b, 0, 0)),
                      pl.BlockSpec((1, tq, KV_COLS), lambda b, ph: (b, 0, C_KVW // KV_COLS)),
                      pl.BlockSpec((1, NSA_KV_HEADS, rows, NSA_HEAD_DIM), lambda b, ph: (b, 0, 0, 0)),
                      pl.BlockSpec(memory_space=pl.ANY)],
            out_specs=pl.BlockSpec((1, tq, NSA_WIDTH), lambda b, ph: (b, 0, 0)),
            scratch_shapes=[pltpu.VMEM((NSA_KV_HEADS, NSA_HEAD_DIM, n_keys), F32),
                            pltpu.VMEM((NSA_KV_HEADS, NSA_HEAD_DIM, n_keys), F32),
                            pltpu.SemaphoreType.DMA(())]),
        out_shape=jax.ShapeDtypeStruct((bsz, tq, NSA_WIDTH), F32),
        compiler_params=_cparams(("arbitrary",)),
        name="nsa_sample_attention",
    )(phys, ps3, ps3, kpos, ps3, win_t, ps3, o_c, cache_t)


def _rows_transposed(cache):
    nd = cache.ndim
    perm = tuple(range(nd - 4)) + (nd - 3, nd - 2, nd - 1, nd - 4)
    t = jnp.transpose(cache, perm)
    return t.reshape(t.shape[:nd - 4] + (KV_COLS, cache.shape[nd - 4]))


def _prompt_mixers(x, w_r, conv_w, a_log, dt_bias, norm_w, cmp_wf, cmp_w2bd, cmp_pos, cmp_w1, tl):
    bsz, seq, _ = x.shape
    p = _matmul(x.reshape(bsz * seq, D_MODEL), w_r, tl["proj_tm"], tl["proj_tn"])
    p3 = p.reshape(bsz, seq, P_COLS)
    hist = jnp.zeros((bsz, 8, 3 * DN_WIDTH), F32)
    s0 = jnp.zeros((bsz, DN_HEADS, DN_HEAD_DIM, DN_HEAD_DIM), F32)
    o_dn, s_new = _deltanet(p3, hist, s0, conv_w, a_log, dt_bias, norm_w, t_valid=seq, tb=tl["dn_tb"], c=DN_CHUNK,
                            hp=tl["dn_heads"])
    assert seq % CMP_STRIDE == 0
    kcvc = _cmp_epilogue(_cmp_rows(p3, cmp_wf), cmp_pos.reshape(2, -1), cmp_w1, cmp_w2bd)
    o_nsa = _nsa_prompt(p3, kcvc, tq=tl["nsa_tq"])
    return p3, o_dn, s_new, o_nsa


def _sample_mixers(x, cache_cmp, cache_slc, win_buf, s0, conv_buf, page_table, w_r, conv_w, a_log, dt_bias,
                   norm_w, cmp_wf, cmp_w2bd, cmp_pos, cmp_w1, tl):
    bsz, t, _ = x.shape
    tq = 8
    n_pages = page_table.shape[1]
    past = n_pages * PAGE_SIZE
    assert t <= tq and t <= SLC_BLOCK and past % SLC_BLOCK == 0 and cache_cmp.shape[1] == PAGE_SIZE
    assert (past + t) // CMP_STRIDE * CMP_STRIDE == past, "new rows never complete a compression sub-block"
    ps = _matmul(x.reshape(bsz * t, D_MODEL), w_r, bsz * t, tl["proj_tn"]).reshape(bsz, t, P_COLS)
    ps3 = jnp.pad(ps, ((0, 0), (0, tq - t), (0, 0)))
    hist = jnp.pad(conv_buf, ((0, 0), (8 - (CONV_W - 1), 0), (0, 0)))
    o_dn, s_new = _deltanet(ps3, hist, s0, conv_w, a_log, dt_bias, norm_w, t_valid=t, tb=tq, c=tq, hp=DN_HEADS)
    n_sub = past // CMP_STRIDE
    pmat = _cmp_paged(_rows_transposed(cache_cmp), page_table, cmp_wf, npg=tl["cmp_pages"])
    kcvc = _cmp_epilogue(pmat.reshape(bsz, n_sub, -1), cmp_pos.reshape(2, -1), cmp_w1, cmp_w2bd)
    n_past_blocks = past // SLC_BLOCK
    o_c, sel = _nsa_select(ps3, kcvc, past=past, n_cmp=n_sub - 1, n_slc=n_past_blocks + 1)
    sel = sel[:, :, :t, :N_SELECT]
    bpp = PAGE_SIZE // SLC_BLOCK
    jp = jnp.minimum(sel, n_past_blocks - 1)
    page = jp // bpp
    phys = page_table[jnp.arange(bsz)[:, None, None, None], page]
    row = jnp.arange(PAGE_SIZE)
    in_blk = (row // SLC_BLOCK == (jp % bpp)[..., None]) & (sel < n_past_blocks)[..., None]
    kpos = jnp.where(in_blk, (page[..., None] * PAGE_SIZE + row).astype(F32), 1e9)
    kpos = kpos.reshape(bsz, NSA_KV_HEADS, 1, t * N_SELECT * PAGE_SIZE)
    o_nsa = _nsa_sample(ps3, o_c, phys.reshape(-1).astype(I32), kpos, _rows_transposed(cache_slc),
                        _rows_transposed(win_buf), t_valid=t, past=past)
    return ps, o_dn, s_new, o_nsa


def _layer_norm(x, g, b):
    xc = x - jnp.mean(x, -1, keepdims=True)
    var = jnp.mean(xc * xc, -1, keepdims=True)
    return xc * lax.rsqrt(var + LN_EPS) * g + b


def _rank_rows(v, n):
    ri = _iota(v.shape, 0)
    rank = jnp.zeros(v.shape, F32)
    for rp in range(n):
        row = v[rp:rp + 1, :]
        beats = (row > v) | ((row == v) & (rp < ri))
        rank = rank + jnp.where(beats, 1.0, 0.0)
    return rank


def _post_mixer_kernel(x_ref, odn_ref, onsa_ref, gdn_ref, gnsa_ref, wo_ref, g_ref, b_ref, wr_ref, br_ref,
                       x1_ref, xp_ref, idx_ref, wt_ref, pos_ref, cnt_ref, run_scr, *, tm, alpha):
    i = pl.program_id(0)

    @pl.when(i == 0)
    def _():
        run_scr[...] = jnp.zeros(run_scr.shape, F32)

    h = _sigmoid(gdn_ref[0]) * odn_ref[...] + _sigmoid(gnsa_ref[0]) * onsa_ref[...]
    x1 = _layer_norm(alpha * x_ref[...] + _bdot(h, wo_ref[...]), g_ref[...], b_ref[...])
    x1_ref[...] = x1
    bits = pltpu.bitcast(x1.astype(BF16).astype(F32), jnp.uint32)
    half = x1.shape[1] // 2
    xp_ref[...] = (bits[:, :half] >> 16) | (bits[:, half:] & jnp.uint32(0xFFFF0000))

    ne = N_EXPERTS
    per = ne // N_GROUPS
    scores = _sigmoid(_hdot_nt(wr_ref[...], x1))
    s3 = (scores + br_ref[...]).reshape(N_GROUPS, per, tm)
    e3 = _iota((N_GROUPS, per, tm), 1).astype(F32)
    g1 = jnp.max(s3, axis=1, keepdims=True)
    first = jnp.min(jnp.where(s3 == g1, e3, float(per)), axis=1, keepdims=True)
    g2 = jnp.max(jnp.where(e3 == first, -jnp.inf, s3), axis=1, keepdims=True)
    grank = _rank_rows((g1 + g2).reshape(N_GROUPS, tm), N_GROUPS)
    keep = (grank < TOPK_GROUPS).reshape(N_GROUPS, 1, tm)
    selm = jnp.where(keep, s3, NEG).reshape(ne, tm)
    erank = _rank_rows(selm, ne)
    ei = _iota((ne, tm), 0).astype(F32)
    chosen = jnp.where(erank < TOP_K, 1.0, 0.0)
    tr = _iota((tm, tm), 0)
    tc = _iota((tm, tm), 1)
    before = jnp.where(tr < tc, 1.0, 0.0)
    pos_full = _bdot(chosen, before) + run_scr[:, 0:1]
    idx_rows, w_rows, pos_rows = [], [], []
    for k in range(TOP_K):
        hit = erank == float(k)
        idx_rows.append(jnp.sum(jnp.where(hit, ei, 0.0), 0, keepdims=True))
        w_rows.append(jnp.sum(jnp.where(hit, scores, 0.0), 0, keepdims=True))
        pos_rows.append(jnp.sum(jnp.where(hit, pos_full, 0.0), 0, keepdims=True))
    wsum = w_rows[0]
    for k in range(1, TOP_K):
        wsum = wsum + w_rows[k]
    zero = jnp.zeros((8 - TOP_K, tm), F32)
    idx_ref[...] = jnp.concatenate(idx_rows + [zero], 0).astype(I32)
    wt_ref[...] = jnp.concatenate([w / wsum * ROUTED_SCALE for w in w_rows] + [zero], 0)
    pos_ref[...] = jnp.concatenate(pos_rows + [zero], 0).astype(I32)
    run_scr[...] = run_scr[...] + jnp.sum(chosen, 1, keepdims=True)
    cnt_ref[...] = run_scr[...]


def _post_mixer(x, o_dn, o_nsa, p3, w_out_bf16, ln_g, ln_b, w_router_t, b_router, *, tm, alpha):
    n, d = x.shape
    assert n % tm == 0
    bsz, seq, _ = p3.shape
    assert seq % tm == 0 or tm % seq == 0
    if seq % tm == 0:
        per_b = seq // tm
        gspec = lambda c: pl.BlockSpec((1, tm, d), lambda i: (i // per_b, i % per_b, c))
        p_in = p3
    else:
        p_in = p3.reshape(1, n, P_COLS)
        gspec = lambda c: pl.BlockSpec((1, tm, d), lambda i: (0, i, c))
    tok = pl.BlockSpec((tm, d), lambda i: (i, 0))
    full = lambda a: pl.BlockSpec(a.shape, lambda i: (0,) * a.ndim)
    rt = pl.BlockSpec((8, tm), lambda i: (0, i))
    kern = functools.partial(_post_mixer_kernel, tm=tm, alpha=alpha)
    g2 = ln_g.reshape(1, d)
    b2 = ln_b.reshape(1, d)
    br = b_router.reshape(N_EXPERTS, 1)
    return pl.pallas_call(
        kern,
        grid=(n // tm,),
        in_specs=[tok, tok, tok, gspec(C_MG // d), gspec(C_MG // d + 1), full(w_out_bf16), full(g2), full(b2),
                  full(w_router_t), full(br)],
        out_specs=[tok, pl.BlockSpec((tm, d // 2), lambda i: (i, 0)), rt, rt, rt,
                   pl.BlockSpec((N_EXPERTS, LANES), lambda i: (0, 0))],
        out_shape=[jax.ShapeDtypeStruct((n, d), F32), jax.ShapeDtypeStruct((n, d // 2), jnp.uint32),
                   jax.ShapeDtypeStruct((8, n), I32),
                   jax.ShapeDtypeStruct((8, n), F32), jax.ShapeDtypeStruct((8, n), I32),
                   jax.ShapeDtypeStruct((N_EXPERTS, LANES), F32)],
        scratch_shapes=[pltpu.VMEM((N_EXPERTS, LANES), F32)],
        compiler_params=_cparams(("arbitrary",)),
        name="merge_outproj_ln_router",
    )(x, o_dn, o_nsa, p_in, p_in, w_out_bf16, g2, b2, w_router_t, br)


def _slot_kernel(ps_ref, idx_ref, pos_ref, slot_ref):
    idx = idx_ref[...]
    acc = pos_ref[...]
    for e in range(N_EXPERTS):
        acc = acc + jnp.where(idx == e, ps_ref[e], 0)
    slot_ref[...] = jnp.where(_iota(idx.shape, 0) < TOP_K, acc, 0)


def _slots(pad_start, idx, pos):
    n = idx.shape[1]
    blk = pl.BlockSpec((8, n), lambda i, ps: (0, 0))
    return pl.pallas_call(
        _slot_kernel,
        grid_spec=pltpu.PrefetchScalarGridSpec(num_scalar_prefetch=1, grid=(1,), in_specs=[blk, blk], out_specs=blk),
        out_shape=jax.ShapeDtypeStruct((8, n), I32),
        compiler_params=_cparams(("arbitrary",)),
        name="moe_slots",
    )(pad_start, idx, pos)


def _dispatch_kernel(zb_ref, nu_ref, slot_ref, x_ref, xs_ref, zero_scr, sem, zsem, *, tm, blk):
    @pl.when(pl.program_id(0) == 0)
    def _():
        zero_scr[...] = jnp.zeros(zero_scr.shape, zero_scr.dtype)
        n_blocks = xs_ref.shape[0] // blk

        def zero_copy(b):
            return pltpu.make_async_copy(zero_scr, xs_ref.at[pl.ds(pl.multiple_of(b * blk, blk), blk)], zsem)

        for e in range(N_EXPERTS):
            zero_copy(zb_ref[e]).start()
        for e in range(N_EXPERTS):
            zero_copy(0).wait()
        for t in range(n_blocks - N_EXPERTS, n_blocks):
            @pl.when(t >= nu_ref[0])
            def _(t=t):
                zero_copy(t).start()
        for t in range(n_blocks - N_EXPERTS, n_blocks):
            @pl.when(t >= nu_ref[0])
            def _():
                zero_copy(0).wait()

    def row_copy(r, s):
        return pltpu.make_async_copy(x_ref.at[pl.ds(r, 1)], xs_ref.at[pl.ds(s, 1)], sem)

    def issue(r, carry):
        for k in range(TOP_K):
            row_copy(r, slot_ref[k, r]).start()
        return carry

    lax.fori_loop(0, tm, issue, 0, unroll=8)

    def drain(r, carry):
        for k in range(TOP_K):
            row_copy(0, 0).wait()
        return carry

    lax.fori_loop(0, tm, drain, 0, unroll=8)


def _dispatch(x1, slot, zero_blocks, n_used, n_slots, *, tm, blk):
    n, d = x1.shape
    assert n % tm == 0 and n_slots % blk == 0 and n_slots // blk >= N_EXPERTS
    kern = functools.partial(_dispatch_kernel, tm=tm, blk=blk)
    return pl.pallas_call(
        kern,
        grid_spec=pltpu.PrefetchScalarGridSpec(
            num_scalar_prefetch=2,
            grid=(n // tm,),
            in_specs=[pl.BlockSpec((8, tm), lambda i, zb, nu: (0, i), memory_space=pltpu.SMEM),
                      pl.BlockSpec((tm, d), lambda i, zb, nu: (i, 0))],
            out_specs=pl.BlockSpec(memory_space=pl.ANY),
            scratch_shapes=[pltpu.VMEM((blk, d), x1.dtype), pltpu.SemaphoreType.DMA(()),
                            pltpu.SemaphoreType.DMA(())]),
        out_shape=jax.ShapeDtypeStruct((n_slots, d), x1.dtype),
        compiler_params=_cparams(("arbitrary",)),
        name="moe_dispatch",
    )(zero_blocks, n_used, slot, x1)


def _expert_kernel(be_ref, nu_ref, x_ref, wg_ref, wu_ref, wd_ref, y_ref, wg_b, wu_b, wd_b):
    i = pl.program_id(0)

    @pl.when((i == 0) | (be_ref[i] != be_ref[jnp.maximum(i - 1, 0)]))
    def _():
        wg_b[...] = wg_ref[0].astype(BF16)
        wu_b[...] = wu_ref[0].astype(BF16)
        wd_b[...] = wd_ref[0].astype(BF16)

    @pl.when(i < nu_ref[0])
    def _():
        w = x_ref[...]
        x = jnp.concatenate([pltpu.bitcast(w << 16, F32), pltpu.bitcast(w & jnp.uint32(0xFFFF0000), F32)],
                            axis=1).astype(BF16)
        hg = jnp.dot(x, wg_b[...], preferred_element_type=F32)
        hu = jnp.dot(x, wu_b[...], preferred_element_type=F32)
        y_ref[...] = jnp.dot((_silu(hg) * hu).astype(BF16), wd_b[...], preferred_element_type=F32)

    @pl.when(i >= nu_ref[0])
    def _():
        y_ref[...] = jnp.zeros(y_ref.shape, F32)


def _experts(xs, blk_exp, n_used, w_gate, w_up, w_down, *, blk):
    n_slots, dpk = xs.shape
    d, de = w_gate.shape[1:]
    assert dpk * 2 == d
    n_blocks = n_slots // blk
    return pl.pallas_call(
        _expert_kernel,
        grid_spec=pltpu.PrefetchScalarGridSpec(
            num_scalar_prefetch=2,
            grid=(n_blocks,),
            in_specs=[pl.BlockSpec((blk, dpk), lambda i, be, nu: (jnp.maximum(jnp.minimum(i, nu[0] - 1), 0), 0)),
                      pl.BlockSpec((1, d, de), lambda i, be, nu: (be[i], 0, 0)),
                      pl.BlockSpec((1, d, de), lambda i, be, nu: (be[i], 0, 0)),
                      pl.BlockSpec((1, de, d), lambda i, be, nu: (be[i], 0, 0))],
            out_specs=pl.BlockSpec((blk, d), lambda i, be, nu: (i, 0)),
            scratch_shapes=[pltpu.VMEM((d, de), BF16), pltpu.VMEM((d, de), BF16), pltpu.VMEM((de, d), BF16)]),
        out_shape=jax.ShapeDtypeStruct((n_slots, d), F32),
        compiler_params=_cparams(("arbitrary",)),
        name="moe_experts",
    )(blk_exp, n_used, xs, w_gate, w_up, w_down)


def _combine_kernel(slot_ref, x_ref, w_ref, ys_ref, wsg_ref, wsu_ref, wsd_ref, g_ref, b_ref, o_ref, buf, sem,
                    *, tm, alpha):
    def row_copy(s, k, r):
        return pltpu.make_async_copy(ys_ref.at[pl.ds(s, 1)], buf.at[k, pl.ds(r, 1)], sem)

    def issue(r, carry):
        for k in range(TOP_K):
            row_copy(slot_ref[k, r], k, r).start()
        return carry

    lax.fori_loop(0, tm, issue, 0, unroll=8)
    x = x_ref[...]
    xb = x.astype(BF16)
    hs = _silu(jnp.dot(xb, wsg_ref[...], preferred_element_type=F32)) * jnp.dot(xb, wsu_ref[...],
                                                                               preferred_element_type=F32)
    acc = alpha * x + _bdot(hs, wsd_ref[...])

    def drain(r, carry):
        for k in range(TOP_K):
            row_copy(0, k, 0).wait()
        return carry

    lax.fori_loop(0, tm, drain, 0, unroll=8)
    w = w_ref[...]
    for k in range(TOP_K):
        acc = acc + w[:, k:k + 1] * buf[k]
    o_ref[...] = _layer_norm(acc, g_ref[...], b_ref[...])


def _combine(x1, slot, w_tok, ys, ws_gate, ws_up, ws_down, ln_g, ln_b, *, tm, alpha):
    n, d = x1.shape
    assert n % tm == 0
    kern = functools.partial(_combine_kernel, tm=tm, alpha=alpha)
    full = lambda a: pl.BlockSpec(a.shape, lambda i: (0,) * a.ndim)
    g2 = ln_g.reshape(1, d)
    b2 = ln_b.reshape(1, d)
    return pl.pallas_call(
        kern,
        grid=(n // tm,),
        in_specs=[pl.BlockSpec((8, tm), lambda i: (0, i), memory_space=pltpu.SMEM),
                  pl.BlockSpec((tm, d), lambda i: (i, 0)),
                  pl.BlockSpec((tm, 8), lambda i: (i, 0)),
                  pl.BlockSpec(memory_space=pl.ANY),
                  full(ws_gate), full(ws_up), full(ws_down), full(g2), full(b2)],
        out_specs=pl.BlockSpec((tm, d), lambda i: (i, 0)),
        out_shape=jax.ShapeDtypeStruct((n, d), F32),
        scratch_shapes=[pltpu.VMEM((TOP_K, tm, d), F32), pltpu.SemaphoreType.DMA(())],
        compiler_params=_cparams(("arbitrary",)),
        name="moe_combine_ln",
    )(slot, x1, w_tok, ys, ws_gate, ws_up, ws_down, g2, b2)


def _moe_layer(x1, xp, idx, wts, pos, counts, w_gate, w_up, w_down, ws_gate, ws_up, ws_down, ln_g, ln_b,
               *, blk, tm_d, tm_c, alpha):
    n = x1.shape[0]
    cnt = counts[:, 0].astype(I32)
    padded = (cnt + blk - 1) // blk * blk
    pad_end = jnp.cumsum(padded)
    slot = _slots((pad_end - padded).astype(I32), idx, pos)
    n_blocks = -(-(n * TOP_K) // blk) + N_EXPERTS
    blk_exp = jnp.minimum(jnp.sum(pad_end[None, :] <= (jnp.arange(n_blocks) * blk)[:, None], axis=1),
                          N_EXPERTS - 1).astype(I32)
    n_used = (pad_end[-1:] // blk).astype(I32)
    empty = padded == 0
    zero_blocks = jnp.where(empty, n_blocks - jnp.cumsum(empty), pad_end // blk - 1).astype(I32)
    xs = _dispatch(xp, slot, zero_blocks, n_used, n_blocks * blk, tm=tm_d, blk=blk)
    ys = _experts(xs, blk_exp, n_used, w_gate, w_up, w_down, blk=blk)
    return _combine(x1, slot, wts.T, ys, ws_gate.astype(BF16), ws_up.astype(BF16), ws_down.astype(BF16),
                    ln_g, ln_b, tm=tm_c, alpha=alpha)


def kernel(x_prompt, x_sample, cache_cmp_kv, cache_slc_kv, cache_win_kv, state_delta_S, state_delta_conv, page_table, w_in, dn_conv_w, dn_A_log, dn_dt_bias, dn_norm_w, nsa_cmp_w1, nsa_cmp_pos, nsa_cmp_w2, w_out, ln1_g, ln1_b, w_router, b_router, w_exp_gate, w_exp_up, w_exp_down, w_sh_gate, w_sh_up, w_sh_down, ln2_g, ln2_b):
    depth = w_in.shape[0]
    assert depth == 1
    alpha = (2.0 * depth) ** 0.25
    bsz, seq, d = x_prompt.shape
    sb, st, _ = x_sample.shape
    tl = _tiles(bsz * seq, seq, sb * st)
    w_r = _reorder_w_in(w_in[0])
    wf, w2bd = _cmp_weights(nsa_cmp_w1[0], nsa_cmp_w2[0])
    mix_w = (w_r, dn_conv_w[0], dn_A_log[0], dn_dt_bias[0], dn_norm_w[0], wf, w2bd, nsa_cmp_pos[0], nsa_cmp_w1[0])
    p3, o_dn, s_p, o_nsa = _prompt_mixers(x_prompt, *mix_w, tl)
    ps, o_dn_s, s_s, o_nsa_s = _sample_mixers(x_sample, cache_cmp_kv[0], cache_slc_kv[0], cache_win_kv[0],
                                              state_delta_S[0], state_delta_conv[0], page_table, *mix_w, tl)
    wo = w_out[0].astype(BF16)
    wrt = w_router[0].T

    def ffn(x2, o_dn2, o_nsa2, p_any, tm, blk, tm_d, tm_c):
        x1, xp, idx, wts, pos, counts = _post_mixer(x2, o_dn2, o_nsa2, p_any, wo, ln1_g[0], ln1_b[0], wrt,
                                                    b_router[0], tm=tm, alpha=alpha)
        return _moe_layer(x1, xp, idx, wts, pos, counts, w_exp_gate[0], w_exp_up[0], w_exp_down[0],
                          w_sh_gate[0], w_sh_up[0], w_sh_down[0], ln2_g[0], ln2_b[0],
                          blk=blk, tm_d=tm_d, tm_c=tm_c, alpha=alpha)

    y_p = ffn(x_prompt.reshape(-1, d), o_dn.reshape(-1, d), o_nsa.reshape(-1, d), p3,
              tl["post_tm"], tl["moe_blk"], tl["moe_tm_dispatch"], tl["moe_tm_combine"])
    y_s = ffn(x_sample.reshape(-1, d), o_dn_s[:, :st].reshape(-1, d), o_nsa_s[:, :st].reshape(-1, d), ps,
              tl["sample_tm"], tl["sample_moe_blk"], tl["sample_tm"], tl["sample_tm"])

    kv_shape = (2, NSA_KV_HEADS, NSA_HEAD_DIM)

    def kv_rows(pp, c0):
        return pp[:, :, c0:c0 + KV_COLS].reshape(pp.shape[:2] + kv_shape)

    nconv = CONV_W - 1
    conv_p = jnp.concatenate([jnp.zeros((bsz, nconv, 3 * DN_WIDTH), F32), p3[:, :, :3 * DN_WIDTH]], 1)[:, -nconv:]
    conv_s = jnp.concatenate([state_delta_conv[0], ps[:, :, :3 * DN_WIDTH]], 1)[:, -nconv:]
    past = page_table.shape[1] * PAGE_SIZE
    win_s = jnp.concatenate([cache_win_kv[0], kv_rows(ps, C_KVW)], 1)[:, -min(WINDOW, past + st):]
    return (y_p.reshape(x_prompt.shape), y_s.reshape(x_sample.shape),
            kv_rows(p3, C_KVC)[None], kv_rows(p3, C_KVS)[None], kv_rows(p3, C_KVW)[:, -min(WINDOW, seq):][None],
            s_p[None], conv_p[None],
            kv_rows(ps, C_KVC)[None], kv_rows(ps, C_KVS)[None], win_s[None], s_s[None], conv_s[None])
```

```python
import functools
import math

import jax
import jax.numpy as jnp
import numpy as np
from jax import lax
from jax.experimental import pallas as pl
from jax.experimental.pallas import tpu as pltpu

F32 = jnp.float32
BF16 = jnp.bfloat16
I32 = jnp.int32
HIGHEST = lax.Precision.HIGHEST

D_MODEL = 1024
PAGE_SIZE = 128
DN_HEADS = 8
DN_HEAD_DIM = 128
DN_WIDTH = DN_HEADS * DN_HEAD_DIM
CONV_W = 4
DN_CHUNK = 64
NSA_HEADS = 16
NSA_KV_HEADS = 2
NSA_GROUP = NSA_HEADS // NSA_KV_HEADS
NSA_HEAD_DIM = 64
NSA_WIDTH = NSA_HEADS * NSA_HEAD_DIM
NSA_KV_WIDTH = NSA_KV_HEADS * NSA_HEAD_DIM
KV_COLS = 2 * NSA_KV_WIDTH
CMP_BLOCK = 32
CMP_STRIDE = 16
SLC_BLOCK = 64
N_SELECT = 16
WINDOW = 512
N_EXPERTS = 64
TOP_K = 6
N_GROUPS = 8
TOPK_GROUPS = 4
ROUTED_SCALE = 2.5
LN_EPS = 1e-5
RMS_EPS = 1e-6
NEG = -1e30
LOG2E = math.log2(math.e)
FORCE_BONUS = 1e6

C_QKV = 0
C_Z = 3072
C_NQ = 4096
C_MG = 5120
C_KVC = 7168
C_KVS = 7424
C_KVW = 7680
C_SMALL = 7936
P_COLS = 8064
SM_A = 0
SM_B = DN_HEADS
SM_NG = 2 * DN_HEADS

LANES = 128
VMEM_LIMIT = 48 * 1024 * 1024
KEY_TILE = 256


def _tiles(n_prompt_tokens, seq, n_sample_tokens):
    return dict(
        proj_tm=min(512, n_prompt_tokens), proj_tn=P_COLS // 3,
        dn_tb=min(256, seq), dn_heads=8,
        nsa_tq=256,
        cmp_pages=32,
        post_tm=min(256, n_prompt_tokens),
        moe_blk=512, moe_tm_dispatch=min(256, n_prompt_tokens), moe_tm_combine=min(128, n_prompt_tokens),
        sample_moe_blk=64, sample_tm=n_sample_tokens,
    )


def _cparams(sem):
    return pltpu.CompilerParams(dimension_semantics=sem, vmem_limit_bytes=VMEM_LIMIT)


def _bdot(a, b):
    return jnp.dot(a.astype(BF16), b.astype(BF16), preferred_element_type=F32)


def _bdot_nt(a, b):
    return lax.dot_general(a.astype(BF16), b.astype(BF16), (((1,), (1,)), ((), ())),
                           preferred_element_type=F32)


def _bdot_tn(a, b):
    return lax.dot_general(a.astype(BF16), b.astype(BF16), (((0,), (0,)), ((), ())),
                           preferred_element_type=F32)


def _hdot(a, b):
    return jnp.dot(a, b, precision=HIGHEST, preferred_element_type=F32)


def _hdot_tn(a, b):
    return lax.dot_general(a, b, (((0,), (0,)), ((), ())), precision=HIGHEST,
                           preferred_element_type=F32)


def _hdot_nt(a, b):
    return lax.dot_general(a, b, (((1,), (1,)), ((), ())), precision=HIGHEST,
                           preferred_element_type=F32)


def _sigmoid(x):
    return 1.0 / (1.0 + jnp.exp(-x))


def _silu(x):
    return x * _sigmoid(x)


def _softplus(x):
    return jnp.maximum(x, 0.0) + jnp.log(1.0 + jnp.exp(-jnp.abs(x)))


def _iota(shape, dim):
    return lax.broadcasted_iota(I32, shape, dim)


def _log2(n):
    assert n & (n - 1) == 0
    return int(math.log2(n))


def _mm_kernel(x_ref, w_ref, o_ref):
    o_ref[...] = jnp.dot(x_ref[...].astype(BF16), w_ref[...], preferred_element_type=F32)


def _matmul(x, w_bf16, tm, tn):
    m, k = x.shape
    n = w_bf16.shape[1]
    assert m % tm == 0 and n % tn == 0
    return pl.pallas_call(
        _mm_kernel,
        grid=(n // tn, m // tm),
        in_specs=[pl.BlockSpec((tm, k), lambda j, i: (i, 0)),
                  pl.BlockSpec((k, tn), lambda j, i: (0, j))],
        out_specs=pl.BlockSpec((tm, tn), lambda j, i: (i, j)),
        out_shape=jax.ShapeDtypeStruct((m, n), F32),
        compiler_params=_cparams(("parallel", "parallel")),
        name="dense_matmul",
    )(x, w_bf16)


def _reorder_w_in(w_in):
    o = 0
    seg = {}
    for name, size in (("qkv", 3 * DN_WIDTH), ("z", DN_WIDTH), ("a", DN_HEADS), ("b", DN_HEADS),
                       ("nq", NSA_WIDTH), ("kvc", KV_COLS), ("kvs", KV_COLS),
                       ("kvw", KV_COLS), ("ng", 3 * NSA_HEADS), ("mg", 2 * D_MODEL)):
        seg[name] = w_in[:, o:o + size]
        o += size
    assert o == w_in.shape[1]
    pad = jnp.zeros((w_in.shape[0], P_COLS - C_SMALL - SM_NG - 3 * NSA_HEADS), w_in.dtype)
    w = jnp.concatenate([seg["qkv"], seg["z"], seg["nq"], seg["mg"], seg["kvc"], seg["kvs"], seg["kvw"],
                         seg["a"], seg["b"], seg["ng"], pad], axis=1)
    assert w.shape[1] == P_COLS
    return w.astype(BF16)


def _tri_inverse(lmats, c):
    r = _iota((c, c), 0)
    q = _iota((c, c), 1)
    eye = (r == q).astype(F32)
    blk = min(16, c)
    shift = _log2(blk)
    same = (r >> shift) == (q >> shift)
    dmats = [jnp.where(same, lm, 0.0) for lm in lmats]
    prods = [eye - dm for dm in dmats]
    dpows = dmats
    k = 2
    while k < blk:
        dpows = [_bdot(dp, dp) for dp in dpows]
        prods = [pr + _bdot(pr, dp) for pr, dp in zip(prods, dpows)]
        k *= 2
    if c == blk:
        return prods
    mmats = [_bdot(pr, lm - dm) for pr, lm, dm in zip(prods, lmats, dmats)]
    outers = [eye - mm for mm in mmats]
    mpows = mmats
    k = 2
    while k < c // blk:
        mpows = [_bdot(mp, mp) for mp in mpows]
        outers = [ou + _bdot(ou, mp) for ou, mp in zip(outers, mpows)]
        k *= 2
    return [_bdot(ou, pr) for ou, pr in zip(outers, prods)]


def _dn_kernel(q_ref, k_ref, v_ref, z_ref, sm_ref, hq_ref, hk_ref, hv_ref, cwq_ref, cwk_ref, cwv_ref,
               hp_ref, nw_ref, s0_ref, o_ref, sout_ref,
               s_scr, xp_scr, qn_scr, kn_scr, vn_scr, gb_scr, u_scr, w_scr, qe_scr, kd_scr, a_scr, eg_scr,
               *, tb, c, t_valid, hp):
    hb = pl.program_id(1)
    t = pl.program_id(2)
    nt = pl.num_programs(2)
    dk = DN_HEAD_DIM
    nc = tb // c

    @pl.when(t == 0)
    def _():
        s_scr[...] = s0_ref[0]
        xp_scr[0, 0:8, :] = hq_ref[0]
        xp_scr[1, 0:8, :] = hk_ref[0]
        xp_scr[2, 0:8, :] = hv_ref[0]

    rows = t * tb + _iota((tb, 1), 0)
    valid = rows < t_valid

    def conv(i, raw_ref, cw_ref):
        xp_scr[i, 8:8 + tb, :] = raw_ref[0]
        acc = xp_scr[i, 8:8 + tb, :] * cw_ref[CONV_W - 1:CONV_W, :]
        for j in range(CONV_W - 1):
            acc = acc + xp_scr[i, 8 - (CONV_W - 1) + j:8 - (CONV_W - 1) + j + tb, :] * cw_ref[j:j + 1, :]
        tail = xp_scr[i, tb:tb + 8, :]
        xp_scr[i, 0:8, :] = tail
        return _silu(acc)

    qc = conv(0, q_ref, cwq_ref)
    kc = conv(1, k_ref, cwk_ref)
    vc = conv(2, v_ref, cwv_ref)
    lane = _iota((1, LANES), 1)
    sm = sm_ref[0]
    for hh in range(hp):
        h = hb * hp + hh
        cs = slice(hh * dk, (hh + 1) * dk)
        qh = qc[:, cs]
        kh = kc[:, cs]
        qn = qh * lax.rsqrt(jnp.sum(qh * qh, -1, keepdims=True) + 1e-6) * (dk ** -0.5)
        kn = kh * lax.rsqrt(jnp.sum(kh * kh, -1, keepdims=True) + 1e-6)
        a_h = jnp.sum(jnp.where(lane == SM_A + h, sm, 0.0), -1, keepdims=True)
        b_h = jnp.sum(jnp.where(lane == SM_B + h, sm, 0.0), -1, keepdims=True)
        neg_a = -jnp.exp(jnp.sum(jnp.where(lane == h, hp_ref[0:1, :], 0.0), -1, keepdims=True))
        dtb = jnp.sum(jnp.where(lane == h, hp_ref[1:2, :], 0.0), -1, keepdims=True)
        g = neg_a * _softplus(a_h + dtb)
        beta = _sigmoid(b_h)
        qn_scr[hh] = jnp.where(valid, qn, 0.0)
        kn_scr[hh] = jnp.where(valid, kn, 0.0)
        vn_scr[hh] = jnp.where(valid, vc[:, cs], 0.0)
        gb_scr[hh] = jnp.where(lane == 0, jnp.where(valid, g, 0.0), jnp.where(valid, beta, 0.0))

    r = _iota((c, c), 0)
    q = _iota((c, c), 1)
    incl = r >= q
    strict = r > q

    where = [(hh, slice(ci * c, (ci + 1) * c)) for hh in range(hp) for ci in range(nc)]
    lmats, vbs, kbes = [], [], []
    for hh, rs in where:
        qi = qn_scr[hh, rs, :]
        ki = kn_scr[hh, rs, :]
        gb = gb_scr[hh, rs, :]
        gi = gb[:, 0:1]
        bi = gb[:, 1:2]
        g_row = jnp.sum(jnp.where(r == q, gi, 0.0), 0, keepdims=True)
        gcum_col = jnp.sum(jnp.where(incl, g_row, 0.0), 1, keepdims=True)
        gcum_row = jnp.sum(jnp.where(r <= q, gi, 0.0), 0, keepdims=True)
        decay = jnp.where(incl, jnp.exp(jnp.where(incl, gcum_col - gcum_row, 0.0)), 0.0)
        kb = ki * bi
        eg = jnp.exp(gcum_col)
        g_last = gcum_col[c - 1:c, :]
        lmats.append(jnp.where(strict, _bdot_nt(kb, ki) * decay, 0.0))
        vbs.append(vn_scr[hh, rs, :] * bi)
        kbes.append(kb * eg)
        a_scr[hh, rs, :] = jnp.where(incl, _bdot_nt(qi, ki) * decay, 0.0)
        qe_scr[hh, rs, :] = qi * eg
        kd_scr[hh, rs, :] = ki * jnp.exp(g_last - gcum_col)
        e0 = rs.start // c * 8
        eg_scr[hh, e0:e0 + 8, :] = jnp.broadcast_to(jnp.exp(g_last), (8, LANES))
    tms = _tri_inverse(lmats, c)
    for (hh, rs), tm, vb, kbe in zip(where, tms, vbs, kbes):
        u_scr[hh, rs, :] = _bdot(tm, vb)
        w_scr[hh, rs, :] = _bdot(tm, kbe)

    nw = nw_ref[...]

    def chunk(ci, carry):
        r0 = pl.multiple_of(ci * c, c)
        e0 = pl.multiple_of(ci * 8, 8)
        heads = range(hp)
        rows = pl.ds(r0, c)
        ss = [s_scr[hh] for hh in heads]
        wss = [_bdot(jnp.concatenate([w_scr[hh, rows, :], qe_scr[hh, rows, :]], axis=0), ss[hh]) for hh in heads]
        v_news = [u_scr[hh, rows, :] - wss[hh][0:c] for hh in heads]
        intra = [_bdot(a_scr[hh, rows, :], v_news[hh]) for hh in heads]
        s_adds = [_bdot_tn(kd_scr[hh, rows, :], v_news[hh]) for hh in heads]
        for hh in heads:
            s_scr[hh] = ss[hh] * eg_scr[hh, pl.ds(e0, 8), :][0:1, :] + s_adds[hh]
            o = wss[hh][c:2 * c] + intra[hh]
            o = o * lax.rsqrt(jnp.mean(o * o, -1, keepdims=True) + RMS_EPS) * nw
            o_ref[0, rows, hh * dk:(hh + 1) * dk] = o * _silu(z_ref[0, rows, hh * dk:(hh + 1) * dk])
        return carry

    lax.fori_loop(0, nc, chunk, 0)

    @pl.when(t == nt - 1)
    def _():
        sout_ref[0] = s_scr[...]


def _deltanet(p3, hist, s0, conv_w, a_log, dt_bias, norm_w, *, t_valid, tb, c, hp):
    bsz, tpad, _ = p3.shape
    assert tpad % tb == 0 and tb % c == 0 and tb % 8 == 0 and DN_HEADS % hp == 0
    nt = tpad // tb
    dk = DN_HEAD_DIM
    wid = hp * dk
    cw = jnp.concatenate([conv_w, jnp.zeros((8 - CONV_W, conv_w.shape[1]), F32)], 0)
    hpar = jnp.zeros((8, LANES), F32).at[0, :DN_HEADS].set(a_log).at[1, :DN_HEADS].set(dt_bias)
    nw = norm_w.reshape(1, dk)
    nb = DN_WIDTH // wid

    tok = lambda off: pl.BlockSpec((1, tb, wid), lambda b, h, t: (b, t, off + h))
    his = lambda off: pl.BlockSpec((1, 8, wid), lambda b, h, t: (b, 0, off + h))
    cws = lambda off: pl.BlockSpec((8, wid), lambda b, h, t: (0, off + h))
    st = pl.BlockSpec((1, hp, dk, dk), lambda b, h, t: (b, h, 0, 0))
    kern = functools.partial(_dn_kernel, tb=tb, c=c, t_valid=t_valid, hp=hp)
    big = pltpu.VMEM((hp, tb, dk), F32)
    return pl.pallas_call(
        kern,
        grid=(bsz, DN_HEADS // hp, nt),
        in_specs=[tok(0), tok(nb), tok(2 * nb), tok(C_Z // wid),
                  pl.BlockSpec((1, tb, LANES), lambda b, h, t: (b, t, C_SMALL // LANES)),
                  his(0), his(nb), his(2 * nb), cws(0), cws(nb), cws(2 * nb),
                  pl.BlockSpec((8, LANES), lambda b, h, t: (0, 0)),
                  pl.BlockSpec((1, dk), lambda b, h, t: (0, 0)),
                  st],
        out_specs=[pl.BlockSpec((1, tb, wid), lambda b, h, t: (b, t, h)), st],
        out_shape=[jax.ShapeDtypeStruct((bsz, tpad, DN_WIDTH), F32),
                   jax.ShapeDtypeStruct((bsz, DN_HEADS, dk, dk), F32)],
        scratch_shapes=[pltpu.VMEM((hp, dk, dk), F32),
                        pltpu.VMEM((3, tb + 8, wid), F32),
                        big, big, big, big, big, big, big, big,
                        pltpu.VMEM((hp, tb, c), F32),
                        pltpu.VMEM((hp, (tb // c) * 8, LANES), F32)],
        compiler_params=_cparams(("parallel", "parallel", "arbitrary")),
        name="gated_deltanet",
    )(p3, p3, p3, p3, p3, hist, hist, hist, cw, cw, cw, hpar, nw, s0)


def _cmp_weights(w1, w2):
    assert NSA_KV_HEADS == 2 and CMP_BLOCK // CMP_STRIDE == 2
    w1r = w1.reshape(2, 2, CMP_STRIDE, NSA_HEAD_DIM, NSA_HEAD_DIM).astype(BF16)
    a0, a1 = w1r[:, 0], w1r[:, 1]
    z = jnp.zeros_like(a0)
    wfs = jnp.concatenate([jnp.concatenate([a0, z, a1, z], axis=-1),
                           jnp.concatenate([z, a0, z, a1], axis=-1)], axis=2)
    z2 = jnp.zeros((NSA_HEAD_DIM, NSA_HEAD_DIM), BF16)
    w2b = w2.astype(BF16)
    w2bd = jnp.concatenate([jnp.concatenate([w2b[0], z2, z2, z2], axis=1),
                            jnp.concatenate([z2, w2b[0], z2, z2], axis=1),
                            jnp.concatenate([z2, z2, w2b[1], z2], axis=1),
                            jnp.concatenate([z2, z2, z2, w2b[1]], axis=1)], axis=0)
    return wfs, w2bd


def _cmp_epi_kernel(p_ref, pos_ref, w1_ref, w2_ref, o_ref):
    pm = p_ref[0]
    n = pm.shape[0]
    nxt = pltpu.roll(pm[:, KV_COLS:2 * KV_COLS], n - 1, 0)
    b_k = _hdot(pos_ref[0:1, :], w1_ref[0])
    b_v = _hdot(pos_ref[1:2, :], w1_ref[1])
    bias = jnp.concatenate([b_k, b_k, b_v, b_v], axis=-1)
    h = pm[:, 0:KV_COLS] + nxt + bias
    o_ref[0] = _bdot(jax.nn.gelu(h), w2_ref[...])


def _cmp_epilogue(pmat, pos, w1, w2bd):
    bsz, n_sub, wid = pmat.shape
    return pl.pallas_call(
        _cmp_epi_kernel,
        grid=(bsz,),
        in_specs=[pl.BlockSpec((1, n_sub, wid), lambda b: (b, 0, 0)),
                  pl.BlockSpec(pos.shape, lambda b: (0, 0)),
                  pl.BlockSpec(w1.shape, lambda b: (0, 0, 0)),
                  pl.BlockSpec(w2bd.shape, lambda b: (0, 0))],
        out_specs=pl.BlockSpec((1, n_sub, wid // 2), lambda b: (b, 0, 0)),
        out_shape=jax.ShapeDtypeStruct((bsz, n_sub, wid // 2), F32),
        compiler_params=_cparams(("parallel",)),
        name="nsa_compress_epilogue",
    )(pmat, pos, w1, w2bd)


def _cmp_paged_kernel(pt_ref, cache_ref, w_ref, o_ref, buf, rows_k, rows_v, sem, *, npg):
    i = pl.program_id(0)
    n = pl.num_programs(0)
    spp = PAGE_SIZE // CMP_STRIDE

    def page_copy(page, slot, j):
        return pltpu.make_async_copy(cache_ref.at[page], buf.at[slot, j], sem.at[slot])

    def issue(step, slot):
        for j in range(npg):
            page_copy(pt_ref[step * npg + j], slot, j).start()

    @pl.when(i == 0)
    def _():
        issue(0, 0)

    @pl.when(i + 1 < n)
    def _():
        issue(i + 1, (i + 1) % 2)

    slot = i % 2
    for j in range(npg):
        page_copy(0, slot, j).wait()
    rows = (rows_k, rows_v)
    for hf in range(2):
        for j in range(npg):
            rows[hf][j * PAGE_SIZE:(j + 1) * PAGE_SIZE, :] = buf[slot, j, hf * LANES:(hf + 1) * LANES, :].T
    for hf in range(2):
        _project_half(lambda p, r=rows[hf]: r[pl.ds(p, npg * spp, stride=CMP_STRIDE), :], w_ref, hf, o_ref)


def _project_half(rows_at, w_ref, hf, o_ref):
    acc = jnp.zeros((o_ref.shape[-2], 2 * LANES), F32)
    for p in range(CMP_STRIDE):
        acc = acc + jnp.dot(rows_at(p).astype(BF16), w_ref[hf, p], preferred_element_type=F32)
    for rslot in range(CMP_BLOCK // CMP_STRIDE):
        c0 = rslot * KV_COLS + hf * LANES
        o_ref[..., c0:c0 + LANES] = acc[:, rslot * LANES:(rslot + 1) * LANES].reshape(o_ref.shape[:-1] + (LANES,))


def _cmp_paged(cache_t, page_table, wfs, *, npg):
    n_pool, cols, psz = cache_t.shape
    bsz, n_pages = page_table.shape
    total = bsz * n_pages
    spp = psz // CMP_STRIDE
    assert total % npg == 0 and cols == KV_COLS == 2 * LANES and psz == PAGE_SIZE
    kern = functools.partial(_cmp_paged_kernel, npg=npg)
    return pl.pallas_call(
        kern,
        grid_spec=pltpu.PrefetchScalarGridSpec(
            num_scalar_prefetch=1,
            grid=(total // npg,),
            in_specs=[pl.BlockSpec(memory_space=pl.ANY),
                      pl.BlockSpec(wfs.shape, lambda i, pt: (0, 0, 0, 0))],
            out_specs=pl.BlockSpec((npg * spp, 2 * KV_COLS), lambda i, pt: (i, 0)),
            scratch_shapes=[pltpu.VMEM((2, npg, cols, psz), F32), pltpu.VMEM((npg * psz, LANES), F32),
                            pltpu.VMEM((npg * psz, LANES), F32), pltpu.SemaphoreType.DMA((2,))]),
        out_shape=jax.ShapeDtypeStruct((total * spp, 2 * KV_COLS), F32),
        compiler_params=_cparams(("arbitrary",)),
        name="nsa_compress_paged",
    )(page_table.reshape(-1), cache_t, wfs)


def _cmp_rows_kernel(k_ref, v_ref, w_ref, o_ref):
    n_sub = o_ref.shape[1]
    for hf, ref in enumerate((k_ref, v_ref)):
        _project_half(lambda p, r=ref: r[0, pl.ds(p, n_sub, stride=CMP_STRIDE), :], w_ref, hf, o_ref)


def _cmp_rows(p3, wfs):
    bsz, seq, _ = p3.shape
    n_sub = seq // CMP_STRIDE
    half = lambda hf: pl.BlockSpec((1, seq, LANES), lambda b: (b, 0, C_KVC // LANES + hf))
    return pl.pallas_call(
        _cmp_rows_kernel,
        grid=(bsz,),
        in_specs=[half(0), half(1), pl.BlockSpec(wfs.shape, lambda b: (0, 0, 0, 0))],
        out_specs=pl.BlockSpec((1, n_sub, 2 * KV_COLS), lambda b: (b, 0, 0)),
        out_shape=jax.ShapeDtypeStruct((bsz, n_sub, 2 * KV_COLS), F32),
        compiler_params=_cparams(("parallel",)),
        name="nsa_compress_rows",
    )(p3, p3, wfs)


def _slope(head):
    return 2.0 ** (-8.0 * (head + 1) / NSA_HEADS)


def _gather_heads(q_ref, hk):
    g = NSA_GROUP
    dh = NSA_HEAD_DIM
    qs = jnp.concatenate([q_ref[0, :, (hk * g + i) * dh:(hk * g + i + 1) * dh] for i in range(g)], axis=0)
    return qs * (dh ** -0.5)


def _cmp_branch(qs, kc, vc, hk, valid_c, dist_c, tq):
    s_all = _bdot_nt(qs, kc)
    ps = []
    psum = None
    for i in range(NSA_GROUP):
        s = s_all[i * tq:(i + 1) * tq] - _slope(hk * NSA_GROUP + i) * dist_c
        s = jnp.where(valid_c, s, NEG)
        m = jnp.max(s, -1, keepdims=True)
        p = jnp.where(valid_c, jnp.exp(s - m), 0.0)
        p = p / jnp.maximum(jnp.sum(p, -1, keepdims=True), 1e-30)
        ps.append(p)
        psum = p if psum is None else psum + p
    return _bdot(jnp.concatenate(ps, axis=0), vc), psum


def _slope_features(tq):
    out = np.zeros((NSA_KV_HEADS, NSA_GROUP * tq, NSA_HEAD_DIM), np.float32)
    for hk in range(NSA_KV_HEADS):
        for g in range(NSA_GROUP):
            rem = _slope(hk * NSA_GROUP + g) * LOG2E
            for i in range(3):
                piece = float(np.float32(rem).astype(jnp.bfloat16))
                out[hk, g * tq:(g + 1) * tq, 2 * i:2 * i + 2] = piece
                rem -= piece
    return jnp.asarray(out)


def _position_features(pos):
    lo = (pos % 256).astype(F32)
    hi = (pos - pos % 256).astype(F32)
    cols = jnp.stack([hi, lo, hi, lo, hi, lo], axis=1)
    return jnp.pad(cols, ((0, 0), (0, NSA_HEAD_DIM - 6)))


_NT = (((1,), (1,)), ((), ()))


def _cmp_branch_aug(q_aug, kc_aug, vc, valid_c, tq):
    raw = lax.dot_general(q_aug, kc_aug, _NT, preferred_element_type=F32)
    bias = jnp.where(valid_c, 0.0, NEG)
    ps = []
    psum = None
    for i in range(NSA_GROUP):
        s = raw[i * tq:(i + 1) * tq] + bias
        p = jnp.where(valid_c, jnp.exp2(s - jnp.max(s, -1, keepdims=True)), 0.0)
        p = p * (1.0 / jnp.maximum(jnp.sum(p, -1, keepdims=True), 1e-30))
        ps.append(p)
        psum = p if psum is None else psum + p
    return _bdot(jnp.concatenate(ps, axis=0), vc), psum


def _flash_branch(q_aug, kf, kv_ref, hk, t_lo, t_hi, bias_fn, m_scr, acc_scr, tq):
    g = NSA_GROUP
    dh = NSA_HEAD_DIM
    m_scr[...] = jnp.full(m_scr.shape, NEG, F32)
    acc_scr[...] = jnp.zeros(acc_scr.shape, F32)
    ones = jnp.ones((KEY_TILE, dh), F32)

    def body(i, carry):
        t = t_hi - 1 - i
        k0 = pl.multiple_of(t * KEY_TILE, KEY_TILE)
        k_aug = jnp.concatenate([kv_ref[0, pl.ds(k0, KEY_TILE), hk * dh:(hk + 1) * dh], kf], axis=1).astype(BF16)
        v = kv_ref[0, pl.ds(k0, KEY_TILE), NSA_KV_WIDTH + hk * dh:NSA_KV_WIDTH + (hk + 1) * dh]
        vaug = jnp.concatenate([v, ones], axis=1).astype(BF16)
        bias = bias_fn(k0)
        k0f = k0.astype(F32)
        half = g // 2
        raws = [lax.dot_general(q_aug[h * half * tq:(h + 1) * half * tq], k_aug, _NT, preferred_element_type=F32)
                for h in range(2)]
        for h in range(2):
            ps = []
            alphas = []
            for jj in range(half):
                j = h * half + jj
                rs = slice(j * tq, (j + 1) * tq)
                shift = k0f * (_slope(hk * g + j) * LOG2E)
                s = raws[h][jj * tq:(jj + 1) * tq] + bias
                m_old = m_scr[rs, :]
                m_new = jnp.maximum(m_old, jnp.max(s, -1, keepdims=True) + shift)
                alphas.append(jnp.exp2(m_old - m_new))
                ps.append(jnp.exp2(s - jnp.concatenate([m_new - shift] * (KEY_TILE // LANES), axis=1)).astype(BF16))
                m_scr[rs, :] = m_new
            hs = slice(h * half * tq, (h + 1) * half * tq)
            pv = jnp.dot(jnp.concatenate(ps, axis=0), vaug, preferred_element_type=F32)
            acc_scr[hs, :] = jnp.concatenate(alphas, axis=0) * acc_scr[hs, :] + pv
        return carry

    lax.fori_loop(0, t_hi - t_lo, body, 0)
    acc = acc_scr[...]
    return acc[:, 0:dh] / jnp.maximum(acc[:, dh:2 * dh], 1e-30)


def _gate_combine(sm, hk, o_c, o_s, o_w, o_ref, tq):
    g = NSA_GROUP
    dh = NSA_HEAD_DIM
    outs = []
    for i in range(g):
        c0 = SM_NG + (hk * g + i) * 3
        gt = _sigmoid(sm[:, c0:c0 + 3])
        rows = slice(i * tq, (i + 1) * tq)
        outs.append(gt[:, 0:1] * o_c[rows] + gt[:, 1:2] * o_s[rows] + gt[:, 2:3] * o_w[rows])
    for i in range(0, g, 2):
        c0 = (hk * g + i) * dh
        o_ref[0, :, c0:c0 + 2 * dh] = jnp.concatenate([outs[i], outs[i + 1]], axis=-1)


def _nsa_prompt_kernel(q_ref, sm_ref, kvs_ref, kvw_ref, kc_ref, qsl_ref, kf_ref, kfc_ref, o_ref, m_scr, acc_scr,
                       *, tq, seq, n_cmp):
    dh = NSA_HEAD_DIM
    q0 = pl.program_id(1) * tq
    n_sub = kc_ref.shape[1]
    n_slc = seq // SLC_BLOCK
    qpos_i = q0 + _iota((tq, 1), 0)
    sm = sm_ref[0]
    t_hi = (q0 + tq + KEY_TILE - 1) // KEY_TILE
    t_lo_w = jnp.maximum(q0 - (WINDOW - 1), 0) // KEY_TILE

    cidx = _iota((1, n_sub), 1)
    valid_c = (cidx * CMP_STRIDE + (CMP_BLOCK - 1) <= qpos_i) & (cidx < n_cmp)
    cr = _iota((n_slc, n_sub), 1) * CMP_STRIDE
    s_st = _iota((n_slc, n_sub), 0) * SLC_BLOCK
    cover_t = jnp.where((cr < s_st + SLC_BLOCK) & (cr + (CMP_BLOCK - 1) >= s_st), 1.0, 0.0)
    srow = _iota((n_slc, 1), 0)
    qrow = q0 + _iota((1, tq), 1)
    cur = qrow >> _log2(SLC_BLOCK)
    forced = (srow == 0) | (srow == cur) | (srow == cur - 1)
    bonus = jnp.where(forced, FORCE_BONUS, 0.0)
    past_ok = srow * SLC_BLOCK <= qrow
    kf = kf_ref[...]

    q_augs, o_cs, sels = [], [], []
    for hk in range(NSA_KV_HEADS):
        q_aug = jnp.concatenate([_gather_heads(q_ref, hk) * LOG2E, qsl_ref[hk]], axis=1).astype(BF16)
        kc_aug = jnp.concatenate([kc_ref[0, :, hk * dh:(hk + 1) * dh], kfc_ref[...]], axis=1).astype(BF16)
        vc = kc_ref[0, :, NSA_KV_WIDTH + hk * dh:NSA_KV_WIDTH + (hk + 1) * dh]
        o_c, psum = _cmp_branch_aug(q_aug, kc_aug, vc, valid_c, tq)
        q_augs.append(q_aug)
        o_cs.append(o_c)
        score_t = jnp.where(past_ok, _hdot_nt(cover_t, psum) + bonus, NEG)
        sels.append(jnp.where(_rank_rows(score_t, n_slc) < N_SELECT, 1.0, 0.0).astype(BF16))

    for hk in range(NSA_KV_HEADS):
        def slc_bias(k0, sel_t=sels[hk]):
            kblk = (k0 + _iota((n_slc, KEY_TILE), 1)) >> _log2(SLC_BLOCK)
            expand = jnp.where(_iota((n_slc, KEY_TILE), 0) == kblk, 1.0, 0.0).astype(BF16)
            picked = lax.dot_general(sel_t, expand, (((0,), (0,)), ((), ())), preferred_element_type=F32)
            dist = qpos_i - (k0 + _iota((1, KEY_TILE), 1))
            return jnp.where((picked > 0.5) & (dist >= 0), 0.0, NEG)

        def win_bias(k0):
            dist = qpos_i - (k0 + _iota((1, KEY_TILE), 1))
            return jnp.where((dist >= 0) & (dist < WINDOW), 0.0, NEG)

        o_s = _flash_branch(q_augs[hk], kf, kvs_ref, hk, 0, t_hi, slc_bias, m_scr, acc_scr, tq)
        o_w = _flash_branch(q_augs[hk], kf, kvw_ref, hk, t_lo_w, t_hi, win_bias, m_scr, acc_scr, tq)
        _gate_combine(sm, hk, o_cs[hk], o_s, o_w, o_ref, tq)


def _nsa_prompt(p3, kcvc, *, tq):
    bsz, seq, _ = p3.shape
    n_sub = kcvc.shape[1]
    assert seq % KEY_TILE == 0 and seq % tq == 0 and seq % SLC_BLOCK == 0 and KEY_TILE % tq == 0
    assert seq + CMP_BLOCK < 256 * 256, "positions are split into two bf16-exact parts"
    kern = functools.partial(_nsa_prompt_kernel, tq=tq, seq=seq, n_cmp=seq // CMP_STRIDE - 1)
    rows = NSA_GROUP * tq
    qsl = _slope_features(tq)
    kf = _position_features(jnp.arange(KEY_TILE))
    kfc = _position_features(jnp.arange(n_sub) * CMP_STRIDE + (CMP_BLOCK - 1))
    full = lambda a: pl.BlockSpec(a.shape, lambda b, j: (0,) * a.ndim)
    return pl.pallas_call(
        kern,
        grid=(bsz, seq // tq),
        in_specs=[pl.BlockSpec((1, tq, NSA_WIDTH), lambda b, j: (b, j, C_NQ // NSA_WIDTH)),
                  pl.BlockSpec((1, tq, LANES), lambda b, j: (b, j, C_SMALL // LANES)),
                  pl.BlockSpec((1, seq, KV_COLS), lambda b, j: (b, 0, C_KVS // KV_COLS)),
                  pl.BlockSpec((1, seq, KV_COLS), lambda b, j: (b, 0, C_KVW // KV_COLS)),
                  pl.BlockSpec((1, n_sub, KV_COLS), lambda b, j: (b, 0, 0)),
                  full(qsl), full(kf), full(kfc)],
        out_specs=pl.BlockSpec((1, tq, NSA_WIDTH), lambda b, j: (b, j, 0)),
        out_shape=jax.ShapeDtypeStruct((bsz, seq, NSA_WIDTH), F32),
        scratch_shapes=[pltpu.VMEM((rows, LANES), F32), pltpu.VMEM((rows, 2 * NSA_HEAD_DIM), F32)],
        compiler_params=_cparams(("parallel", "arbitrary")),
        name="nsa_prompt_attention",
    )(p3, p3, p3, p3, kcvc, qsl, kf, kfc)


def _nsa_select_kernel(q_ref, kc_ref, oc_ref, sel_ref, *, tq, past, n_cmp, n_slc, n_slc_pad):
    dh = NSA_HEAD_DIM
    n_sub = kc_ref.shape[1]
    qpos_i = past + _iota((tq, 1), 0)
    qpos = qpos_i.astype(F32)
    cidx = _iota((1, n_sub), 1)
    c_end = cidx * CMP_STRIDE + (CMP_BLOCK - 1)
    valid_c = (c_end <= qpos_i) & (cidx < n_cmp)
    dist_c = qpos - c_end.astype(F32)
    cr = _iota((n_sub, n_slc_pad), 0) * CMP_STRIDE
    s_st = _iota((n_sub, n_slc_pad), 1) * SLC_BLOCK
    cover = jnp.where((cr < s_st + SLC_BLOCK) & (cr + (CMP_BLOCK - 1) >= s_st), 1.0, 0.0)
    sidx = _iota((1, n_slc_pad), 1)
    sidx_f = sidx.astype(F32)
    cur = qpos_i >> _log2(SLC_BLOCK)
    forced = (sidx == 0) | (sidx == cur) | (sidx == cur - 1)
    bonus = jnp.where(forced, FORCE_BONUS, 0.0)
    past_ok = sidx * SLC_BLOCK <= qpos_i
    lane = _iota((1, LANES), 1)
    for hk in range(NSA_KV_HEADS):
        qs = _gather_heads(q_ref, hk)
        kc = kc_ref[0, :, hk * dh:(hk + 1) * dh]
        vc = kc_ref[0, :, NSA_KV_WIDTH + hk * dh:NSA_KV_WIDTH + (hk + 1) * dh]
        o_c, psum = _cmp_branch(qs, kc, vc, hk, valid_c, dist_c, tq)
        oc_ref[0, hk] = o_c
        imp = _hdot(psum, cover)
        score = jnp.where(past_ok, imp + bonus, NEG)
        score = jnp.where(sidx < n_slc, score, -jnp.inf)
        res = jnp.zeros((tq, LANES), F32)
        for it in range(min(N_SELECT, n_slc)):
            m = jnp.max(score, -1, keepdims=True)
            idx = jnp.min(jnp.where(score == m, sidx_f, 1e9), -1, keepdims=True)
            res = jnp.where(lane == it, idx, res)
            score = jnp.where(sidx_f == idx, -jnp.inf, score)
        sel_ref[0, hk] = res.astype(I32)


def _nsa_select(ps3, kcvc, *, past, n_cmp, n_slc):
    bsz, tq, _ = ps3.shape
    n_sub = kcvc.shape[1]
    n_slc_pad = -(-n_slc // LANES) * LANES
    kern = functools.partial(_nsa_select_kernel, tq=tq, past=past, n_cmp=n_cmp, n_slc=n_slc, n_slc_pad=n_slc_pad)
    rows = NSA_GROUP * tq
    return pl.pallas_call(
        kern,
        grid=(bsz,),
        in_specs=[pl.BlockSpec((1, tq, NSA_WIDTH), lambda b: (b, 0, C_NQ // NSA_WIDTH)),
                  pl.BlockSpec((1, n_sub, KV_COLS), lambda b: (b, 0, 0))],
        out_specs=[pl.BlockSpec((1, NSA_KV_HEADS, rows, NSA_HEAD_DIM), lambda b: (b, 0, 0, 0)),
                   pl.BlockSpec((1, NSA_KV_HEADS, tq, LANES), lambda b: (b, 0, 0, 0))],
        out_shape=[jax.ShapeDtypeStruct((bsz, NSA_KV_HEADS, rows, NSA_HEAD_DIM), F32),
                   jax.ShapeDtypeStruct((bsz, NSA_KV_HEADS, tq, LANES), I32)],
        compiler_params=_cparams(("parallel",)),
        name="nsa_sample_select",
    )(ps3, kcvc)


def _joint_softmax_pv(parts, hk, tq):
    g = NSA_GROUP
    outs = []
    for j in range(g):
        rs = slice(j * tq, (j + 1) * tq)
        slope = _slope(hk * g + j)
        ss = [jnp.where(valid, s_all[rs] - slope * dist, NEG) for s_all, valid, dist, _, _ in parts]
        m = None
        for s in ss:
            mi = jnp.max(s, -1, keepdims=True)
            m = mi if m is None else jnp.maximum(m, mi)
        num = None
        den = None
        for s, (_, valid, _, v, v_t) in zip(ss, parts):
            p = jnp.where(valid, jnp.exp(s - m), 0.0)
            d = jnp.sum(p, -1, keepdims=True)
            o = _bdot_nt(p, v) if v_t else _bdot(p, v)
            num = o if num is None else num + o
            den = d if den is None else den + d
        outs.append(num / jnp.maximum(den, 1e-30))
    return jnp.concatenate(outs, axis=0)


def _nsa_sample_kernel(phys_ref, q_ref, sm_ref, kpos_ref, tail_ref, wcache_ref, wnew_ref, oc_ref, cache_ref,
                       o_ref, kbuf, vbuf, sem, *, tq, t_valid, past, n_gather):
    dh = NSA_HEAD_DIM
    b = pl.program_id(0)
    per_b = NSA_KV_HEADS * n_gather

    def page_copies(page, hk, i):
        dst = pl.ds(i * PAGE_SIZE, PAGE_SIZE)
        return (pltpu.make_async_copy(cache_ref.at[page, pl.ds(hk * dh, dh), :], kbuf.at[hk, :, dst], sem),
                pltpu.make_async_copy(cache_ref.at[page, pl.ds(NSA_KV_WIDTH + hk * dh, dh), :], vbuf.at[hk, :, dst], sem))

    for hk in range(NSA_KV_HEADS):
        for i in range(n_gather):
            for cp in page_copies(phys_ref[b * per_b + hk * n_gather + i], hk, i):
                cp.start()

    qpos_i = past + _iota((tq, 1), 0)
    qpos = qpos_i.astype(F32)
    sm = sm_ref[0]
    n_keys = n_gather * PAGE_SIZE
    per_q = n_keys // t_valid
    new_ok = _iota((1, tq), 1) < t_valid
    dist_new = qpos - (past + _iota((1, tq), 1)).astype(F32)
    n_win = wcache_ref.shape[2]
    dist_wc = qpos - (past - n_win + _iota((1, n_win), 1)).astype(F32)
    ok_wc = (dist_wc >= 0.0) & (dist_wc < float(WINDOW))
    ok_wn = (dist_new >= 0.0) & (dist_new < float(WINDOW)) & new_ok

    qss = [_gather_heads(q_ref, hk) for hk in range(NSA_KV_HEADS)]
    win = []
    for hk in range(NSA_KV_HEADS):
        kw_t = wcache_ref[0, hk * dh:(hk + 1) * dh, :]
        vw_t = wcache_ref[0, NSA_KV_WIDTH + hk * dh:NSA_KV_WIDTH + (hk + 1) * dh, :]
        kn = wnew_ref[0, :, hk * dh:(hk + 1) * dh]
        vn = wnew_ref[0, :, NSA_KV_WIDTH + hk * dh:NSA_KV_WIDTH + (hk + 1) * dh]
        win.append(_joint_softmax_pv([(_bdot(qss[hk], kw_t), ok_wc, dist_wc, vw_t, True),
                                      (_bdot_nt(qss[hk], kn), ok_wn, dist_new, vn, False)], hk, tq))

    for hk in range(NSA_KV_HEADS):
        for i in range(n_gather):
            for cp in page_copies(0, hk, i):
                cp.wait()

    g = NSA_GROUP
    grow = _iota((g, g * tq), 0)
    gcol = _iota((g, g * tq), 1)
    gi = _iota((g, 1), 0)
    tcol = _iota((1, tq), 1)
    probs = [(hk, q) for hk in range(NSA_KV_HEADS) for q in range(t_valid)]
    slopes = []
    for hk in range(NSA_KV_HEADS):
        slope = jnp.zeros((g, 1), F32)
        for j in range(g):
            slope = jnp.where(gi == j, _slope(hk * g + j), slope)
        slopes.append(slope)
    kts = [tail_ref[0, :, hk * dh:(hk + 1) * dh] for hk in range(NSA_KV_HEADS)]
    vts = [tail_ref[0, :, NSA_KV_WIDTH + hk * dh:NSA_KV_WIDTH + (hk + 1) * dh] for hk in range(NSA_KV_HEADS)]
    picks = [jnp.where(gcol == grow * tq + q, 1.0, 0.0) for _, q in probs]
    q8s = [_hdot(pick, qss[hk]) for pick, (hk, _) in zip(picks, probs)]
    raw_p = [_bdot(q8, kbuf[hk, :, q * per_q:(q + 1) * per_q]) for q8, (hk, q) in zip(q8s, probs)]
    raw_t = [_bdot_nt(q8, kts[hk]) for q8, (hk, _) in zip(q8s, probs)]
    pps, pts, dens = [], [], []
    for (hk, q), rp, rt in zip(probs, raw_p, raw_t):
        dist_p = float(past + q) - kpos_ref[0, hk][:, q * per_q:(q + 1) * per_q]
        dist_t = (q - tcol).astype(F32)
        ok_t = (dist_t >= 0.0) & new_ok
        s_p = jnp.where(dist_p >= 0.0, rp - slopes[hk] * dist_p, NEG)
        s_t = jnp.where(ok_t, rt - slopes[hk] * dist_t, NEG)
        m = jnp.maximum(jnp.max(s_p, -1, keepdims=True), jnp.max(s_t, -1, keepdims=True))
        p_p = jnp.where(dist_p >= 0.0, jnp.exp(s_p - m), 0.0)
        p_t = jnp.where(ok_t, jnp.exp(s_t - m), 0.0)
        pps.append(p_p)
        pts.append(p_t)
        dens.append(jnp.sum(p_p, -1, keepdims=True) + jnp.sum(p_t, -1, keepdims=True))
    o_qs = [(_bdot_nt(p_p, vbuf[hk, :, q * per_q:(q + 1) * per_q]) + _bdot(p_t, vts[hk])) / jnp.maximum(den, 1e-30)
            for (hk, q), p_p, p_t, den in zip(probs, pps, pts, dens)]
    backs = [_hdot_tn(pick, o_q) for pick, o_q in zip(picks, o_qs)]
    for hk in range(NSA_KV_HEADS):
        o_s = backs[hk * t_valid]
        for q in range(1, t_valid):
            o_s = o_s + backs[hk * t_valid + q]
        _gate_combine(sm, hk, oc_ref[0, hk], o_s, win[hk], o_ref, tq)


def _nsa_sample(ps3, o_c, phys, kpos, cache_t, win_t, *, t_valid, past):
    bsz, tq, _ = ps3.shape
    n_gather = t_valid * N_SELECT
    rows = NSA_GROUP * tq
    n_keys = n_gather * PAGE_SIZE
    kern = functools.partial(_nsa_sample_kernel, tq=tq, t_valid=t_valid, past=past, n_gather=n_gather)
    return pl.pallas_call(
        kern,
        grid_spec=pltpu.PrefetchScalarGridSpec(
            num_scalar_prefetch=1,
            grid=(bsz,),
            in_specs=[pl.BlockSpec((1, tq, NSA_WIDTH), lambda b, ph: (b, 0, C_NQ // NSA_WIDTH)),
                      pl.BlockSpec((1, tq, LANES), lambda b, ph: (b, 0, C_SMALL // LANES)),
                      pl.BlockSpec((1, NSA_KV_HEADS, 1, n_keys), lambda b, ph: (b, 0, 0, 0)),
                      pl.BlockSpec((1, tq, KV_COLS), lambda b, ph: (b, 0, C_KVS // KV_COLS)),
                      pl.BlockSpec((1,) + win_t.shape[1:], lambda b, ph: (b, 0, 0)),
                      pl.BlockSpec((1, tq, KV_COLS), lambda b, ph: (b, 0, C_KVW // KV_COLS)),
                      pl.BlockSpec((1, NSA_KV_HEADS, rows, NSA_HEAD_DIM), lambda b, ph: (b, 0, 0, 0)),
                      pl.BlockSpec(memory_space=pl.ANY)],
            out_specs=pl.BlockSpec((1, tq, NSA_WIDTH), lambda b, ph: (b, 0, 0)),
            scratch_shapes=[pltpu.VMEM((NSA_KV_HEADS, NSA_HEAD_DIM, n_keys), F32),
                            pltpu.VMEM((NSA_KV_HEADS, NSA_HEAD_DIM, n_keys), F32),
                            pltpu.SemaphoreType.DMA(())]),
        out_shape=jax.ShapeDtypeStruct((bsz, tq, NSA_WIDTH), F32),
        compiler_params=_cparams(("arbitrary",)),
        name="nsa_sample_attention",
    )(phys, ps3, ps3, kpos, ps3, win_t, ps3, o_c, cache_t)


def _rows_transposed(cache):
    nd = cache.ndim
    perm = tuple(range(nd - 4)) + (nd - 3, nd - 2, nd - 1, nd - 4)
    t = jnp.transpose(cache, perm)
    return t.reshape(t.shape[:nd - 4] + (KV_COLS, cache.shape[nd - 4]))


def _prompt_mixers(x, w_r, conv_w, a_log, dt_bias, norm_w, cmp_wf, cmp_w2bd, cmp_pos, cmp_w1, tl):
    bsz, seq, _ = x.shape
    p = _matmul(x.reshape(bsz * seq, D_MODEL), w_r, tl["proj_tm"], tl["proj_tn"])
    p3 = p.reshape(bsz, seq, P_COLS)
    hist = jnp.zeros((bsz, 8, 3 * DN_WIDTH), F32)
    s0 = jnp.zeros((bsz, DN_HEADS, DN_HEAD_DIM, DN_HEAD_DIM), F32)
    o_dn, s_new = _deltanet(p3, hist, s0, conv_w, a_log, dt_bias, norm_w, t_valid=seq, tb=tl["dn_tb"], c=DN_CHUNK,
                            hp=tl["dn_heads"])
    assert seq % CMP_STRIDE == 0
    kcvc = _cmp_epilogue(_cmp_rows(p3, cmp_wf), cmp_pos.reshape(2, -1), cmp_w1, cmp_w2bd)
    o_nsa = _nsa_prompt(p3, kcvc, tq=tl["nsa_tq"])
    return p3, o_dn, s_new, o_nsa


def _sample_mixers(x, cache_cmp, cache_slc, win_buf, s0, conv_buf, page_table, w_r, conv_w, a_log, dt_bias,
                   norm_w, cmp_wf, cmp_w2bd, cmp_pos, cmp_w1, tl):
    bsz, t, _ = x.shape
    tq = 8
    n_pages = page_table.shape[1]
    past = n_pages * PAGE_SIZE
    assert t <= tq and t <= SLC_BLOCK and past % SLC_BLOCK == 0 and cache_cmp.shape[1] == PAGE_SIZE
    assert (past + t) // CMP_STRIDE * CMP_STRIDE == past, "new rows never complete a compression sub-block"
    ps = _matmul(x.reshape(bsz * t, D_MODEL), w_r, bsz * t, tl["proj_tn"]).reshape(bsz, t, P_COLS)
    ps3 = jnp.pad(ps, ((0, 0), (0, tq - t), (0, 0)))
    hist = jnp.pad(conv_buf, ((0, 0), (8 - (CONV_W - 1), 0), (0, 0)))
    o_dn, s_new = _deltanet(ps3, hist, s0, conv_w, a_log, dt_bias, norm_w, t_valid=t, tb=tq, c=tq, hp=DN_HEADS)
    n_sub = past // CMP_STRIDE
    pmat = _cmp_paged(_rows_transposed(cache_cmp), page_table, cmp_wf, npg=tl["cmp_pages"])
    kcvc = _cmp_epilogue(pmat.reshape(bsz, n_sub, -1), cmp_pos.reshape(2, -1), cmp_w1, cmp_w2bd)
    n_past_blocks = past // SLC_BLOCK
    o_c, sel = _nsa_select(ps3, kcvc, past=past, n_cmp=n_sub - 1, n_slc=n_past_blocks + 1)
    sel = sel[:, :, :t, :N_SELECT]
    bpp = PAGE_SIZE // SLC_BLOCK
    jp = jnp.minimum(sel, n_past_blocks - 1)
    page = jp // bpp
    phys = page_table[jnp.arange(bsz)[:, None, None, None], page]
    row = jnp.arange(PAGE_SIZE)
    in_blk = (row // SLC_BLOCK == (jp % bpp)[..., None]) & (sel < n_past_blocks)[..., None]
    kpos = jnp.where(in_blk, (page[..., None] * PAGE_SIZE + row).astype(F32), 1e9)
    kpos = kpos.reshape(bsz, NSA_KV_HEADS, 1, t * N_SELECT * PAGE_SIZE)
    o_nsa = _nsa_sample(ps3, o_c, phys.reshape(-1).astype(I32), kpos, _rows_transposed(cache_slc),
                        _rows_transposed(win_buf), t_valid=t, past=past)
    return ps, o_dn, s_new, o_nsa


def _layer_norm(x, g, b):
    xc = x - jnp.mean(x, -1, keepdims=True)
    var = jnp.mean(xc * xc, -1, keepdims=True)
    return xc * lax.rsqrt(var + LN_EPS) * g + b


def _rank_rows(v, n):
    ri = _iota(v.shape, 0)
    rank = jnp.zeros(v.shape, F32)
    for rp in range(n):
        row = v[rp:rp + 1, :]
        beats = (row > v) | ((row == v) & (rp < ri))
        rank = rank + jnp.where(beats, 1.0, 0.0)
    return rank


def _post_mixer_kernel(x_ref, odn_ref, onsa_ref, gdn_ref, gnsa_ref, wo_ref, g_ref, b_ref, wr_ref, br_ref,
                       x1_ref, xp_ref, idx_ref, wt_ref, pos_ref, cnt_ref, run_scr, *, tm, alpha):
    i = pl.program_id(0)

    @pl.when(i == 0)
    def _():
        run_scr[...] = jnp.zeros(run_scr.shape, F32)

    h = _sigmoid(gdn_ref[0]) * odn_ref[...] + _sigmoid(gnsa_ref[0]) * onsa_ref[...]
    x1 = _layer_norm(alpha * x_ref[...] + _bdot(h, wo_ref[...]), g_ref[...], b_ref[...])
    x1_ref[...] = x1
    bits = pltpu.bitcast(x1.astype(BF16).astype(F32), jnp.uint32)
    half = x1.shape[1] // 2
    xp_ref[...] = (bits[:, :half] >> 16) | (bits[:, half:] & jnp.uint32(0xFFFF0000))

    ne = N_EXPERTS
    per = ne // N_GROUPS
    scores = _sigmoid(_hdot_nt(wr_ref[...], x1))
    s3 = (scores + br_ref[...]).reshape(N_GROUPS, per, tm)
    e3 = _iota((N_GROUPS, per, tm), 1).astype(F32)
    g1 = jnp.max(s3, axis=1, keepdims=True)
    first = jnp.min(jnp.where(s3 == g1, e3, float(per)), axis=1, keepdims=True)
    g2 = jnp.max(jnp.where(e3 == first, -jnp.inf, s3), axis=1, keepdims=True)
    grank = _rank_rows((g1 + g2).reshape(N_GROUPS, tm), N_GROUPS)
    keep = (grank < TOPK_GROUPS).reshape(N_GROUPS, 1, tm)
    selm = jnp.where(keep, s3, NEG).reshape(ne, tm)
    erank = _rank_rows(selm, ne)
    ei = _iota((ne, tm), 0).astype(F32)
    chosen = jnp.where(erank < TOP_K, 1.0, 0.0)
    tr = _iota((tm, tm), 0)
    tc = _iota((tm, tm), 1)
    before = jnp.where(tr < tc, 1.0, 0.0)
    pos_full = _bdot(chosen, before) + run_scr[:, 0:1]
    idx_rows, w_rows, pos_rows = [], [], []
    for k in range(TOP_K):
        hit = erank == float(k)
        idx_rows.append(jnp.sum(jnp.where(hit, ei, 0.0), 0, keepdims=True))
        w_rows.append(jnp.sum(jnp.where(hit, scores, 0.0), 0, keepdims=True))
        pos_rows.append(jnp.sum(jnp.where(hit, pos_full, 0.0), 0, keepdims=True))
    wsum = w_rows[0]
    for k in range(1, TOP_K):
        wsum = wsum + w_rows[k]
    zero = jnp.zeros((8 - TOP_K, tm), F32)
    idx_ref[...] = jnp.concatenate(idx_rows + [zero], 0).astype(I32)
    wt_ref[...] = jnp.concatenate([w / wsum * ROUTED_SCALE for w in w_rows] + [zero], 0)
    pos_ref[...] = jnp.concatenate(pos_rows + [zero], 0).astype(I32)
    run_scr[...] = run_scr[...] + jnp.sum(chosen, 1, keepdims=True)
    cnt_ref[...] = run_scr[...]


def _post_mixer(x, o_dn, o_nsa, p3, w_out_bf16, ln_g, ln_b, w_router_t, b_router, *, tm, alpha):
    n, d = x.shape
    assert n % tm == 0
    bsz, seq, _ = p3.shape
    assert seq % tm == 0 or tm % seq == 0
    if seq % tm == 0:
        per_b = seq // tm
        gspec = lambda c: pl.BlockSpec((1, tm, d), lambda i: (i // per_b, i % per_b, c))
        p_in = p3
    else:
        p_in = p3.reshape(1, n, P_COLS)
        gspec = lambda c: pl.BlockSpec((1, tm, d), lambda i: (0, i, c))
    tok = pl.BlockSpec((tm, d), lambda i: (i, 0))
    full = lambda a: pl.BlockSpec(a.shape, lambda i: (0,) * a.ndim)
    rt = pl.BlockSpec((8, tm), lambda i: (0, i))
    kern = functools.partial(_post_mixer_kernel, tm=tm, alpha=alpha)
    g2 = ln_g.reshape(1, d)
    b2 = ln_b.reshape(1, d)
    br = b_router.reshape(N_EXPERTS, 1)
    return pl.pallas_call(
        kern,
        grid=(n // tm,),
        in_specs=[tok, tok, tok, gspec(C_MG // d), gspec(C_MG // d + 1), full(w_out_bf16), full(g2), full(b2),
                  full(w_router_t), full(br)],
        out_specs=[tok, pl.BlockSpec((tm, d // 2), lambda i: (i, 0)), rt, rt, rt,
                   pl.BlockSpec((N_EXPERTS, LANES), lambda i: (0, 0))],
        out_shape=[jax.ShapeDtypeStruct((n, d), F32), jax.ShapeDtypeStruct((n, d // 2), jnp.uint32),
                   jax.ShapeDtypeStruct((8, n), I32),
                   jax.ShapeDtypeStruct((8, n), F32), jax.ShapeDtypeStruct((8, n), I32),
                   jax.ShapeDtypeStruct((N_EXPERTS, LANES), F32)],
        scratch_shapes=[pltpu.VMEM((N_EXPERTS, LANES), F32)],
        compiler_params=_cparams(("arbitrary",)),
        name="merge_outproj_ln_router",
    )(x, o_dn, o_nsa, p_in, p_in, w_out_bf16, g2, b2, w_router_t, br)


def _slot_kernel(ps_ref, idx_ref, pos_ref, slot_ref):
    idx = idx_ref[...]
    acc = pos_ref[...]
    for e in range(N_EXPERTS):
        acc = acc + jnp.where(idx == e, ps_ref[e], 0)
    slot_ref[...] = jnp.where(_iota(idx.shape, 0) < TOP_K, acc, 0)


def _slots(pad_start, idx, pos):
    n = idx.shape[1]
    blk = pl.BlockSpec((8, n), lambda i, ps: (0, 0))
    return pl.pallas_call(
        _slot_kernel,
        grid_spec=pltpu.PrefetchScalarGridSpec(num_scalar_prefetch=1, grid=(1,), in_specs=[blk, blk], out_specs=blk),
        out_shape=jax.ShapeDtypeStruct((8, n), I32),
        compiler_params=_cparams(("arbitrary",)),
        name="moe_slots",
    )(pad_start, idx, pos)


def _dispatch_kernel(zb_ref, nu_ref, slot_ref, x_ref, xs_ref, zero_scr, sem, zsem, *, tm, blk):
    @pl.when(pl.program_id(0) == 0)
    def _():
        zero_scr[...] = jnp.zeros(zero_scr.shape, zero_scr.dtype)
        n_blocks = xs_ref.shape[0] // blk

        def zero_copy(b):
            return pltpu.make_async_copy(zero_scr, xs_ref.at[pl.ds(pl.multiple_of(b * blk, blk), blk)], zsem)

        for e in range(N_EXPERTS):
            zero_copy(zb_ref[e]).start()
        for e in range(N_EXPERTS):
            zero_copy(0).wait()
        for t in range(n_blocks - N_EXPERTS, n_blocks):
            @pl.when(t >= nu_ref[0])
            def _(t=t):
                zero_copy(t).start()
        for t in range(n_blocks - N_EXPERTS, n_blocks):
            @pl.when(t >= nu_ref[0])
            def _():
                zero_copy(0).wait()

    def row_copy(r, s):
        return pltpu.make_async_copy(x_ref.at[pl.ds(r, 1)], xs_ref.at[pl.ds(s, 1)], sem)

    def issue(r, carry):
        for k in range(TOP_K):
            row_copy(r, slot_ref[k, r]).start(priority=k % 2)
        return carry

    lax.fori_loop(0, tm, issue, 0, unroll=8)

    def drain(r, carry):
        for k in range(TOP_K):
            row_copy(0, 0).wait()
        return carry

    lax.fori_loop(0, tm, drain, 0, unroll=8)


def _dispatch(x1, slot, zero_blocks, n_used, n_slots, *, tm, blk):
    n, d = x1.shape
    assert n % tm == 0 and n_slots % blk == 0 and n_slots // blk >= N_EXPERTS
    kern = functools.partial(_dispatch_kernel, tm=tm, blk=blk)
    return pl.pallas_call(
        kern,
        grid_spec=pltpu.PrefetchScalarGridSpec(
            num_scalar_prefetch=2,
            grid=(n // tm,),
            in_specs=[pl.BlockSpec((8, tm), lambda i, zb, nu: (0, i), memory_space=pltpu.SMEM),
                      pl.BlockSpec((tm, d), lambda i, zb, nu: (i, 0))],
            out_specs=pl.BlockSpec(memory_space=pl.ANY),
            scratch_shapes=[pltpu.VMEM((blk, d), x1.dtype), pltpu.SemaphoreType.DMA(()),
                            pltpu.SemaphoreType.DMA(())]),
        out_shape=jax.ShapeDtypeStruct((n_slots, d), x1.dtype),
        compiler_params=_cparams(("arbitrary",)),
        name="moe_dispatch",
    )(zero_blocks, n_used, slot, x1)


def _expert_kernel(be_ref, nu_ref, x_ref, wg_ref, wu_ref, wd_ref, y_ref, wg_b, wu_b, wd_b):
    i = pl.program_id(0)

    @pl.when((i == 0) | (be_ref[i] != be_ref[jnp.maximum(i - 1, 0)]))
    def _():
        wg_b[...] = wg_ref[0].astype(BF16)
        wu_b[...] = wu_ref[0].astype(BF16)
        wd_b[...] = wd_ref[0].astype(BF16)

    @pl.when(i < nu_ref[0])
    def _():
        w = x_ref[...]
        x = jnp.concatenate([pltpu.bitcast(w << 16, F32), pltpu.bitcast(w & jnp.uint32(0xFFFF0000), F32)],
                            axis=1).astype(BF16)
        hg = jnp.dot(x, wg_b[...], preferred_element_type=F32)
        hu = jnp.dot(x, wu_b[...], preferred_element_type=F32)
        y_ref[...] = jnp.dot((_silu(hg) * hu).astype(BF16), wd_b[...], preferred_element_type=F32)

    @pl.when(i >= nu_ref[0])
    def _():
        y_ref[...] = jnp.zeros(y_ref.shape, F32)


def _experts(xs, blk_exp, n_used, w_gate, w_up, w_down, *, blk):
    n_slots, dpk = xs.shape
    d, de = w_gate.shape[1:]
    assert dpk * 2 == d
    n_blocks = n_slots // blk
    return pl.pallas_call(
        _expert_kernel,
        grid_spec=pltpu.PrefetchScalarGridSpec(
            num_scalar_prefetch=2,
            grid=(n_blocks,),
            in_specs=[pl.BlockSpec((blk, dpk), lambda i, be, nu: (jnp.maximum(jnp.minimum(i, nu[0] - 1), 0), 0)),
                      pl.BlockSpec((1, d, de), lambda i, be, nu: (be[i], 0, 0)),
                      pl.BlockSpec((1, d, de), lambda i, be, nu: (be[i], 0, 0)),
                      pl.BlockSpec((1, de, d), lambda i, be, nu: (be[i], 0, 0))],
            out_specs=pl.BlockSpec((blk, d), lambda i, be, nu: (i, 0)),
            scratch_shapes=[pltpu.VMEM((d, de), BF16), pltpu.VMEM((d, de), BF16), pltpu.VMEM((de, d), BF16)]),
        out_shape=jax.ShapeDtypeStruct((n_slots, d), F32),
        compiler_params=_cparams(("arbitrary",)),
        name="moe_experts",
    )(blk_exp, n_used, xs, w_gate, w_up, w_down)


def _combine_kernel(slot_ref, x_ref, w_ref, ys_ref, wsg_ref, wsu_ref, wsd_ref, g_ref, b_ref, o_ref, buf, sem,
                    *, tm, alpha):
    def row_copy(s, k, r):
        return pltpu.make_async_copy(ys_ref.at[pl.ds(s, 1)], buf.at[k, pl.ds(r, 1)], sem)

    def issue(r, carry):
        for k in range(TOP_K):
            row_copy(slot_ref[k, r], k, r).start(priority=k % 2)
        return carry

    lax.fori_loop(0, tm, issue, 0, unroll=8)
    x = x_ref[...]
    xb = x.astype(BF16)
    hs = _silu(jnp.dot(xb, wsg_ref[...], preferred_element_type=F32)) * jnp.dot(xb, wsu_ref[...],
                                                                               preferred_element_type=F32)
    acc = alpha * x + _bdot(hs, wsd_ref[...])

    def drain(r, carry):
        for k in range(TOP_K):
            row_copy(0, k, 0).wait()
        return carry

    lax.fori_loop(0, tm, drain, 0, unroll=8)
    w = w_ref[...]
    for k in range(TOP_K):
        acc = acc + w[:, k:k + 1] * buf[k]
    o_ref[...] = _layer_norm(acc, g_ref[...], b_ref[...])


def _combine(x1, slot, w_tok, ys, ws_gate, ws_up, ws_down, ln_g, ln_b, *, tm, alpha):
    n, d = x1.shape
    assert n % tm == 0
    kern = functools.partial(_combine_kernel, tm=tm, alpha=alpha)
    full = lambda a: pl.BlockSpec(a.shape, lambda i: (0,) * a.ndim)
    g2 = ln_g.reshape(1, d)
    b2 = ln_b.reshape(1, d)
    return pl.pallas_call(
        kern,
        grid=(n // tm,),
        in_specs=[pl.BlockSpec((8, tm), lambda i: (0, i), memory_space=pltpu.SMEM),
                  pl.BlockSpec((tm, d), lambda i: (i, 0)),
                  pl.BlockSpec((tm, 8), lambda i: (i, 0)),
                  pl.BlockSpec(memory_space=pl.ANY),
                  full(ws_gate), full(ws_up), full(ws_down), full(g2), full(b2)],
        out_specs=pl.BlockSpec((tm, d), lambda i: (i, 0)),
        out_shape=jax.ShapeDtypeStruct((n, d), F32),
        scratch_shapes=[pltpu.VMEM((TOP_K, tm, d), F32), pltpu.SemaphoreType.DMA(())],
        compiler_params=_cparams(("arbitrary",)),
        name="moe_combine_ln",
    )(slot, x1, w_tok, ys, ws_gate, ws_up, ws_down, g2, b2)


def _moe_layer(x1, xp, idx, wts, pos, counts, w_gate, w_up, w_down, ws_gate, ws_up, ws_down, ln_g, ln_b,
               *, blk, tm_d, tm_c, alpha):
    n = x1.shape[0]
    cnt = counts[:, 0].astype(I32)
    padded = (cnt + blk - 1) // blk * blk
    pad_end = jnp.cumsum(padded)
    slot = _slots((pad_end - padded).astype(I32), idx, pos)
    n_blocks = -(-(n * TOP_K) // blk) + N_EXPERTS
    blk_exp = jnp.minimum(jnp.sum(pad_end[None, :] <= (jnp.arange(n_blocks) * blk)[:, None], axis=1),
                          N_EXPERTS - 1).astype(I32)
    n_used = (pad_end[-1:] // blk).astype(I32)
    empty = padded == 0
    zero_blocks = jnp.where(empty, n_blocks - jnp.cumsum(empty), pad_end // blk - 1).astype(I32)
    xs = _dispatch(xp, slot, zero_blocks, n_used, n_blocks * blk, tm=tm_d, blk=blk)
    ys = _experts(xs, blk_exp, n_used, w_gate, w_up, w_down, blk=blk)
    return _combine(x1, slot, wts.T, ys, ws_gate.astype(BF16), ws_up.astype(BF16), ws_down.astype(BF16),
                    ln_g, ln_b, tm=tm_c, alpha=alpha)


def kernel(x_prompt, x_sample, cache_cmp_kv, cache_slc_kv, cache_win_kv, state_delta_S, state_delta_conv, page_table, w_in, dn_conv_w, dn_A_log, dn_dt_bias, dn_norm_w, nsa_cmp_w1, nsa_cmp_pos, nsa_cmp_w2, w_out, ln1_g, ln1_b, w_router, b_router, w_exp_gate, w_exp_up, w_exp_down, w_sh_gate, w_sh_up, w_sh_down, ln2_g, ln2_b):
    depth = w_in.shape[0]
    assert depth == 1
    alpha = (2.0 * depth) ** 0.25
    bsz, seq, d = x_prompt.shape
    sb, st, _ = x_sample.shape
    tl = _tiles(bsz * seq, seq, sb * st)
    w_r = _reorder_w_in(w_in[0])
    wf, w2bd = _cmp_weights(nsa_cmp_w1[0], nsa_cmp_w2[0])
    mix_w = (w_r, dn_conv_w[0], dn_A_log[0], dn_dt_bias[0], dn_norm_w[0], wf, w2bd, nsa_cmp_pos[0], nsa_cmp_w1[0])
    p3, o_dn, s_p, o_nsa = _prompt_mixers(x_prompt, *mix_w, tl)
    ps, o_dn_s, s_s, o_nsa_s = _sample_mixers(x_sample, cache_cmp_kv[0], cache_slc_kv[0], cache_win_kv[0],
                                              state_delta_S[0], state_delta_conv[0], page_table, *mix_w, tl)
    wo = w_out[0].astype(BF16)
    wrt = w_router[0].T

    def ffn(x2, o_dn2, o_nsa2, p_any, tm, blk, tm_d, tm_c):
        x1, xp, idx, wts, pos, counts = _post_mixer(x2, o_dn2, o_nsa2, p_any, wo, ln1_g[0], ln1_b[0], wrt,
                                                    b_router[0], tm=tm, alpha=alpha)
        return _moe_layer(x1, xp, idx, wts, pos, counts, w_exp_gate[0], w_exp_up[0], w_exp_down[0],
                          w_sh_gate[0], w_sh_up[0], w_sh_down[0], ln2_g[0], ln2_b[0],
                          blk=blk, tm_d=tm_d, tm_c=tm_c, alpha=alpha)

    y_p = ffn(x_prompt.reshape(-1, d), o_dn.reshape(-1, d), o_nsa.reshape(-1, d), p3,
              tl["post_tm"], tl["moe_blk"], tl["moe_tm_dispatch"], tl["moe_tm_combine"])
    y_s = ffn(x_sample.reshape(-1, d), o_dn_s[:, :st].reshape(-1, d), o_nsa_s[:, :st].reshape(-1, d), ps,
              tl["sample_tm"], tl["sample_moe_blk"], tl["sample_tm"], tl["sample_tm"])

    kv_shape = (2, NSA_KV_HEADS, NSA_HEAD_DIM)

    def kv_rows(pp, c0):
        return pp[:, :, c0:c0 + KV_COLS].reshape(pp.shape[:2] + kv_shape)

    nconv = CONV_W - 1
    conv_p = jnp.concatenate([jnp.zeros((bsz, nconv, 3 * DN_WIDTH), F32), p3[:, :, :3 * DN_WIDTH]], 1)[:, -nconv:]
    conv_s = jnp.concatenate([state_delta_conv[0], ps[:, :, :3 * DN_WIDTH]], 1)[:, -nconv:]
    past = page_table.shape[1] * PAGE_SIZE
    win_s = jnp.concatenate([cache_win_kv[0], kv_rows(ps, C_KVW)], 1)[:, -min(WINDOW, past + st):]
    return (y_p.reshape(x_prompt.shape), y_s.reshape(x_sample.shape),
            kv_rows(p3, C_KVC)[None], kv_rows(p3, C_KVS)[None], kv_rows(p3, C_KVW)[:, -min(WINDOW, seq):][None],
            s_p[None], conv_p[None],
            kv_rows(ps, C_KVC)[None], kv_rows(ps, C_KVS)[None], win_s[None], s_s[None], conv_s[None])
```
